```python
import math
import jax, jax.numpy as jnp
from jax import lax
import numpy as np

D_MODEL = 1024
BATCH = 2
SEQ = 8192
DEPTH = 2
DEC_BATCH = 128
DEC_SEQ = 8
PAST_LEN = 2048
PAGE_SIZE = 128

RET_HEADS = 4
RET_DK = 128
RET_DV = 256
RET_QK = RET_HEADS * RET_DK
RET_V = RET_HEADS * RET_DV
RET_CHUNK = 128
ROPE_BASE = 10000.0
POOL_WINDOWS = (2, 4, 8, 16)
POOL_GROUPS = 4
POOL_GDIM = 128
POOL_WIDTH = POOL_GROUPS * POOL_GDIM
POOL_BUF = 15
NSA_HEADS = 8
NSA_KV_HEADS = 2
NSA_GROUP = NSA_HEADS // NSA_KV_HEADS
NSA_DH = 64
NSA_WIDTH = NSA_HEADS * NSA_DH
NSA_KV = NSA_KV_HEADS * NSA_DH
L_CMP = 32
L_SEL = 64
N_SEL = 16
WINDOW = 512
Q_BLOCK = 128
FORCE_SCORE = 1e4
N_BUCKETS = 32
MAX_DIST = 128
EPS = 1e-6
IN_SIZES = (RET_QK, RET_QK, RET_V, RET_V,
            POOL_WIDTH, POOL_WIDTH,
            NSA_WIDTH, NSA_KV, NSA_KV, NSA_KV, NSA_KV, NSA_KV, NSA_KV, 3 * NSA_HEADS, NSA_WIDTH,
            3 * D_MODEL)
D_IN = sum(IN_SIZES)

kernel_name = "hybrid_retention_pool_nsa_decoder_step"


def rms_norm(x, g):
    xf = x.astype(jnp.float32)
    y = xf * lax.rsqrt(jnp.mean(xf * xf, axis=-1, keepdims=True) + EPS) * g.astype(jnp.float32)
    return y.astype(x.dtype)


def masked_softmax(s, valid):
    s = jnp.where(valid, s, -jnp.inf)
    m = jnp.max(s, axis=-1, keepdims=True)
    m = jnp.where(jnp.isfinite(m), m, 0.0)
    e = jnp.where(valid, jnp.exp(s - m), 0.0)
    return e / jnp.maximum(jnp.sum(e, axis=-1, keepdims=True), 1e-30)


def t5_bucket(dist):
    n = jnp.maximum(dist, 0)
    exact = N_BUCKETS // 2
    nf = jnp.maximum(n, 1).astype(jnp.float32)
    large = exact + (jnp.log(nf / exact) / math.log(MAX_DIST / exact) * (N_BUCKETS - exact)).astype(jnp.int32)
    large = jnp.minimum(large, N_BUCKETS - 1)
    return jnp.where(n < exact, n, large)


def rotary(x, pos):
    d = x.shape[-1]
    inv = ROPE_BASE ** (-jnp.arange(0, d, 2, dtype=jnp.float32) / d)
    ang = pos.astype(jnp.float32)[:, None] * inv[None, :]
    cos = jnp.cos(ang)[None, :, None, :]
    sin = jnp.sin(ang)[None, :, None, :]
    x1, x2 = x[..., 0::2], x[..., 1::2]
    return jnp.stack([x1 * cos - x2 * sin, x1 * sin + x2 * cos], axis=-1).reshape(x.shape)


def retention(q, k, v, s0, pos0):
    n, t, h, _ = q.shape
    f32 = jnp.float32
    pos = pos0 + jnp.arange(t)
    q = rotary(q.astype(f32), pos)
    k = rotary(k.astype(f32), pos) * (RET_DK ** -0.5)
    v = v.astype(f32)
    c = RET_CHUNK if t % RET_CHUNK == 0 else t
    nc = t // c
    log_g = jnp.log1p(-jnp.exp2(-5.0 - jnp.arange(h, dtype=f32)))
    idx = jnp.arange(c, dtype=f32)
    rel = idx[:, None] - idx[None, :]
    dmask = jnp.where(rel[None] >= 0, jnp.exp(jnp.maximum(rel, 0.0)[None] * log_g[:, None, None]), 0.0)
    cross = jnp.exp((idx + 1.0)[None, :] * log_g[:, None])
    tail = jnp.exp((c - 1.0 - idx)[None, :] * log_g[:, None])
    chunk_decay = jnp.exp(c * log_g)

    def to_chunks(a):
        return a.reshape(n, nc, c, h, a.shape[-1]).transpose(1, 0, 3, 2, 4)

    def step(s, xs):
        qc, kc, vc = xs
        att = jnp.einsum('nhid,nhjd->nhij', qc, kc) * dmask
        o = jnp.einsum('nhij,nhje->nhie', att, vc) + jnp.einsum('nhid,nhde->nhie', qc, s) * cross[None, :, :, None]
        s = s * chunk_decay[None, :, None, None] + jnp.einsum('nhjd,nhje->nhde', kc * tail[None, :, :, None], vc)
        return s, o

    s_new, o = lax.scan(step, s0.astype(f32), (to_chunks(q), to_chunks(k), to_chunks(v)))
    o = o.transpose(1, 0, 3, 2, 4).reshape(n, t, h, RET_DV)
    o = o * lax.rsqrt(jnp.mean(o * o, axis=-1, keepdims=True) + EPS)
    return o.reshape(n, t, h * RET_DV), s_new


def pool_mix(u, buf, pos0, w_pool, pool_scale):
    n, t, cw = u.shape
    ext = jnp.concatenate([buf.astype(u.dtype), u], axis=1)
    extf = ext.astype(jnp.float32)
    cs = jnp.pad(jnp.cumsum(extf, axis=1), ((0, 0), (1, 0), (0, 0)))
    cnt_pos = pos0 + jnp.arange(t)
    means = []
    for g, w in enumerate(POOL_WINDOWS):
        sl = slice(g * POOL_GDIM, (g + 1) * POOL_GDIM)
        hi = cs[:, POOL_BUF + 1:POOL_BUF + 1 + t, sl]
        lo = cs[:, POOL_BUF + 1 - w:POOL_BUF + 1 - w + t, sl]
        cnt = jnp.minimum(cnt_pos + 1, w).astype(jnp.float32)
        means.append((hi - lo) / cnt[None, :, None])
    d = (jnp.concatenate(means, axis=-1) - extf[:, POOL_BUF:]).reshape(n, t, POOL_GROUPS, POOL_GDIM)
    y = jnp.einsum('ntgc,gce->ntge', d, w_pool.astype(jnp.float32)).reshape(n, t, cw)
    y = y * pool_scale.astype(jnp.float32)
    return y, ext[:, ext.shape[1] - POOL_BUF:]


def nsa_attention(q, kv_cmp, kv_sel, kv_win, pos0, w0, pe_cmp, w_ck, w_cv, rel_bias, br_gate):
    n, t, h, dh = q.shape
    L = kv_cmp.shape[1]
    f32 = jnp.float32
    qg = q.astype(f32).reshape(n, t, NSA_KV_HEADS, NSA_GROUP, dh) * (dh ** -0.5)
    q_pos = pos0 + jnp.arange(t)
    rb = rel_bias.astype(f32)
    bias_g = rb.reshape(N_BUCKETS, NSA_KV_HEADS, NSA_GROUP).transpose(1, 0, 2)

    nc = L // L_CMP
    blk = kv_cmp[:, :nc * L_CMP].astype(f32).reshape(n, nc, L_CMP, 2, NSA_KV_HEADS, dh) + pe_cmp.astype(f32)
    cm = jnp.mean(blk, axis=2)
    kc = jnp.einsum('ncgd,de->ncge', cm[:, :, 0], w_ck.astype(f32))
    vc = jnp.einsum('ncgd,de->ncge', cm[:, :, 1], w_cv.astype(f32))
    dist_c = q_pos[:, None] - (jnp.arange(nc) * L_CMP + L_CMP - 1)[None, :]
    bias_c = rb[t5_bucket(dist_c)].reshape(t, nc, NSA_KV_HEADS, NSA_GROUP).transpose(0, 2, 3, 1)
    s_c = jnp.einsum('ntgrd,ncgd->ntgrc', qg, kc) + bias_c[None]
    p_c = masked_softmax(s_c, (dist_c >= 0)[None, :, None, None, :])
    o_cmp = jnp.einsum('ntgrc,ncgd->ntgrd', p_c, vc)

    n_sel = -(-L // L_SEL)
    ratio = L_SEL // L_CMP
    imp = jnp.pad(jnp.sum(p_c, axis=3), ((0, 0), (0, 0), (0, 0), (0, n_sel * ratio - nc)))
    imp = imp.reshape(n, t, NSA_KV_HEADS, n_sel, ratio).sum(-1)
    blk_id = jnp.arange(n_sel)
    forced = (blk_id[None, :] == 0) | (blk_id[None, :] == (q_pos // L_SEL)[:, None])
    avail = blk_id[None, :] * L_SEL <= q_pos[:, None]
    imp = jnp.where(forced[None, :, None, :], FORCE_SCORE, imp)
    imp = jnp.where(avail[None, :, None, :], imp, -jnp.inf)
    k_top = min(N_SEL, n_sel)
    sel_idx = lax.top_k(imp, k_top)[1]

    lp = n_sel * L_SEL
    kvs = jnp.pad(kv_sel.astype(f32), ((0, 0), (0, lp - L), (0, 0), (0, 0), (0, 0)))
    kvs = kvs.reshape(n, n_sel, L_SEL, 2, NSA_KV_HEADS, dh).transpose(0, 4, 1, 2, 3, 5)
    kvw = jnp.pad(kv_win.astype(f32), ((0, 0), (WINDOW, 0), (0, 0), (0, 0), (0, 0)))
    qb_size = Q_BLOCK if t % Q_BLOCK == 0 else t
    nqb = t // qb_size
    span = WINDOW + qb_size
    gather_blocks = jax.vmap(jax.vmap(lambda kb, ib: kb[ib]))
    lookup = jax.vmap(lambda tab, b: tab[b], in_axes=(0, 1), out_axes=1)

    def block(args):
        qb, ib, i = args
        p0 = pos0 + i * qb_size
        qp = p0 + jnp.arange(qb_size)
        ibt = ib.transpose(0, 2, 1, 3)
        kvg = gather_blocks(kvs, ibt).reshape(n, NSA_KV_HEADS, qb_size, k_top * L_SEL, 2, dh)
        spos = (ibt[..., None] * L_SEL + jnp.arange(L_SEL)).reshape(n, NSA_KV_HEADS, qb_size, k_top * L_SEL)
        dist_s = qp[None, None, :, None] - spos
        bias_s = lookup(bias_g, t5_bucket(dist_s)).transpose(0, 1, 2, 4, 3)
        s_s = jnp.einsum('nqgrd,ngqsd->ngqrs', qb, kvg[..., 0, :]) + bias_s
        p_s = masked_softmax(s_s, (dist_s >= 0)[:, :, :, None, :])
        o_s = jnp.einsum('ngqrs,ngqsd->nqgrd', p_s, kvg[..., 1, :])
        kw = lax.dynamic_slice_in_dim(kvw, p0 - w0, span, axis=1)
        wpos = p0 - WINDOW + jnp.arange(span)
        dist_w = qp[:, None] - wpos[None, :]
        valid_w = (dist_w >= 0) & (dist_w < WINDOW) & (wpos >= max(w0, 0))[None, :]
        bias_w = rb[t5_bucket(dist_w)].reshape(qb_size, span, NSA_KV_HEADS, NSA_GROUP).transpose(0, 2, 3, 1)
        s_w = jnp.einsum('nqgrd,nsgd->nqgrs', qb, kw[:, :, 0]) + bias_w[None]
        p_w = masked_softmax(s_w, valid_w[None, :, None, None, :])
        o_w = jnp.einsum('nqgrs,nsgd->nqgrd', p_w, kw[:, :, 1])
        return o_s, o_w

    qblk = qg.reshape(n, nqb, qb_size, NSA_KV_HEADS, NSA_GROUP, dh).transpose(1, 0, 2, 3, 4, 5)
    iblk = sel_idx.reshape(n, nqb, qb_size, NSA_KV_HEADS, k_top).transpose(1, 0, 2, 3, 4)
    o_sel, o_win = lax.map(block, (qblk, iblk, jnp.arange(nqb)))

    def back(o):
        return o.transpose(1, 0, 2, 3, 4, 5).reshape(n, t, NSA_KV_HEADS, NSA_GROUP, dh)

    g = br_gate.astype(f32).reshape(n, t, 3, NSA_KV_HEADS, NSA_GROUP)[..., None]
    o = g[:, :, 0] * o_cmp + g[:, :, 1] * back(o_sel) + g[:, :, 2] * back(o_win)
    return o.reshape(n, t, h * dh)


def mixer_layer(x, pos0, s_ret, pool_buf, cmp_past, sel_past, win_past,
                gain, w_in, w_pool, pool_scale, pe_cmp, w_ck, w_cv,
                w_br_a, w_br_b, w_br_c, w_out, rel_bias):
    n, t, _ = x.shape
    f32 = jnp.float32
    z = rms_norm(x, gain) @ w_in
    split_at = np.cumsum(IN_SIZES)[:-1].tolist()
    (rq, rk, rv, rg, pu, pg, nq, ck, cv, sk, sv, wk, wv, nbg, ng, mg) = jnp.split(z, split_at, axis=-1)
    o_a, s_ret_new = retention(rq.reshape(n, t, RET_HEADS, RET_DK), rk.reshape(n, t, RET_HEADS, RET_DK),
                               rv.reshape(n, t, RET_HEADS, RET_DV), s_ret, pos0)
    b_a = (jax.nn.silu(rg.astype(f32)) * o_a).astype(x.dtype) @ w_br_a
    o_b, pool_new = pool_mix(pu, pool_buf, pos0, w_pool, pool_scale)
    b_b = (jax.nn.silu(pg.astype(f32)) * o_b).astype(x.dtype) @ w_br_b
    def kv(a, b):
        return jnp.stack([a.reshape(n, t, NSA_KV_HEADS, NSA_DH), b.reshape(n, t, NSA_KV_HEADS, NSA_DH)], axis=2)
    cmp_new, sel_new, win_new = kv(ck, cv), kv(sk, sv), kv(wk, wv)
    win_full = jnp.concatenate([win_past.astype(x.dtype), win_new], axis=1)
    o_c = nsa_attention(nq.reshape(n, t, NSA_HEADS, NSA_DH),
                        jnp.concatenate([cmp_past.astype(x.dtype), cmp_new], axis=1),
                        jnp.concatenate([sel_past.astype(x.dtype), sel_new], axis=1),
                        win_full, pos0, pos0 - win_past.shape[1], pe_cmp, w_ck, w_cv, rel_bias,
                        jax.nn.sigmoid(nbg.astype(f32)))
    b_c = (jax.nn.silu(ng.astype(f32)) * o_c).astype(x.dtype) @ w_br_c
    gates = jax.nn.sigmoid(mg.astype(f32)).reshape(n, t, 3, D_MODEL)
    merged = gates[:, :, 0] * b_a + gates[:, :, 1] * b_b + gates[:, :, 2] * b_c
    y = x + merged.astype(x.dtype) @ w_out
    keep = min(WINDOW, win_full.shape[1])
    return y, (s_ret_new, pool_new, win_full[:, win_full.shape[1] - keep:], cmp_new, sel_new)


def setup_inputs(seed: int = 0) -> dict:
    key = jax.random.key(seed)
    ks = jax.random.split(key, 24)
    f32 = jnp.float32
    n_pages = PAST_LEN // PAGE_SIZE
    n_used = DEC_BATCH * n_pages
    n_phys = n_used + max(1, n_used // 4)
    w_buf = min(WINDOW, PAST_LEN)

    def nrm(k, shape, s):
        return s * jax.random.normal(k, shape, f32)

    page_table = jax.random.permutation(ks[7], n_phys)[:n_used].reshape(DEC_BATCH, n_pages).astype(jnp.int32)
    return {
        "x_prompt": nrm(ks[0], (BATCH, SEQ, D_MODEL), 1.0),
        "x_sample": nrm(ks[1], (DEC_BATCH, DEC_SEQ, D_MODEL), 1.0),
        "state_ret": nrm(ks[2], (DEPTH, DEC_BATCH, RET_HEADS, RET_DK, RET_DV), 0.5),
        "state_pool": nrm(ks[3], (DEPTH, DEC_BATCH, POOL_BUF, POOL_WIDTH), 1.0),
        "cache_win": nrm(ks[4], (DEPTH, DEC_BATCH, w_buf, 2, NSA_KV_HEADS, NSA_DH), 1.0),
        "cache_cmp": nrm(ks[5], (DEPTH, n_phys, PAGE_SIZE, 2, NSA_KV_HEADS, NSA_DH), 1.0),
        "cache_sel": nrm(ks[6], (DEPTH, n_phys, PAGE_SIZE, 2, NSA_KV_HEADS, NSA_DH), 1.0),
        "page_table": page_table,
        "norm_gain": 1.0 + nrm(ks[8], (DEPTH, D_MODEL), 0.02),
        "w_in": nrm(ks[9], (DEPTH, D_MODEL, D_IN), D_MODEL ** -0.5),
        "w_pool": nrm(ks[10], (DEPTH, POOL_GROUPS, POOL_GDIM, POOL_GDIM), POOL_GDIM ** -0.5),
        "pool_scale": 1.0 + nrm(ks[11], (DEPTH, POOL_WIDTH), 0.02),
        "pe_cmp": nrm(ks[12], (DEPTH, L_CMP, 2, NSA_KV_HEADS, NSA_DH), 0.1),
        "w_ck": nrm(ks[13], (DEPTH, NSA_DH, NSA_DH), NSA_DH ** -0.5),
        "w_cv": nrm(ks[14], (DEPTH, NSA_DH, NSA_DH), NSA_DH ** -0.5),
        "w_br_a": nrm(ks[15], (DEPTH, RET_V, D_MODEL), RET_V ** -0.5),
        "w_br_b": nrm(ks[16], (DEPTH, POOL_WIDTH, D_MODEL), POOL_WIDTH ** -0.5),
        "w_br_c": nrm(ks[17], (DEPTH, NSA_WIDTH, D_MODEL), NSA_WIDTH ** -0.5),
        "w_out": nrm(ks[18], (DEPTH, D_MODEL, D_MODEL), D_MODEL ** -0.5),
        "rel_bias": nrm(ks[19], (N_BUCKETS, NSA_HEADS), 0.5),
        "final_gain": 1.0 + nrm(ks[20], (D_MODEL,), 0.02),
    }


def reference(x_prompt, x_sample, state_ret, state_pool, cache_win, cache_cmp, cache_sel, page_table,
              norm_gain, w_in, w_pool, pool_scale, pe_cmp, w_ck, w_cv, w_br_a, w_br_b, w_br_c, w_out,
              rel_bias, final_gain):
    b = x_prompt.shape[0]
    nb = x_sample.shape[0]
    past_len = page_table.shape[1] * PAGE_SIZE
    zero_kv = jnp.zeros((b, 0, 2, NSA_KV_HEADS, NSA_DH), x_prompt.dtype)
    hp, hs = x_prompt, x_sample
    new_p = [[] for _ in range(5)]
    new_s = [[] for _ in range(5)]
    for l in range(DEPTH):
        wl = (norm_gain[l], w_in[l], w_pool[l], pool_scale[l], pe_cmp[l], w_ck[l], w_cv[l],
              w_br_a[l], w_br_b[l], w_br_c[l], w_out[l], rel_bias)
        hp, st_p = mixer_layer(hp, 0, jnp.zeros((b, RET_HEADS, RET_DK, RET_DV), jnp.float32),
                               jnp.zeros((b, POOL_BUF, POOL_WIDTH), x_prompt.dtype),
                               zero_kv, zero_kv, zero_kv, *wl)
        past_cmp = cache_cmp[l][page_table].reshape(nb, past_len, 2, NSA_KV_HEADS, NSA_DH)
        past_sel = cache_sel[l][page_table].reshape(nb, past_len, 2, NSA_KV_HEADS, NSA_DH)
        hs, st_s = mixer_layer(hs, past_len, state_ret[l], state_pool[l],
                               past_cmp, past_sel, cache_win[l], *wl)
        for j in range(5):
            new_p[j].append(st_p[j])
            new_s[j].append(st_s[j])
    y_prompt = rms_norm(hp, final_gain)
    y_sample = rms_norm(hs, final_gain)
    return (y_prompt, y_sample,
            jnp.stack(new_p[0]), jnp.stack(new_s[0]),
            jnp.stack(new_p[1]), jnp.stack(new_s[1]),
            jnp.stack(new_p[2]), jnp.stack(new_s[2]),
            jnp.stack(new_p[3]), jnp.stack(new_s[3]),
            jnp.stack(new_p[4]), jnp.stack(new_s[4]))
```

```python
import functools
import math

import numpy as np
import jax
import jax.numpy as jnp
from jax import lax
from jax.experimental import pallas as pl
from jax.experimental.pallas import tpu as pltpu

F32 = jnp.float32
BF16 = jnp.bfloat16
I32 = jnp.int32

D_MODEL = 1024
PAGE_SIZE = 128
RET_HEADS, RET_DK, RET_DV = 4, 128, 256
RET_QK, RET_V = RET_HEADS * RET_DK, RET_HEADS * RET_DV
ROPE_BASE = 10000.0
POOL_WINDOWS = (2, 4, 8, 16)
POOL_GROUPS, POOL_GDIM = 4, 128
POOL_WIDTH = POOL_GROUPS * POOL_GDIM
POOL_BUF = 15
NSA_HEADS, NSA_KV_HEADS, NSA_DH = 8, 2, 64
NSA_GROUP = NSA_HEADS // NSA_KV_HEADS
NSA_WIDTH, NSA_KV = NSA_HEADS * NSA_DH, NSA_KV_HEADS * NSA_DH
L_CMP, L_SEL, N_SEL, WINDOW = 32, 64, 16, 512
FORCE_SCORE = 1e4
N_BUCKETS, MAX_DIST = 32, 128
EPS = 1e-6
IN_SIZES = (RET_QK, RET_QK, RET_V, RET_V, POOL_WIDTH, POOL_WIDTH, NSA_WIDTH,
            NSA_KV, NSA_KV, NSA_KV, NSA_KV, NSA_KV, NSA_KV, 3 * NSA_HEADS, NSA_WIDTH, 3 * D_MODEL)

LANES = 128
VMEM_LIMIT = 56 * 1024 * 1024

C_MG, C_RV, C_RG, C_RQ, C_RK = 0, 3072, 4096, 5120, 5632
C_PU, C_PG, C_NQ, C_NG, C_KV, C_NBG = 6144, 6656, 7168, 7680, 8192, 8960
NP = 9216
KT = 128
NEG = -float(2.0 ** 100)
M_INIT = -1e30


def _cparams(sem):
    return pltpu.CompilerParams(dimension_semantics=sem, vmem_limit_bytes=VMEM_LIMIT)


def _nt(a, b):
    return lax.dot_general(a, b, (((1,), (1,)), ((), ())), preferred_element_type=F32)


def _tn(a, b):
    return lax.dot_general(a, b, (((0,), (0,)), ((), ())), preferred_element_type=F32)


def _dot(a, b):
    return jnp.dot(a, b, preferred_element_type=F32)


def _silu(x):
    return x * jax.nn.sigmoid(x)


def _inproj_kernel(x_ref, g_ref, w_ref, z_ref, xn_ref):
    @pl.when(pl.program_id(1) == 0)
    def _():
        x = x_ref[...]
        ms = jnp.mean(x * x, axis=-1, keepdims=True)
        xn_ref[...] = (x * lax.rsqrt(ms + EPS) * g_ref[...]).astype(BF16)

    z_ref[...] = _dot(xn_ref[...], w_ref[...])


def _inproj(x2, gain, w_p):
    m = x2.shape[0]
    tm = min(1024, m)
    tn = 1024
    return pl.pallas_call(
        _inproj_kernel,
        out_shape=jax.ShapeDtypeStruct((m, NP), F32),
        grid=(m // tm, NP // tn),
        in_specs=[pl.BlockSpec((tm, D_MODEL), lambda i, j: (i, 0)),
                  pl.BlockSpec((1, D_MODEL), lambda i, j: (0, 0)),
                  pl.BlockSpec((D_MODEL, tn), lambda i, j: (0, j))],
        out_specs=pl.BlockSpec((tm, tn), lambda i, j: (i, j)),
        scratch_shapes=[pltpu.VMEM((tm, D_MODEL), BF16)],
        compiler_params=_cparams(("parallel", "arbitrary")),
        name="inproj",
    )(x2, gain.reshape(1, D_MODEL), w_p)


def _pack_w_in(w_in):
    offs = np.cumsum((0,) + IN_SIZES)
    f = [w_in[..., offs[i]:offs[i + 1]] for i in range(len(IN_SIZES))]
    (rq, rk, rv, rg, pu, pg, nq, ck, cv, sk, sv, wk, wv, nbg, ng, mg) = f
    pad = lambda n: jnp.zeros(w_in.shape[:-1] + (n,), w_in.dtype)
    cols = [mg, rv, rg, rq, rk, pu, pg, nq, ng, ck, cv, sk, sv, wk, wv, nbg, pad(LANES - 3 * NSA_HEADS), pad(NP - C_NBG - LANES)]
    return jnp.concatenate(cols, axis=-1).astype(BF16)


def _retention_kernel(q_ref, k_ref, v_ref, g_ref, cos_ref, sin_ref, dm_ref, cross_ref, tail_ref, s0_ref,
                      ga_ref, s_ref, qr_scr, kt_scr, o_scr, *, nseq, chunk, decay):
    rows = nseq * chunk

    @pl.when(pl.program_id(1) == 0)
    def _():
        s_ref[...] = s0_ref[...]

    cos = cos_ref[...]
    sin = sin_ref[...]
    lane = lax.broadcasted_iota(I32, (rows, RET_DK), 1)
    even = (lane % 2) == 0

    def rot(x):
        sw = jnp.where(even, pltpu.roll(x, RET_DK - 1, 1), pltpu.roll(x, 1, 1))
        return x * cos + sw * sin

    for h in range(RET_HEADS):
        qr = rot(q_ref[:, h * RET_DK:(h + 1) * RET_DK])
        kr = rot(k_ref[:, h * RET_DK:(h + 1) * RET_DK]) * (RET_DK ** -0.5)
        att = _nt(qr.astype(BF16), kr.astype(BF16)) * dm_ref[h]
        o_scr[:, h * RET_DV:(h + 1) * RET_DV] = _dot(att.astype(BF16), v_ref[:, h * RET_DV:(h + 1) * RET_DV].astype(BF16))
        qr_scr[h] = qr
        kt_scr[h] = kr * tail_ref[h]

    def per_seq(j, carry):
        r0 = pl.multiple_of(j * chunk, chunk)
        for h in range(RET_HEADS):
            s = s_ref[j, h]
            cols = slice(h * RET_DV, (h + 1) * RET_DV)
            inter = _dot(qr_scr[h, pl.ds(r0, chunk), :].astype(BF16), s.astype(BF16)) * cross_ref[h]
            o_scr[pl.ds(r0, chunk), cols] = o_scr[pl.ds(r0, chunk), cols] + inter
            upd = _tn(kt_scr[h, pl.ds(r0, chunk), :].astype(BF16), v_ref[pl.ds(r0, chunk), cols].astype(BF16))
            s_ref[j, h] = s * decay[h] + upd
        return carry

    lax.fori_loop(0, nseq, per_seq, 0)

    for h in range(RET_HEADS):
        cols = slice(h * RET_DV, (h + 1) * RET_DV)
        o = o_scr[:, cols]
        o = o * lax.rsqrt(jnp.mean(o * o, axis=-1, keepdims=True) + EPS)
        ga_ref[:, cols] = (_silu(g_ref[:, cols]) * o).astype(BF16)


def _retention(z, s0, *, n, t, pos0, chunk, nseq):
    nc = t // chunk
    rows = nseq * chunk
    assert t % chunk == 0 and n % nseq == 0 and (nseq == 1 or nc == 1)
    h = np.arange(RET_HEADS, dtype=np.float64)
    log_g = np.log1p(-np.exp2(-5.0 - h))
    idx = np.arange(chunk, dtype=np.float64)
    rel = idx[:, None] - idx[None, :]
    dmask = np.where(rel[None] >= 0, np.exp(np.maximum(rel, 0.0)[None] * log_g[:, None, None]), 0.0)
    dmask = np.stack([np.kron(np.eye(nseq), dmask[i]) for i in range(RET_HEADS)])
    cross = np.exp((idx + 1.0)[None, :] * log_g[:, None])
    tail = np.exp((chunk - 1.0 - idx)[None, :] * log_g[:, None])
    decay = tuple(float(x) for x in np.exp(chunk * log_g))
    cross_b = np.broadcast_to(cross[:, :, None], (RET_HEADS, chunk, RET_DV))
    tail_b = np.broadcast_to(np.tile(tail, (1, nseq))[:, :, None], (RET_HEADS, rows, RET_DK))
    pos = pos0 + np.arange(t, dtype=np.float64)
    inv = ROPE_BASE ** (-np.arange(0, RET_DK, 2, dtype=np.float64) / RET_DK)
    ang = (pos.astype(np.float32)[:, None] * inv.astype(np.float32)[None, :]).astype(np.float64)
    cos = np.repeat(np.cos(ang), 2, axis=1)
    sgn = np.tile(np.array([-1.0, 1.0]), RET_DK // 2)
    sin = np.repeat(np.sin(ang), 2, axis=1) * sgn[None, :]
    if nseq > 1:
        cos, sin = np.tile(cos, (nseq, 1)), np.tile(sin, (nseq, 1))
    c = lambda a: jnp.asarray(a, F32)
    kern = functools.partial(_retention_kernel, nseq=nseq, chunk=chunk, decay=decay)
    row_blk = lambda i, j: i * nc + j
    ga, s_new = pl.pallas_call(
        kern,
        out_shape=(jax.ShapeDtypeStruct((n * t, RET_V), BF16), jax.ShapeDtypeStruct(s0.shape, F32)),
        grid=(n // nseq, nc),
        in_specs=[pl.BlockSpec((rows, RET_QK), lambda i, j: (row_blk(i, j), C_RQ // RET_QK)),
                  pl.BlockSpec((rows, RET_QK), lambda i, j: (row_blk(i, j), C_RK // RET_QK)),
                  pl.BlockSpec((rows, RET_V), lambda i, j: (row_blk(i, j), C_RV // RET_V)),
                  pl.BlockSpec((rows, RET_V), lambda i, j: (row_blk(i, j), C_RG // RET_V)),
                  pl.BlockSpec((rows, RET_DK), lambda i, j: (j, 0)),
                  pl.BlockSpec((rows, RET_DK), lambda i, j: (j, 0)),
                  pl.BlockSpec((RET_HEADS, rows, rows), lambda i, j: (0, 0, 0)),
                  pl.BlockSpec((RET_HEADS, chunk, RET_DV), lambda i, j: (0, 0, 0)),
                  pl.BlockSpec((RET_HEADS, rows, RET_DK), lambda i, j: (0, 0, 0)),
                  pl.BlockSpec((nseq, RET_HEADS, RET_DK, RET_DV), lambda i, j: (i, 0, 0, 0))],
        out_specs=(pl.BlockSpec((rows, RET_V), lambda i, j: (row_blk(i, j), 0)),
                   pl.BlockSpec((nseq, RET_HEADS, RET_DK, RET_DV), lambda i, j: (i, 0, 0, 0))),
        scratch_shapes=[pltpu.VMEM((RET_HEADS, rows, RET_DK), F32),
                        pltpu.VMEM((RET_HEADS, rows, RET_DK), F32),
                        pltpu.VMEM((rows, RET_V), F32)],
        compiler_params=_cparams(("parallel", "arbitrary")),
        name="retention",
    )(z, z, z, z, c(cos), c(sin), c(dmask), c(cross_b), c(tail_b), s0)
    return ga, s_new


def _pool_kernel(u_ref, pg_ref, halo_ref, w_ref, sc_ref, gb_ref, ext_scr, *, nb, tt, pos0, nbt):
    hal = 16
    ext_scr[:, 0:hal, :] = halo_ref[...]
    ext_scr[:, hal:hal + tt, :] = u_ref[...]
    pos_base = pos0 + (pl.program_id(0) * nb % nbt) * tt
    p1 = lax.broadcasted_iota(I32, (nb, tt, POOL_GDIM), 1) + pos_base + 1
    for g, w in enumerate(POOL_WINDOWS):
        cols = slice(g * POOL_GDIM, (g + 1) * POOL_GDIM)
        acc = ext_scr[:, hal:hal + tt, cols]
        for j in range(1, w):
            acc = acc + ext_scr[:, hal - j:hal - j + tt, cols]
        cnt = jnp.minimum(p1, w).astype(F32)
        d = acc / cnt - u_ref[:, :, cols]
        y = _dot(d.reshape(nb * tt, POOL_GDIM).astype(BF16), w_ref[g])
        y = y * sc_ref[:, cols]
        gate = _silu(pg_ref[:, :, cols]).reshape(nb * tt, POOL_GDIM)
        gb_ref[:, cols] = (gate * y).astype(BF16)


def _pool(z, halo, w_pool_b, scale, *, n, t, pos0, tt, nb):
    nbt = t // tt
    g_tot = n * nbt
    assert g_tot % nb == 0 and (nb == 1 or nbt == 1)
    z3 = z.reshape(g_tot, tt, NP)
    kern = functools.partial(_pool_kernel, nb=nb, tt=tt, pos0=pos0, nbt=nbt)
    return pl.pallas_call(
        kern,
        out_shape=jax.ShapeDtypeStruct((n * t, POOL_WIDTH), BF16),
        grid=(g_tot // nb,),
        in_specs=[pl.BlockSpec((nb, tt, POOL_WIDTH), lambda i: (i, 0, C_PU // POOL_WIDTH)),
                  pl.BlockSpec((nb, tt, POOL_WIDTH), lambda i: (i, 0, C_PG // POOL_WIDTH)),
                  pl.BlockSpec((nb, 16, POOL_WIDTH), lambda i: (i, 0, 0)),
                  pl.BlockSpec((POOL_GROUPS, POOL_GDIM, POOL_GDIM), lambda i: (0, 0, 0)),
                  pl.BlockSpec((1, POOL_WIDTH), lambda i: (0, 0))],
        out_specs=pl.BlockSpec((nb * tt, POOL_WIDTH), lambda i: (i, 0)),
        scratch_shapes=[pltpu.VMEM((nb, 16 + tt, POOL_WIDTH), F32)],
        compiler_params=_cparams(("parallel",)),
        name="pool",
    )(z3, z3, halo, w_pool_b, scale.reshape(1, POOL_WIDTH))


def _t5_bucket(dist):
    n = jnp.maximum(dist, 0)
    exact = N_BUCKETS // 2
    nf = jnp.maximum(n, 1).astype(F32)
    large = exact + (jnp.log(nf / exact) / math.log(MAX_DIST / exact) * (N_BUCKETS - exact)).astype(I32)
    large = jnp.minimum(large, N_BUCKETS - 1)
    return jnp.where(n < exact, n, large)


def _bias_table(rel_bias):
    rb = rel_bias.astype(F32)
    tab = rb[_t5_bucket(jnp.arange(2 * KT, dtype=I32))]
    return tab - rb[N_BUCKETS - 1][None, :]


def _toeplitz_T(tab, delta, rows_q, window=None):
    i = np.arange(rows_q)[None, :]
    j = np.arange(KT)[:, None]
    d = delta + i - j
    valid = d >= 0
    if window is not None:
        valid &= d < window
    dc = np.clip(d, 0, 2 * KT - 1)
    near = (d < 2 * KT) & valid
    vals = tab[jnp.asarray(dc)]
    vals = jnp.where(jnp.asarray(near)[:, :, None], vals, 0.0)
    vals = jnp.where(jnp.asarray(valid)[:, :, None], vals, NEG)
    vals = vals.reshape(KT, rows_q, NSA_KV_HEADS, NSA_GROUP).transpose(2, 0, 3, 1)
    return vals.reshape(NSA_KV_HEADS, KT, NSA_GROUP * rows_q)


def _cmp_rowbias(tab, pos, per_row_head):
    b = np.asarray(pos) % L_CMP
    cols = []
    for m in range(-1, 4):
        if m < 0:
            v = jnp.where(jnp.asarray(b == L_CMP - 1)[:, None], tab[0][None, :], NEG)
        else:
            v = tab[jnp.asarray(L_CMP * m + b + 1)]
        cols.append(v)
    r = jnp.stack(cols, axis=1)
    if per_row_head is None:
        r = r.reshape(r.shape[0], 5 * NSA_HEADS)
        return jnp.pad(r, ((0, 0), (0, LANES - 5 * NSA_HEADS)))
    r = r[jnp.arange(r.shape[0]), :, jnp.asarray(per_row_head)]
    return jnp.pad(r, ((0, 0), (0, LANES - 5)))


def _cmp_bias(dist, col):
    b = jnp.where(dist <= 4 * L_CMP, col(3), 0.0)
    b = jnp.where(dist <= 3 * L_CMP, col(2), b)
    b = jnp.where(dist <= 2 * L_CMP, col(1), b)
    b = jnp.where(dist <= L_CMP, col(0), b)
    return jnp.where(dist <= 0, col(-1), b)


def _select_mask(imp, sblk, qpos, k_top):
    forced = (sblk == 0) | (sblk == qpos // L_SEL)
    avail = sblk * L_SEL <= qpos
    w = jnp.where(forced, FORCE_SCORE, imp)
    w = jnp.where(avail, w, -jnp.inf)
    lane = lax.broadcasted_iota(I32, w.shape, 1).astype(F32)
    sel = jnp.zeros(w.shape, F32)
    for _ in range(k_top):
        mx = jnp.max(w, axis=-1, keepdims=True)
        first = jnp.min(jnp.where(w == mx, lane, float(4 * LANES)), axis=-1, keepdims=True)
        pick = lane == first
        sel = jnp.where(pick, 1.0, sel)
        w = jnp.where(pick, -jnp.inf, w)
    return jnp.where((sel > 0.0) & avail, 0.0, NEG)


def _cmpkv_kernel(kv_ref, pe_ref, wk_ref, wv_ref, kc_ref, vc_ref):
    x = kv_ref[...]
    nblk = x.shape[0] // L_CMP
    cm = x.reshape(nblk, L_CMP, 2 * NSA_KV).sum(axis=1) * (1.0 / L_CMP)
    cm = cm + jnp.mean(pe_ref[...], axis=0, keepdims=True)
    kc_ref[...] = _dot(cm[:, :NSA_KV].astype(BF16), wk_ref[...])
    vc_ref[...] = _dot(cm[:, NSA_KV:].astype(BF16), wv_ref[...])


def _cmpkv(z, pe2, bd_ck, bd_cv, *, n, t):
    nblk = t // L_CMP
    z3 = z.reshape(n, t, NP)
    return pl.pallas_call(
        _cmpkv_kernel,
        out_shape=(jax.ShapeDtypeStruct((n, nblk, NSA_KV), F32), jax.ShapeDtypeStruct((n, nblk, NSA_KV), F32)),
        grid=(n,),
        in_specs=[pl.BlockSpec((None, nblk * L_CMP, 2 * NSA_KV), lambda i: (i, 0, C_KV // (2 * NSA_KV))),
                  pl.BlockSpec((L_CMP, 2 * NSA_KV), lambda i: (0, 0)),
                  pl.BlockSpec((NSA_KV, NSA_KV), lambda i: (0, 0)),
                  pl.BlockSpec((NSA_KV, NSA_KV), lambda i: (0, 0))],
        out_specs=(pl.BlockSpec((None, nblk, NSA_KV), lambda i: (i, 0, 0)),
                   pl.BlockSpec((None, nblk, NSA_KV), lambda i: (i, 0, 0))),
        compiler_params=_cparams(("parallel",)),
        name="cmp_kv",
    )(z3, pe2, bd_ck, bd_cv)


def _cmpsel_kernel(q_ref, kc_ref, vc_ref, rb_ref, cpos_ref, sblk_ref, oc_ref, mn_ref, *, tq, k_top):
    qs = pl.program_id(1) * tq
    qpos = lax.broadcasted_iota(I32, (tq, 1), 0) + qs
    dist = qpos - cpos_ref[...]
    valid = dist >= 0
    rb = rb_ref[...]
    kc = kc_ref[...].astype(BF16)
    vc = vc_ref[...].astype(BF16)
    for g in range(NSA_KV_HEADS):
        imp = jnp.zeros((tq, 2 * LANES), F32)
        kg = kc[:, g * NSA_DH:(g + 1) * NSA_DH]
        vg = vc[:, g * NSA_DH:(g + 1) * NSA_DH]
        for r in range(NSA_GROUP):
            h = g * NSA_GROUP + r
            q = (q_ref[:, h * NSA_DH:(h + 1) * NSA_DH] * (NSA_DH ** -0.5)).astype(BF16)
            s = _nt(q, kg) + _cmp_bias(dist, lambda m, h=h: rb[:, (m + 1) * NSA_HEADS + h:(m + 1) * NSA_HEADS + h + 1])
            s = jnp.where(valid, s, M_INIT)
            mx = jnp.max(s, axis=-1, keepdims=True)
            e = jnp.where(valid, jnp.exp(s - mx), 0.0)
            p = e / jnp.maximum(jnp.sum(e, axis=-1, keepdims=True), 1e-30)
            oc_ref[:, h * NSA_DH:(h + 1) * NSA_DH] = _dot(p.astype(BF16), vg)
            imp = imp + p
        imp = imp[:, :LANES] + imp[:, LANES:]
        mn_ref[:, g * LANES:(g + 1) * LANES] = _select_mask(imp, sblk_ref[...], qpos, k_top).astype(BF16)


def _cmpsel(z, kcp, vcp, rowb, cpos, sblk, *, n, t, k_top):
    tq = KT
    nqt = t // tq
    kern = functools.partial(_cmpsel_kernel, tq=tq, k_top=k_top)
    return pl.pallas_call(
        kern,
        out_shape=(jax.ShapeDtypeStruct((n * t, NSA_WIDTH), F32), jax.ShapeDtypeStruct((n * t, NSA_KV_HEADS * LANES), BF16)),
        grid=(n, nqt),
        in_specs=[pl.BlockSpec((tq, NSA_WIDTH), lambda b, i: (b * nqt + i, C_NQ // NSA_WIDTH)),
                  pl.BlockSpec((None, 2 * LANES, NSA_KV), lambda b, i: (b, 0, 0)),
                  pl.BlockSpec((None, 2 * LANES, NSA_KV), lambda b, i: (b, 0, 0)),
                  pl.BlockSpec((tq, LANES), lambda b, i: (0, 0)),
                  pl.BlockSpec((1, 2 * LANES), lambda b, i: (0, 0)),
                  pl.BlockSpec((1, LANES), lambda b, i: (0, 0))],
        out_specs=(pl.BlockSpec((tq, NSA_WIDTH), lambda b, i: (b * nqt + i, 0)),
                   pl.BlockSpec((tq, NSA_KV_HEADS * LANES), lambda b, i: (b * nqt + i, 0))),
        compiler_params=_cparams(("parallel", "parallel")),
        name="cmp_select",
    )(z, kcp, vcp, rowb, cpos, sblk)


def _flash_tile(qp_ref, k_tile, vt_tile, add, m_ref, l_ref, acc_ref):
    st = _nt(k_tile, qp_ref[...])
    if add is not None:
        st = st + add
    m_old = m_ref[...]
    m_new = jnp.maximum(m_old, jnp.max(st, axis=0, keepdims=True))
    alpha = jnp.exp(m_old - m_new)
    p = jnp.exp(st - m_new)
    l_ref[...] = alpha * l_ref[...] + jnp.sum(p, axis=0, keepdims=True)
    acc_ref[...] = alpha * acc_ref[...] + _dot(vt_tile, p.astype(BF16))
    m_ref[...] = m_new


def _flash_reset(m_ref, l_ref, acc_ref):
    m_ref[...] = jnp.full(m_ref.shape, M_INIT, F32)
    l_ref[...] = jnp.zeros(l_ref.shape, F32)
    acc_ref[...] = jnp.zeros(acc_ref.shape, F32)


def _flash_store(o_ref, l_ref, acc_ref, rows_q):
    ot = acc_ref[...] / l_ref[...]
    for r in range(NSA_GROUP):
        o_ref[:, r * NSA_DH:(r + 1) * NSA_DH] = ot[:, r * rows_q:(r + 1) * rows_q].T


def _selwin_kernel(q_ref, mn_ref, ks_ref, vs_ref, kw_ref, vw_ref, a0_ref, a1_ref, aw_ref, os_ref, ow_ref,
                   qp_scr, m_scr, l_scr, acc_scr):
    qt = pl.program_id(2)
    mn = mn_ref[...]
    for r in range(NSA_GROUP):
        qp_scr[r * KT:(r + 1) * KT, 0:LANES] = mn
        qp_scr[r * KT:(r + 1) * KT, LANES:LANES + NSA_DH] = (q_ref[:, r * NSA_DH:(r + 1) * NSA_DH] * (NSA_DH ** -0.5)).astype(BF16)

    stats = (m_scr, l_scr, acc_scr)

    _flash_reset(*stats)

    def far(kb, carry):
        _flash_tile(qp_scr, ks_ref[kb], vs_ref[kb], None, *stats)
        return carry

    lax.fori_loop(0, qt - 1, far, 0)

    @pl.when(qt >= 1)
    def _():
        _flash_tile(qp_scr, ks_ref[qt - 1], vs_ref[qt - 1], a1_ref[...], *stats)

    _flash_tile(qp_scr, ks_ref[qt], vs_ref[qt], a0_ref[...], *stats)
    _flash_store(os_ref, l_scr, acc_scr, KT)

    _flash_reset(*stats)
    nw = WINDOW // KT
    for dk in range(nw, 0, -1):
        add = aw_ref if dk == nw else (a1_ref if dk == 1 else None)

        @pl.when(qt >= dk)
        def _(dk=dk, add=add):
            _flash_tile(qp_scr, kw_ref[qt - dk], vw_ref[qt - dk], None if add is None else add[...], *stats)

    _flash_tile(qp_scr, kw_ref[qt], vw_ref[qt], a0_ref[...], *stats)
    _flash_store(ow_ref, l_scr, acc_scr, KT)


def _selwin(z, mneg, ks, vs, kw, vw, a0, a1, aw, *, n, t):
    nqt = t // KT
    gw = NSA_GROUP * NSA_DH
    kspec = pl.BlockSpec((None, None, nqt, KT, LANES + NSA_DH), lambda b, g, i: (b, g, 0, 0, 0))
    vspec = pl.BlockSpec((None, None, nqt, NSA_DH, KT), lambda b, g, i: (b, g, 0, 0, 0))
    aspec = pl.BlockSpec((None, KT, NSA_GROUP * KT), lambda b, g, i: (g, 0, 0))
    ospec = pl.BlockSpec((KT, gw), lambda b, g, i: (b * nqt + i, g))
    return pl.pallas_call(
        _selwin_kernel,
        out_shape=(jax.ShapeDtypeStruct((n * t, NSA_WIDTH), F32), jax.ShapeDtypeStruct((n * t, NSA_WIDTH), F32)),
        grid=(n, NSA_KV_HEADS, nqt),
        in_specs=[pl.BlockSpec((KT, gw), lambda b, g, i: (b * nqt + i, C_NQ // gw + g)),
                  pl.BlockSpec((KT, LANES), lambda b, g, i: (b * nqt + i, g)),
                  kspec, vspec, kspec, vspec, aspec, aspec, aspec],
        out_specs=(ospec, ospec),
        scratch_shapes=[pltpu.VMEM((NSA_GROUP * KT, LANES + NSA_DH), BF16),
                        pltpu.VMEM((1, NSA_GROUP * KT), F32),
                        pltpu.VMEM((1, NSA_GROUP * KT), F32),
                        pltpu.VMEM((NSA_DH, NSA_GROUP * KT), F32)],
        compiler_params=_cparams(("parallel", "parallel", "arbitrary")),
        name="sel_win",
    )(z, mneg, ks, vs, kw, vw, a0, a1, aw)


def _nsa_prompt(z, pe2, bd_ck, bd_cv, tab, *, n, t):
    assert t % KT == 0 and WINDOW % KT == 0 and KT % L_SEL == 0
    nblk = t // L_CMP
    n_sel = -(-t // L_SEL)
    assert nblk <= 2 * LANES and n_sel <= LANES and nblk % 2 == 0
    kc, vc = _cmpkv(z, pe2, bd_ck, bd_cv, n=n, t=t)
    half = nblk // 2

    def perm(a):
        z0 = jnp.zeros((n, LANES - half, NSA_KV), F32)
        return jnp.concatenate([a[:, 0::2], z0, a[:, 1::2], z0], axis=1)

    slot = np.arange(LANES)
    big = 1 << 20
    cpos = np.concatenate([np.where(slot < half, (2 * slot) * L_CMP + L_CMP - 1, big),
                           np.where(slot < half, (2 * slot + 1) * L_CMP + L_CMP - 1, big)])[None, :]
    sblk = np.where(slot < n_sel, slot, big)[None, :]
    rowb = _cmp_rowbias(tab, np.arange(KT), None)
    o_cmp, mneg = _cmpsel(z, perm(kc), perm(vc), rowb, jnp.asarray(cpos, I32), jnp.asarray(sblk, I32),
                          n=n, t=t, k_top=min(N_SEL, n_sel))

    nqt = t // KT
    kv = z[:, C_KV:C_KV + 6 * NSA_KV].reshape(n, t, 6, NSA_KV_HEADS, NSA_DH)

    def ktiles(k, lead):
        k = k.transpose(0, 2, 1, 3).astype(BF16)
        lead = jnp.broadcast_to(lead[None, None], (n, NSA_KV_HEADS, t, LANES))
        return jnp.concatenate([lead, k], axis=-1).reshape(n, NSA_KV_HEADS, nqt, KT, LANES + NSA_DH)

    def vtiles(v):
        return v.astype(BF16).reshape(n, nqt, KT, NSA_KV_HEADS, NSA_DH).transpose(0, 3, 1, 4, 2)

    onehot = jnp.asarray(np.arange(t)[:, None] // L_SEL == np.arange(LANES)[None, :], BF16)
    ks, vs = ktiles(kv[:, :, 2], onehot), vtiles(kv[:, :, 3])
    kw, vw = ktiles(kv[:, :, 4], jnp.zeros((t, LANES), BF16)), vtiles(kv[:, :, 5])
    a0 = _toeplitz_T(tab, 0, KT)
    a1 = _toeplitz_T(tab, KT, KT)
    aw = _toeplitz_T(tab, WINDOW, KT, window=WINDOW)
    o_sel, o_win = _selwin(z, mneg, ks, vs, kw, vw, a0, a1, aw, n=n, t=t)
    return o_cmp, o_sel, o_win


def _nsa_sample_kernel(pt_ref, *refs, npg, t, past, wbuf, k_top):
    del pt_ref
    cpages = refs[0:npg]
    spages = refs[npg:2 * npg]
    (q_ref, kvn_ref, cw_ref, pe_ref, wk_ref, wv_ref, rb_ref, cpos_ref, sblk_ref, oh_ref, asel_ref, awin_ref,
     oc_ref, os_ref, ow_ref, wout_ref,
     ksel, vsel, kwin, vwin, cm_scr, kcv_scr, qw_scr, qp_scr) = refs[2 * npg:]
    rows = NSA_HEADS * t
    half = NSA_GROUP * t
    lsel = ksel.shape[0]
    lwin = kwin.shape[0]
    kvn = kvn_ref[...]

    @pl.when(pl.program_id(0) == 0)
    def _():
        ksel[:, 0:LANES] = oh_ref[...]
        ksel[past:lsel, LANES:2 * LANES] = jnp.zeros((lsel - past, LANES), BF16)
        vsel[past:lsel, :] = jnp.zeros((lsel - past, LANES), BF16)
        kwin[wbuf:lwin, :] = jnp.zeros((lwin - wbuf, LANES), BF16)
        vwin[wbuf:lwin, :] = jnp.zeros((lwin - wbuf, LANES), BF16)
        kcv_scr[...] = jnp.zeros(kcv_scr.shape, F32)
        qw_scr[...] = jnp.zeros(qw_scr.shape, F32)

    bpp = PAGE_SIZE // L_CMP
    for p in range(npg):
        sp = spages[p]
        ksel[p * PAGE_SIZE:(p + 1) * PAGE_SIZE, LANES:2 * LANES] = sp[:, 0:NSA_KV].astype(BF16)
        vsel[p * PAGE_SIZE:(p + 1) * PAGE_SIZE, :] = sp[:, NSA_KV:2 * NSA_KV].astype(BF16)
        cm_scr[p * bpp:(p + 1) * bpp, :] = cpages[p][...].reshape(bpp, L_CMP, 2 * NSA_KV).sum(axis=1) * (1.0 / L_CMP)
    ksel[past:past + t, LANES:2 * LANES] = kvn[:, 2 * NSA_KV:3 * NSA_KV].astype(BF16)
    vsel[past:past + t, :] = kvn[:, 3 * NSA_KV:4 * NSA_KV].astype(BF16)
    cw = cw_ref[...]
    kwin[0:wbuf, :] = cw[:, 0:NSA_KV].astype(BF16)
    vwin[0:wbuf, :] = cw[:, NSA_KV:2 * NSA_KV].astype(BF16)
    kwin[wbuf:wbuf + t, :] = kvn[:, 4 * NSA_KV:5 * NSA_KV].astype(BF16)
    vwin[wbuf:wbuf + t, :] = kvn[:, 5 * NSA_KV:6 * NSA_KV].astype(BF16)
    keep = wout_ref.shape[0]
    wout_ref[0:keep - t, :] = cw[wbuf + t - keep:wbuf, :]
    wout_ref[keep - t:keep, :] = kvn[:, 4 * NSA_KV:6 * NSA_KV]

    for h in range(NSA_HEADS):
        g = h // NSA_GROUP
        qh = q_ref[:, h * NSA_DH:(h + 1) * NSA_DH] * (NSA_DH ** -0.5)
        qw_scr[h * t:(h + 1) * t, g * NSA_DH:(g + 1) * NSA_DH] = qh
    qw = qw_scr[...].astype(BF16)

    def pick(x):
        row = lax.broadcasted_iota(I32, (rows, NSA_DH), 0)
        return jnp.where(row < half, x[:, :NSA_DH], x[:, NSA_DH:])

    def store(o_ref, x):
        for h in range(NSA_HEADS):
            o_ref[:, h * NSA_DH:(h + 1) * NSA_DH] = x[h * t:(h + 1) * t, :]

    nblk = npg * bpp
    cm = cm_scr[...] + jnp.mean(pe_ref[...], axis=0, keepdims=True)
    kcv_scr[0:nblk, 0:NSA_KV] = _dot(cm[:, :NSA_KV].astype(BF16), wk_ref[...])
    kcv_scr[0:nblk, NSA_KV:] = _dot(cm[:, NSA_KV:].astype(BF16), wv_ref[...])
    kc = kcv_scr[:, 0:NSA_KV].astype(BF16)
    vc = kcv_scr[:, NSA_KV:].astype(BF16)
    qpos = lax.broadcasted_iota(I32, (rows, 1), 0) % t + past
    dist = qpos - cpos_ref[...]
    valid = dist >= 0
    rb = rb_ref[...]
    s = _nt(qw, kc) + _cmp_bias(dist, lambda m: rb[:, m + 1:m + 2])
    s = jnp.where(valid, s, M_INIT)
    mx = jnp.max(s, axis=-1, keepdims=True)
    e = jnp.where(valid, jnp.exp(s - mx), 0.0)
    p = e / jnp.maximum(jnp.sum(e, axis=-1, keepdims=True), 1e-30)
    store(oc_ref, pick(_dot(p.astype(BF16), vc)))

    for g in range(NSA_KV_HEADS):
        imp = p[g * half:g * half + t]
        for r in range(1, NSA_GROUP):
            imp = imp + p[g * half + r * t:g * half + (r + 1) * t]
        imp = imp + pltpu.roll(imp, LANES - 1, 1)
        mn = _select_mask(imp, sblk_ref[...], qpos[0:t], k_top).astype(BF16)
        for r in range(NSA_GROUP):
            qp_scr[g * half + r * t:g * half + (r + 1) * t, 0:LANES] = mn
    qp_scr[:, LANES:2 * LANES] = qw

    def attend(qmat, k_all, v_all, add):
        sc = _nt(qmat, k_all) + add
        mm = jnp.max(sc, axis=-1, keepdims=True)
        ee = jnp.exp(sc - mm)
        ll = jnp.sum(ee, axis=-1, keepdims=True)
        return pick(_dot(ee.astype(BF16), v_all)) / ll

    store(os_ref, attend(qp_scr[...], ksel[...], vsel[...], asel_ref[...]))
    store(ow_ref, attend(qw, kwin[...], vwin[...], awin_ref[...]))


def _nsa_sample(z, cache_cmp_l, cache_sel_l, cache_win_l, page_table, pe2, bd_ck, bd_cv, tab, *, n, t, past):
    npg = page_table.shape[1]
    assert past == npg * PAGE_SIZE and past % L_SEL == 0 and past % L_CMP == 0
    wbuf = cache_win_l.shape[1]
    keep = min(WINDOW, wbuf + t)
    nblk = past // L_CMP
    assert t < L_CMP and nblk <= LANES
    length = past + t
    n_sel = -(-length // L_SEL)
    assert 2 * (n_sel - 1) < LANES
    rows = NSA_HEADS * t
    lsel = -(-(past + t) // KT) * KT
    lwin = -(-(wbuf + t) // KT) * KT
    big = 1 << 20
    slot = np.arange(LANES)
    cpos = np.where(slot < nblk, slot * L_CMP + L_CMP - 1, big)[None, :]
    sblk = np.where((slot % 2 == 0) & (slot // 2 < n_sel), slot // 2, big)[None, :]
    head_of_row = np.repeat(np.arange(NSA_HEADS), t)
    tok_of_row = np.tile(np.arange(t), NSA_HEADS)
    rowb = _cmp_rowbias(tab, past + tok_of_row, head_of_row)
    kpos_sel = np.arange(lsel)
    onehot = jnp.asarray((kpos_sel[:, None] < length) & (2 * (kpos_sel[:, None] // L_SEL) == slot[None, :]), BF16)

    def additive(kpos, exists, window):
        d = (past + tok_of_row)[:, None] - kpos[None, :]
        valid = (d >= 0) & exists[None, :]
        if window is not None:
            valid &= d < window
        near = valid & (d < 2 * KT)
        vals = tab[jnp.asarray(np.clip(d, 0, 2 * KT - 1)), jnp.asarray(head_of_row)[:, None]]
        vals = jnp.where(jnp.asarray(near), vals, 0.0)
        return jnp.where(jnp.asarray(valid), vals, NEG)

    asel = additive(kpos_sel, kpos_sel < length, None)
    kidx = np.arange(lwin)
    awin = additive(past - wbuf + kidx, kidx < wbuf + t, WINDOW)

    cmp3 = cache_cmp_l.reshape(cache_cmp_l.shape[0], PAGE_SIZE, 2 * NSA_KV)
    sel3 = cache_sel_l.reshape(cache_sel_l.shape[0], PAGE_SIZE, 2 * NSA_KV)
    win3 = cache_win_l.reshape(n, wbuf, 2 * NSA_KV)
    page_spec = lambda p: pl.BlockSpec((None, PAGE_SIZE, 2 * NSA_KV), lambda i, pt, p=p: (pt[i, p], 0, 0))
    const = lambda shape: pl.BlockSpec(shape, lambda i, pt: tuple(0 for _ in shape))
    ospec = pl.BlockSpec((t, NSA_WIDTH), lambda i, pt: (i, 0))
    kern = functools.partial(_nsa_sample_kernel, npg=npg, t=t, past=past, wbuf=wbuf, k_top=min(N_SEL, n_sel))
    grid_spec = pltpu.PrefetchScalarGridSpec(
        num_scalar_prefetch=1,
        grid=(n,),
        in_specs=[page_spec(p) for p in range(npg)] + [page_spec(p) for p in range(npg)] + [
            pl.BlockSpec((t, NSA_WIDTH), lambda i, pt: (i, C_NQ // NSA_WIDTH)),
            pl.BlockSpec((t, 1024), lambda i, pt: (i, C_KV // 1024)),
            pl.BlockSpec((None, wbuf, 2 * NSA_KV), lambda i, pt: (i, 0, 0)),
            const((L_CMP, 2 * NSA_KV)), const((NSA_KV, NSA_KV)), const((NSA_KV, NSA_KV)),
            const((rows, LANES)), const((1, LANES)), const((1, LANES)),
            const((lsel, LANES)), const((rows, lsel)), const((rows, lwin))],
        out_specs=(ospec, ospec, ospec, pl.BlockSpec((None, keep, 2 * NSA_KV), lambda i, pt: (i, 0, 0))),
        scratch_shapes=[pltpu.VMEM((lsel, 2 * LANES), BF16), pltpu.VMEM((lsel, LANES), BF16),
                        pltpu.VMEM((lwin, LANES), BF16), pltpu.VMEM((lwin, LANES), BF16),
                        pltpu.VMEM((nblk, 2 * NSA_KV), F32), pltpu.VMEM((LANES, 2 * NSA_KV), F32),
                        pltpu.VMEM((rows, NSA_KV), F32),
                        pltpu.VMEM((rows, 2 * LANES), BF16)])
    o_cmp, o_sel, o_win, win_new = pl.pallas_call(
        kern,
        grid_spec=grid_spec,
        out_shape=(jax.ShapeDtypeStruct((n * t, NSA_WIDTH), F32),) * 3 + (jax.ShapeDtypeStruct((n, keep, 2 * NSA_KV), F32),),
        compiler_params=_cparams(("arbitrary",)),
        name="nsa_sample",
    )(page_table, *([cmp3] * npg), *([sel3] * npg), z, z, win3, pe2, bd_ck, bd_cv, rowb,
      jnp.asarray(cpos, I32), jnp.asarray(sblk, I32), onehot, asel, awin)
    return o_cmp, o_sel, o_win, win_new


def _merge_kernel(x_ref, mg_ref, ga_ref, gb_ref, oc_ref, os_ref, ow_ref, nbg_ref, ng_ref,
                  wa_ref, wb_ref, wc_ref, wo_ref, fg_ref, y_ref, *, final):
    tm = x_ref.shape[0]
    gates = jax.nn.sigmoid(nbg_ref[...])
    lane = lax.broadcasted_iota(I32, (tm, LANES), 1)
    low = lane < NSA_DH
    parts = []
    for pi in range(NSA_HEADS // 2):
        cols = slice(pi * LANES, (pi + 1) * LANES)
        acc = jnp.zeros((tm, LANES), F32)
        for b, o_ref in enumerate((oc_ref, os_ref, ow_ref)):
            c0 = b * NSA_HEADS + 2 * pi
            gt = jnp.where(low, gates[:, c0:c0 + 1], gates[:, c0 + 1:c0 + 2])
            acc = acc + gt * o_ref[:, cols]
        parts.append((_silu(ng_ref[:, cols]) * acc).astype(BF16))
    gc = jnp.concatenate(parts, axis=1)
    b_a = _dot(ga_ref[...], wa_ref[...])
    b_b = _dot(gb_ref[...], wb_ref[...])
    b_c = _dot(gc, wc_ref[...])
    mg = mg_ref[...]
    merged = (jax.nn.sigmoid(mg[:, 0:D_MODEL]) * b_a + jax.nn.sigmoid(mg[:, D_MODEL:2 * D_MODEL]) * b_b
              + jax.nn.sigmoid(mg[:, 2 * D_MODEL:]) * b_c)
    y = x_ref[...] + _dot(merged.astype(BF16), wo_ref[...])
    if final:
        y = y * lax.rsqrt(jnp.mean(y * y, axis=-1, keepdims=True) + EPS) * fg_ref[...]
    y_ref[...] = y


def _merge(x2, z, ga, gb, o_cmp, o_sel, o_win, wa, wb, wc, wo, fgain, *, final):
    m = x2.shape[0]
    tm = min(256, m)
    row = lambda w, c=0: pl.BlockSpec((tm, w), lambda i, c=c: (i, c))
    const = lambda a: pl.BlockSpec(a.shape, lambda i: (0, 0))
    fg = fgain.reshape(1, D_MODEL)
    return pl.pallas_call(
        functools.partial(_merge_kernel, final=final),
        out_shape=jax.ShapeDtypeStruct((m, D_MODEL), F32),
        grid=(m // tm,),
        in_specs=[row(D_MODEL), row(3 * D_MODEL, C_MG // (3 * D_MODEL)), row(RET_V), row(POOL_WIDTH),
                  row(NSA_WIDTH), row(NSA_WIDTH), row(NSA_WIDTH),
                  row(LANES, C_NBG // LANES), row(NSA_WIDTH, C_NG // NSA_WIDTH),
                  const(wa), const(wb), const(wc), const(wo), const(fg)],
        out_specs=row(D_MODEL),
        compiler_params=_cparams(("parallel",)),
        name="merge",
    )(x2, z, ga, gb, o_cmp, o_sel, o_win, z, z, wa, wb, wc, wo, fg)


def _kv_rows(z, n, t):
    kv = z[:, C_KV:C_KV + 6 * NSA_KV].reshape(n, t, 3, 2, NSA_KV_HEADS, NSA_DH)
    return kv[:, :, 0], kv[:, :, 1], kv[:, :, 2]


def kernel(x_prompt, x_sample, state_ret, state_pool, cache_win, cache_cmp, cache_sel, page_table, norm_gain, w_in, w_pool, pool_scale, pe_cmp, w_ck, w_cv, w_br_a, w_br_b, w_br_c, w_out, rel_bias, final_gain):
    b, seq, _ = x_prompt.shape
    nb, dseq, _ = x_sample.shape
    depth = w_in.shape[0]
    past = page_table.shape[1] * PAGE_SIZE
    pt = page_table.astype(I32)

    w_in_p = _pack_w_in(w_in)
    w_pool_b = w_pool.astype(BF16)
    wa, wb, wc, wo = (w.astype(BF16) for w in (w_br_a, w_br_b, w_br_c, w_out))
    eye = jnp.eye(NSA_KV_HEADS, dtype=F32)
    bd_ck = jax.vmap(lambda w: jnp.kron(eye, w))(w_ck).astype(BF16)
    bd_cv = jax.vmap(lambda w: jnp.kron(eye, w))(w_cv).astype(BF16)
    pe2 = pe_cmp.reshape(depth, L_CMP, 2 * NSA_KV)
    tab = _bias_table(rel_bias)

    ret_chunk = 256 if seq % 256 == 0 else seq
    pool_tt = min(1024, seq)
    samp_nseq = 8 if nb % 8 == 0 else 1
    samp_nb = 16 if nb % 16 == 0 else 1

    hp = x_prompt.reshape(b * seq, D_MODEL)
    hs = x_sample.reshape(nb * dseq, D_MODEL)
    zero_state = jnp.zeros((b, RET_HEADS, RET_DK, RET_DV), F32)
    outs_p = [[] for _ in range(5)]
    outs_s = [[] for _ in range(5)]
    for l in range(depth):
        final = l == depth - 1
        z = _inproj(hp, norm_gain[l], w_in_p[l])
        ga, s_new = _retention(z, zero_state, n=b, t=seq, pos0=0, chunk=ret_chunk, nseq=1)
        u = z[:, C_PU:C_PU + POOL_WIDTH].reshape(b, seq, POOL_WIDTH)
        nbt = seq // pool_tt
        halo = u.reshape(b, nbt, pool_tt, POOL_WIDTH)[:, :, pool_tt - 16:, :]
        halo = jnp.concatenate([jnp.zeros((b, 1, 16, POOL_WIDTH), F32), halo[:, :nbt - 1]], axis=1)
        gb = _pool(z, halo.reshape(b * nbt, 16, POOL_WIDTH), w_pool_b[l], pool_scale[l], n=b, t=seq, pos0=0, tt=pool_tt, nb=1)
        o_cmp, o_sel, o_win = _nsa_prompt(z, pe2[l], bd_ck[l], bd_cv[l], tab, n=b, t=seq)
        cmp_new, sel_new, win_new = _kv_rows(z, b, seq)
        hp = _merge(hp, z, ga, gb, o_cmp, o_sel, o_win, wa[l], wb[l], wc[l], wo[l], final_gain, final=final)
        for j, v in enumerate((s_new, u[:, seq - POOL_BUF:], win_new[:, seq - min(WINDOW, seq):], cmp_new, sel_new)):
            outs_p[j].append(v)

        z = _inproj(hs, norm_gain[l], w_in_p[l])
        ga, s_new = _retention(z, state_ret[l], n=nb, t=dseq, pos0=past, chunk=dseq, nseq=samp_nseq)
        u = z[:, C_PU:C_PU + POOL_WIDTH].reshape(nb, dseq, POOL_WIDTH)
        ext = jnp.concatenate([state_pool[l], u], axis=1)
        halo = jnp.pad(state_pool[l], ((0, 0), (16 - POOL_BUF, 0), (0, 0)))
        gb = _pool(z, halo, w_pool_b[l], pool_scale[l], n=nb, t=dseq, pos0=past, tt=dseq, nb=samp_nb)
        o_cmp, o_sel, o_win, win_full = _nsa_sample(z, cache_cmp[l], cache_sel[l], cache_win[l], pt, pe2[l], bd_ck[l], bd_cv[l],
                                                     tab, n=nb, t=dseq, past=past)
        cmp_new, sel_new, _ = _kv_rows(z, nb, dseq)
        hs = _merge(hs, z, ga, gb, o_cmp, o_sel, o_win, wa[l], wb[l], wc[l], wo[l], final_gain, final=final)
        win_full = win_full.reshape(nb, win_full.shape[1], 2, NSA_KV_HEADS, NSA_DH)
        for j, v in enumerate((s_new, ext[:, ext.shape[1] - POOL_BUF:], win_full, cmp_new, sel_new)):
            outs_s[j].append(v)

    y_prompt = hp.reshape(b, seq, D_MODEL)
    y_sample = hs.reshape(nb, dseq, D_MODEL)
    st = lambda xs: jnp.stack(xs)
    return (y_prompt, y_sample,
            st(outs_p[0]), st(outs_s[0]), st(outs_p[1]), st(outs_s[1]), st(outs_p[2]), st(outs_s[2]),
            st(outs_p[3]), st(outs_s[3]), st(outs_p[4]), st(outs_s[4]))
```

```python
import functools
import math

import numpy as np
import jax
import jax.numpy as jnp
from jax import lax
from jax.experimental import pallas as pl
from jax.experimental.pallas import tpu as pltpu

F32 = jnp.float32
BF16 = jnp.bfloat16
I32 = jnp.int32

D_MODEL = 1024
PAGE_SIZE = 128
RET_HEADS, RET_DK, RET_DV = 4, 128, 256
RET_QK, RET_V = RET_HEADS * RET_DK, RET_HEADS * RET_DV
ROPE_BASE = 10000.0
POOL_WINDOWS = (2, 4, 8, 16)
POOL_GROUPS, POOL_GDIM = 4, 128
POOL_WIDTH = POOL_GROUPS * POOL_GDIM
POOL_BUF = 15
NSA_HEADS, NSA_KV_HEADS, NSA_DH = 8, 2, 64
NSA_GROUP = NSA_HEADS // NSA_KV_HEADS
NSA_WIDTH, NSA_KV = NSA_HEADS * NSA_DH, NSA_KV_HEADS * NSA_DH
L_CMP, L_SEL, N_SEL, WINDOW = 32, 64, 16, 512
FORCE_SCORE = 1e4
N_BUCKETS, MAX_DIST = 32, 128
EPS = 1e-6
IN_SIZES = (RET_QK, RET_QK, RET_V, RET_V, POOL_WIDTH, POOL_WIDTH, NSA_WIDTH,
            NSA_KV, NSA_KV, NSA_KV, NSA_KV, NSA_KV, NSA_KV, 3 * NSA_HEADS, NSA_WIDTH, 3 * D_MODEL)

LANES = 128
VMEM_LIMIT = 56 * 1024 * 1024

C_MG, C_RV, C_RG, C_RQ, C_RK = 0, 3072, 4096, 5120, 5632
C_PU, C_PG, C_NQ, C_NG, C_KV, C_NBG = 6144, 6656, 7168, 7680, 8192, 8960
NP = 9216
KT = 128
KPAD = 4
CHUNK = 4
EDGE = 5
NEG = -float(2.0 ** 100)
M_INIT = -1e30
LOG2E = 1.4426950408889634


def _cparams(sem):
    return pltpu.CompilerParams(dimension_semantics=sem, vmem_limit_bytes=VMEM_LIMIT)


def _nt(a, b):
    return lax.dot_general(a, b, (((1,), (1,)), ((), ())), preferred_element_type=F32)


def _tn(a, b):
    return lax.dot_general(a, b, (((0,), (0,)), ((), ())), preferred_element_type=F32)


def _dot(a, b):
    return jnp.dot(a, b, preferred_element_type=F32)


def _silu(x):
    return x * jax.nn.sigmoid(x)


def _inproj_kernel(x_ref, g_ref, w_ref, z_ref, xn_ref):
    @pl.when(pl.program_id(1) == 0)
    def _():
        x = x_ref[...]
        ms = jnp.mean(x * x, axis=-1, keepdims=True)
        xn_ref[...] = (x * lax.rsqrt(ms + EPS) * g_ref[...]).astype(BF16)

    z_ref[...] = _dot(xn_ref[...], w_ref[...])


def _inproj(x2, gain, w_p):
    m = x2.shape[0]
    tm = min(1024, m)
    tn = 1024
    return pl.pallas_call(
        _inproj_kernel,
        out_shape=jax.ShapeDtypeStruct((m, NP), F32),
        grid=(m // tm, NP // tn),
        in_specs=[pl.BlockSpec((tm, D_MODEL), lambda i, j: (i, 0)),
                  pl.BlockSpec((1, D_MODEL), lambda i, j: (0, 0)),
                  pl.BlockSpec((D_MODEL, tn), lambda i, j: (0, j))],
        out_specs=pl.BlockSpec((tm, tn), lambda i, j: (i, j)),
        scratch_shapes=[pltpu.VMEM((tm, D_MODEL), BF16)],
        compiler_params=_cparams(("parallel", "arbitrary")),
        name="inproj",
    )(x2, gain.reshape(1, D_MODEL), w_p)


def _pack_w_in(w_in):
    offs = np.cumsum((0,) + IN_SIZES)
    f = [w_in[..., offs[i]:offs[i + 1]] for i in range(len(IN_SIZES))]
    (rq, rk, rv, rg, pu, pg, nq, ck, cv, sk, sv, wk, wv, nbg, ng, mg) = f
    pad = lambda n: jnp.zeros(w_in.shape[:-1] + (n,), w_in.dtype)
    cols = [mg, rv, rg, rq, rk, pu, pg, nq, ng, ck, cv, sk, sv, wk, wv, nbg, pad(LANES - 3 * NSA_HEADS), pad(NP - C_NBG - LANES)]
    return jnp.concatenate(cols, axis=-1).astype(BF16)


def _retention_kernel(q_ref, k_ref, v_ref, g_ref, cos_ref, sin_ref, dm_ref, cross_ref, tail_ref, s0_ref,
                      ga_ref, s_ref, qr_scr, kt_scr, o_scr, *, nseq, chunk, decay):
    rows = nseq * chunk

    @pl.when(pl.program_id(1) == 0)
    def _():
        s_ref[...] = s0_ref[...]

    cos = cos_ref[...]
    sin = sin_ref[...]
    lane = lax.broadcasted_iota(I32, (rows, RET_DK), 1)
    even = (lane % 2) == 0

    def rot(x):
        sw = jnp.where(even, pltpu.roll(x, RET_DK - 1, 1), pltpu.roll(x, 1, 1))
        return x * cos + sw * sin

    for h in range(RET_HEADS):
        qr = rot(q_ref[:, h * RET_DK:(h + 1) * RET_DK])
        kr = rot(k_ref[:, h * RET_DK:(h + 1) * RET_DK]) * (RET_DK ** -0.5)
        att = _nt(qr.astype(BF16), kr.astype(BF16)) * dm_ref[h]
        o_scr[:, h * RET_DV:(h + 1) * RET_DV] = _dot(att.astype(BF16), v_ref[:, h * RET_DV:(h + 1) * RET_DV].astype(BF16))
        qr_scr[h] = qr
        kt_scr[h] = kr * tail_ref[h]

    def per_seq(j, carry):
        r0 = pl.multiple_of(j * chunk, chunk)
        for h in range(RET_HEADS):
            s = s_ref[j, h]
            cols = slice(h * RET_DV, (h + 1) * RET_DV)
            inter = _dot(qr_scr[h, pl.ds(r0, chunk), :].astype(BF16), s.astype(BF16)) * cross_ref[h]
            o_scr[pl.ds(r0, chunk), cols] = o_scr[pl.ds(r0, chunk), cols] + inter
            upd = _tn(kt_scr[h, pl.ds(r0, chunk), :].astype(BF16), v_ref[pl.ds(r0, chunk), cols].astype(BF16))
            s_ref[j, h] = s * decay[h] + upd
        return carry

    lax.fori_loop(0, nseq, per_seq, 0)

    for h in range(RET_HEADS):
        cols = slice(h * RET_DV, (h + 1) * RET_DV)
        o = o_scr[:, cols]
        o = o * lax.rsqrt(jnp.mean(o * o, axis=-1, keepdims=True) + EPS)
        ga_ref[:, cols] = (_silu(g_ref[:, cols]) * o).astype(BF16)


def _retention(z, s0, *, n, t, pos0, chunk, nseq):
    nc = t // chunk
    rows = nseq * chunk
    assert t % chunk == 0 and n % nseq == 0 and (nseq == 1 or nc == 1)
    h = np.arange(RET_HEADS, dtype=np.float64)
    log_g = np.log1p(-np.exp2(-5.0 - h))
    idx = np.arange(chunk, dtype=np.float64)
    rel = idx[:, None] - idx[None, :]
    dmask = np.where(rel[None] >= 0, np.exp(np.maximum(rel, 0.0)[None] * log_g[:, None, None]), 0.0)
    dmask = np.stack([np.kron(np.eye(nseq), dmask[i]) for i in range(RET_HEADS)])
    cross = np.exp((idx + 1.0)[None, :] * log_g[:, None])
    tail = np.exp((chunk - 1.0 - idx)[None, :] * log_g[:, None])
    decay = tuple(float(x) for x in np.exp(chunk * log_g))
    cross_b = np.broadcast_to(cross[:, :, None], (RET_HEADS, chunk, RET_DV))
    tail_b = np.broadcast_to(np.tile(tail, (1, nseq))[:, :, None], (RET_HEADS, rows, RET_DK))
    pos = pos0 + np.arange(t, dtype=np.float64)
    inv = ROPE_BASE ** (-np.arange(0, RET_DK, 2, dtype=np.float64) / RET_DK)
    ang = (pos.astype(np.float32)[:, None] * inv.astype(np.float32)[None, :]).astype(np.float64)
    cos = np.repeat(np.cos(ang), 2, axis=1)
    sgn = np.tile(np.array([-1.0, 1.0]), RET_DK // 2)
    sin = np.repeat(np.sin(ang), 2, axis=1) * sgn[None, :]
    if nseq > 1:
        cos, sin = np.tile(cos, (nseq, 1)), np.tile(sin, (nseq, 1))
    c = lambda a: jnp.asarray(a, F32)
    kern = functools.partial(_retention_kernel, nseq=nseq, chunk=chunk, decay=decay)
    row_blk = lambda i, j: i * nc + j
    ga, s_new = pl.pallas_call(
        kern,
        out_shape=(jax.ShapeDtypeStruct((n * t, RET_V), BF16), jax.ShapeDtypeStruct(s0.shape, F32)),
        grid=(n // nseq, nc),
        in_specs=[pl.BlockSpec((rows, RET_QK), lambda i, j: (row_blk(i, j), C_RQ // RET_QK)),
                  pl.BlockSpec((rows, RET_QK), lambda i, j: (row_blk(i, j), C_RK // RET_QK)),
                  pl.BlockSpec((rows, RET_V), lambda i, j: (row_blk(i, j), C_RV // RET_V)),
                  pl.BlockSpec((rows, RET_V), lambda i, j: (row_blk(i, j), C_RG // RET_V)),
                  pl.BlockSpec((rows, RET_DK), lambda i, j: (j, 0)),
                  pl.BlockSpec((rows, RET_DK), lambda i, j: (j, 0)),
                  pl.BlockSpec((RET_HEADS, rows, rows), lambda i, j: (0, 0, 0)),
                  pl.BlockSpec((RET_HEADS, chunk, RET_DV), lambda i, j: (0, 0, 0)),
                  pl.BlockSpec((RET_HEADS, rows, RET_DK), lambda i, j: (0, 0, 0)),
                  pl.BlockSpec((nseq, RET_HEADS, RET_DK, RET_DV), lambda i, j: (i, 0, 0, 0))],
        out_specs=(pl.BlockSpec((rows, RET_V), lambda i, j: (row_blk(i, j), 0)),
                   pl.BlockSpec((nseq, RET_HEADS, RET_DK, RET_DV), lambda i, j: (i, 0, 0, 0))),
        scratch_shapes=[pltpu.VMEM((RET_HEADS, rows, RET_DK), F32),
                        pltpu.VMEM((RET_HEADS, rows, RET_DK), F32),
                        pltpu.VMEM((rows, RET_V), F32)],
        compiler_params=_cparams(("parallel", "arbitrary")),
        name="retention",
    )(z, z, z, z, c(cos), c(sin), c(dmask), c(cross_b), c(tail_b), s0)
    return ga, s_new


def _pool_kernel(u_ref, pg_ref, halo_ref, w_ref, sc_ref, gb_ref, ext_scr, *, nb, tt, pos0, nbt):
    hal = 16
    ext_scr[:, 0:hal, :] = halo_ref[...]
    ext_scr[:, hal:hal + tt, :] = u_ref[...]
    pos_base = pos0 + (pl.program_id(0) * nb % nbt) * tt
    p1 = lax.broadcasted_iota(I32, (nb, tt, POOL_GDIM), 1) + pos_base + 1
    for g, w in enumerate(POOL_WINDOWS):
        cols = slice(g * POOL_GDIM, (g + 1) * POOL_GDIM)
        acc = ext_scr[:, hal:hal + tt, cols]
        for j in range(1, w):
            acc = acc + ext_scr[:, hal - j:hal - j + tt, cols]
        cnt = jnp.minimum(p1, w).astype(F32)
        d = acc / cnt - u_ref[:, :, cols]
        y = _dot(d.reshape(nb * tt, POOL_GDIM).astype(BF16), w_ref[g])
        y = y * sc_ref[:, cols]
        gate = _silu(pg_ref[:, :, cols]).reshape(nb * tt, POOL_GDIM)
        gb_ref[:, cols] = (gate * y).astype(BF16)


def _pool(z, halo, w_pool_b, scale, *, n, t, pos0, tt, nb):
    nbt = t // tt
    g_tot = n * nbt
    assert g_tot % nb == 0 and (nb == 1 or nbt == 1)
    z3 = z.reshape(g_tot, tt, NP)
    kern = functools.partial(_pool_kernel, nb=nb, tt=tt, pos0=pos0, nbt=nbt)
    return pl.pallas_call(
        kern,
        out_shape=jax.ShapeDtypeStruct((n * t, POOL_WIDTH), BF16),
        grid=(g_tot // nb,),
        in_specs=[pl.BlockSpec((nb, tt, POOL_WIDTH), lambda i: (i, 0, C_PU // POOL_WIDTH)),
                  pl.BlockSpec((nb, tt, POOL_WIDTH), lambda i: (i, 0, C_PG // POOL_WIDTH)),
                  pl.BlockSpec((nb, 16, POOL_WIDTH), lambda i: (i, 0, 0)),
                  pl.BlockSpec((POOL_GROUPS, POOL_GDIM, POOL_GDIM), lambda i: (0, 0, 0)),
                  pl.BlockSpec((1, POOL_WIDTH), lambda i: (0, 0))],
        out_specs=pl.BlockSpec((nb * tt, POOL_WIDTH), lambda i: (i, 0)),
        scratch_shapes=[pltpu.VMEM((nb, 16 + tt, POOL_WIDTH), F32)],
        compiler_params=_cparams(("parallel",)),
        name="pool",
    )(z3, z3, halo, w_pool_b, scale.reshape(1, POOL_WIDTH))


def _t5_bucket(dist):
    n = jnp.maximum(dist, 0)
    exact = N_BUCKETS // 2
    nf = jnp.maximum(n, 1).astype(F32)
    large = exact + (jnp.log(nf / exact) / math.log(MAX_DIST / exact) * (N_BUCKETS - exact)).astype(I32)
    large = jnp.minimum(large, N_BUCKETS - 1)
    return jnp.where(n < exact, n, large)


def _bias_table(rel_bias):
    rb = rel_bias.astype(F32)
    tab = rb[_t5_bucket(jnp.arange(2 * KT, dtype=I32))]
    return tab - rb[N_BUCKETS - 1][None, :]


def _toeplitz_T(tab, delta, rows_q, window=None):
    i = np.arange(rows_q)[None, :]
    j = np.arange(KT)[:, None]
    d = delta + i - j
    valid = d >= 0
    if window is not None:
        valid &= d < window
    dc = np.clip(d, 0, 2 * KT - 1)
    near = (d < 2 * KT) & valid
    vals = tab[jnp.asarray(dc)]
    vals = jnp.where(jnp.asarray(near)[:, :, None], vals, 0.0)
    vals = jnp.where(jnp.asarray(valid)[:, :, None], vals, NEG)
    vals = vals.reshape(KT, rows_q, NSA_KV_HEADS, NSA_GROUP).transpose(2, 0, 3, 1)
    return vals.reshape(NSA_KV_HEADS, KT, NSA_GROUP * rows_q)


def _cmp_rowbias(tab, pos, per_row_head):
    b = np.asarray(pos) % L_CMP
    cols = []
    for m in range(-1, 4):
        if m < 0:
            v = jnp.where(jnp.asarray(b == L_CMP - 1)[:, None], tab[0][None, :], NEG)
        else:
            v = tab[jnp.asarray(L_CMP * m + b + 1)]
        cols.append(v)
    r = jnp.stack(cols, axis=1)
    if per_row_head is None:
        r = r.reshape(r.shape[0], 5 * NSA_HEADS)
        return jnp.pad(r, ((0, 0), (0, LANES - 5 * NSA_HEADS)))
    r = r[jnp.arange(r.shape[0]), :, jnp.asarray(per_row_head)]
    return jnp.pad(r, ((0, 0), (0, LANES - 5)))


def _cmp_bias(dist, col):
    b = jnp.where(dist <= 4 * L_CMP, col(3), 0.0)
    b = jnp.where(dist <= 3 * L_CMP, col(2), b)
    b = jnp.where(dist <= 2 * L_CMP, col(1), b)
    b = jnp.where(dist <= L_CMP, col(0), b)
    return jnp.where(dist <= 0, col(-1), b)


def _select_mask(imp, sblk, qpos, k_top, axis):
    forced = (sblk == 0) | (sblk == qpos // L_SEL)
    avail = sblk * L_SEL <= qpos
    w = jnp.where(forced, FORCE_SCORE, imp)
    w = jnp.where(avail, w, -jnp.inf)
    idx = lax.broadcasted_iota(I32, w.shape, axis).astype(F32)
    sel = jnp.zeros(w.shape, F32)
    for _ in range(k_top):
        mx = jnp.max(w, axis=axis, keepdims=True)
        first = jnp.min(jnp.where(w == mx, idx, float(4 * LANES)), axis=axis, keepdims=True)
        pick = idx == first
        sel = jnp.where(pick, 1.0, sel)
        w = jnp.where(pick, -jnp.inf, w)
    return jnp.where((sel > 0.0) & avail, 0.0, NEG)


def _cmpkv_kernel(kv_ref, pe_ref, wk_ref, wv_ref, kc_ref, vc_ref):
    x = kv_ref[...]
    nblk = x.shape[0] // L_CMP
    cm = x.reshape(nblk, L_CMP, 2 * NSA_KV).sum(axis=1) * (1.0 / L_CMP)
    cm = cm + jnp.mean(pe_ref[...], axis=0, keepdims=True)
    kc_ref[...] = _dot(cm[:, :NSA_KV].astype(BF16), wk_ref[...])
    vc_ref[...] = _dot(cm[:, NSA_KV:].astype(BF16), wv_ref[...])


def _cmpkv(z, pe2, bd_ck, bd_cv, *, n, t):
    nblk = t // L_CMP
    z3 = z.reshape(n, t, NP)
    return pl.pallas_call(
        _cmpkv_kernel,
        out_shape=(jax.ShapeDtypeStruct((n, nblk, NSA_KV), F32), jax.ShapeDtypeStruct((n, nblk, NSA_KV), F32)),
        grid=(n,),
        in_specs=[pl.BlockSpec((None, nblk * L_CMP, 2 * NSA_KV), lambda i: (i, 0, C_KV // (2 * NSA_KV))),
                  pl.BlockSpec((L_CMP, 2 * NSA_KV), lambda i: (0, 0)),
                  pl.BlockSpec((NSA_KV, NSA_KV), lambda i: (0, 0)),
                  pl.BlockSpec((NSA_KV, NSA_KV), lambda i: (0, 0))],
        out_specs=(pl.BlockSpec((None, nblk, NSA_KV), lambda i: (i, 0, 0)),
                   pl.BlockSpec((None, nblk, NSA_KV), lambda i: (i, 0, 0))),
        compiler_params=_cparams(("parallel",)),
        name="cmp_kv",
    )(z3, pe2, bd_ck, bd_cv)


def _cmpsel_kernel(q_ref, kc_ref, vct_ref, rbt_ref, cpos_ref, sblk_ref, oc_ref, mn_ref, *, tq, k_top):
    qs = pl.program_id(1) * tq
    qpos = lax.broadcasted_iota(I32, (1, tq), 1) + qs
    dist = qpos - cpos_ref[...]
    valid = dist >= 0
    rbt = rbt_ref[...]
    kc = kc_ref[...].astype(BF16)
    vct = vct_ref[...].astype(BF16)
    for g in range(NSA_KV_HEADS):
        imp = jnp.zeros((2 * LANES, tq), F32)
        kg = kc[:, g * NSA_DH:(g + 1) * NSA_DH]
        vg = vct[g * NSA_DH:(g + 1) * NSA_DH, :]
        outs = []
        for r in range(NSA_GROUP):
            h = g * NSA_GROUP + r
            q = (q_ref[:, h * NSA_DH:(h + 1) * NSA_DH] * (NSA_DH ** -0.5)).astype(BF16)
            s = _nt(kg, q) + _cmp_bias(dist, lambda m, h=h: rbt[(m + 1) * NSA_HEADS + h:(m + 1) * NSA_HEADS + h + 1, :])
            s = jnp.where(valid, s, M_INIT)
            mx = jnp.max(s, axis=0, keepdims=True)
            e = jnp.where(valid, jnp.exp(s - mx), 0.0)
            p = e / jnp.maximum(jnp.sum(e, axis=0, keepdims=True), 1e-30)
            outs.append(_dot(vg, p.astype(BF16)))
            imp = imp + p
        for k in range(NSA_GROUP // 2):
            pair = jnp.concatenate([outs[2 * k], outs[2 * k + 1]], axis=0).T
            c0 = (g * NSA_GROUP + 2 * k) * NSA_DH
            oc_ref[:, c0:c0 + 2 * NSA_DH] = pair
        imp = imp[:LANES] + imp[LANES:]
        mn_ref[g] = _select_mask(imp, sblk_ref[...], qpos, k_top, 0).astype(BF16)


def _cmpsel(z, kcp, vctp, rowbt, cpos, sblk, *, n, t, k_top):
    tq = KT
    nqt = t // tq
    kern = functools.partial(_cmpsel_kernel, tq=tq, k_top=k_top)
    return pl.pallas_call(
        kern,
        out_shape=(jax.ShapeDtypeStruct((n * t, NSA_WIDTH), F32),
                   jax.ShapeDtypeStruct((n, NSA_KV_HEADS, LANES, t), BF16)),
        grid=(n, nqt),
        in_specs=[pl.BlockSpec((tq, NSA_WIDTH), lambda b, i: (b * nqt + i, C_NQ // NSA_WIDTH)),
                  pl.BlockSpec((None, 2 * LANES, NSA_KV), lambda b, i: (b, 0, 0)),
                  pl.BlockSpec((None, NSA_KV, 2 * LANES), lambda b, i: (b, 0, 0)),
                  pl.BlockSpec((LANES, tq), lambda b, i: (0, 0)),
                  pl.BlockSpec((2 * LANES, tq), lambda b, i: (0, 0)),
                  pl.BlockSpec((LANES, tq), lambda b, i: (0, 0))],
        out_specs=(pl.BlockSpec((tq, NSA_WIDTH), lambda b, i: (b * nqt + i, 0)),
                   pl.BlockSpec((None, NSA_KV_HEADS, LANES, tq), lambda b, i: (b, 0, 0, i))),
        compiler_params=_cparams(("parallel", "parallel")),
        name="cmp_select",
    )(z, kcp, vctp, rowbt, cpos, sblk)


def _kvpack_kernel(z_ref, oh_ref, ks_ref, vs_ref, vs4_ref, kw_ref, vw_ref, *, nqt):
    i = pl.program_id(1)
    real = (i >= KPAD) & (i < KPAD + nqt)

    @pl.when(real)
    def _():
        kv = z_ref[...]
        for k_ref, v_refs, kcol, lead in ((ks_ref, (vs_ref, vs4_ref), 2 * NSA_KV, oh_ref[...]),
                                          (kw_ref, (vw_ref,), 4 * NSA_KV, jnp.zeros((KT, LANES), BF16))):
            vt = kv[:, kcol + NSA_KV:kcol + 2 * NSA_KV].T.astype(BF16)
            for g in range(NSA_KV_HEADS):
                k_ref[g, :, 0:LANES] = lead
                k_ref[g, :, LANES:LANES + NSA_DH] = kv[:, kcol + g * NSA_DH:kcol + (g + 1) * NSA_DH].astype(BF16)
                for v_ref in v_refs:
                    v_ref[g] = vt[g * NSA_DH:(g + 1) * NSA_DH, :]

    @pl.when(jnp.logical_not(real))
    def _():
        for ref in (ks_ref, vs_ref, vs4_ref, kw_ref, vw_ref):
            ref[...] = jnp.zeros(ref.shape, BF16)


def _kvpack(z, onehot, *, n, t):
    nqt = t // KT
    ntile = nqt + 2 * KPAD
    assert ntile % CHUNK == 0
    src = lambda b, i: b * nqt + jnp.clip(i - KPAD, 0, nqt - 1)
    kshape = jax.ShapeDtypeStruct((n, NSA_KV_HEADS, ntile, KT, LANES + NSA_DH), BF16)
    vshape = jax.ShapeDtypeStruct((n, NSA_KV_HEADS, ntile, NSA_DH, KT), BF16)
    v4shape = jax.ShapeDtypeStruct((n, NSA_KV_HEADS, ntile // CHUNK, NSA_DH, CHUNK * KT), BF16)
    kspec = pl.BlockSpec((None, NSA_KV_HEADS, None, KT, LANES + NSA_DH), lambda b, i: (b, 0, i, 0, 0))
    vspec = pl.BlockSpec((None, NSA_KV_HEADS, None, NSA_DH, KT), lambda b, i: (b, 0, i, 0, 0))
    v4spec = pl.BlockSpec((None, NSA_KV_HEADS, None, NSA_DH, KT), lambda b, i: (b, 0, i // CHUNK, 0, i % CHUNK))
    return pl.pallas_call(
        functools.partial(_kvpack_kernel, nqt=nqt),
        out_shape=(kshape, vshape, v4shape, kshape, vshape),
        grid=(n, ntile),
        in_specs=[pl.BlockSpec((KT, 1024), lambda b, i: (src(b, i), C_KV // 1024)),
                  pl.BlockSpec((KT, LANES), lambda b, i: (jnp.clip(i - KPAD, 0, nqt - 1), 0))],
        out_specs=(kspec, vspec, v4spec, kspec, vspec),
        compiler_params=_cparams(("parallel", "parallel")),
        name="kv_pack",
    )(z, onehot)


def _qk_chunk(k_ref, t0, ntile, qpt_ref):
    k = k_ref[pl.ds(t0, ntile)].reshape(ntile * KT, LANES + NSA_DH)
    return _dot(k, qpt_ref[...])


def _online_softmax(st, pv, m, l, acc):
    m_new = jnp.maximum(m, jnp.max(st, axis=0, keepdims=True))
    alpha = jnp.exp2(m - m_new)
    p = jnp.exp2(st - m_new)
    l_new = alpha * l + jnp.sum(p, axis=0, keepdims=True)
    return m_new, l_new, alpha * acc + pv(p.astype(BF16))


def _pv_tiles(v_ref, t0, ntile):
    def pv(pb):
        out = _dot(v_ref[t0], pb[0:KT])
        for j in range(1, ntile):
            out = out + _dot(v_ref[t0 + j], pb[j * KT:(j + 1) * KT])
        return out
    return pv


def _attn_store(o_ref, ot):
    for k in range(NSA_GROUP // 2):
        o_ref[:, 2 * k * NSA_DH:(2 * k + 2) * NSA_DH] = jnp.concatenate(
            [ot[:, 2 * k * KT:(2 * k + 1) * KT], ot[:, (2 * k + 1) * KT:(2 * k + 2) * KT]], axis=0).T


def _selwin_kernel(q_ref, mn_ref, ks_ref, vs_ref, vs4_ref, kw_ref, vw_ref, asel_ref, awin_ref, os_ref, ow_ref,
                   qpt_scr, sta_scr, stb_scr, m_scr, l_scr, acc_scr):
    qt = pl.program_id(2)
    mn = mn_ref[...]
    for k in range(NSA_GROUP // 2):
        qt2 = (q_ref[:, 2 * k * NSA_DH:(2 * k + 2) * NSA_DH] * (NSA_DH ** -0.5 * LOG2E)).T.astype(BF16)
        for u in range(2):
            r = 2 * k + u
            qpt_scr[0:LANES, r * KT:(r + 1) * KT] = mn
            qpt_scr[LANES:LANES + NSA_DH, r * KT:(r + 1) * KT] = qt2[u * NSA_DH:(u + 1) * NSA_DH]

    m_scr[...] = jnp.full(m_scr.shape, M_INIT, F32)
    l_scr[...] = jnp.zeros(l_scr.shape, F32)
    acc_scr[...] = jnp.zeros(acc_scr.shape, F32)
    nplain = jnp.maximum(qt - 1, 0) // CHUNK

    @pl.when(nplain > 0)
    def _():
        sta_scr[...] = _qk_chunk(ks_ref, KPAD, CHUNK, qpt_scr)

    def update(st_ref, c):
        m, l, acc = _online_softmax(st_ref[...], lambda pb: _dot(vs4_ref[KPAD // CHUNK + c], pb),
                                    m_scr[...], l_scr[...], acc_scr[...])
        m_scr[...] = m
        l_scr[...] = l
        acc_scr[...] = acc

    def pair(i, carry):
        c0 = 2 * i
        stb_scr[...] = _qk_chunk(ks_ref, KPAD + CHUNK * (c0 + 1), CHUNK, qpt_scr)
        update(sta_scr, c0)
        sta_scr[...] = _qk_chunk(ks_ref, KPAD + CHUNK * jnp.minimum(c0 + 2, nplain - 1), CHUNK, qpt_scr)
        update(stb_scr, c0 + 1)
        return carry

    lax.fori_loop(0, nplain // 2, pair, 0)

    @pl.when(nplain % 2 == 1)
    def _():
        update(sta_scr, nplain - 1)

    e0 = CHUNK * nplain
    off = pl.multiple_of((EDGE - (qt - e0)) * KT, KT)
    st = _qk_chunk(ks_ref, KPAD + e0, EDGE, qpt_scr) + asel_ref[pl.ds(off, EDGE * KT), :]
    _, l, acc = _online_softmax(st, _pv_tiles(vs_ref, KPAD + e0, EDGE), m_scr[...], l_scr[...], acc_scr[...])
    _attn_store(os_ref, acc / l)

    st = _qk_chunk(kw_ref, qt, EDGE, qpt_scr) + awin_ref[...]
    rows = []
    for j in range(EDGE):
        blk = st[j * KT:(j + 1) * KT]
        if j < EDGE - 1:
            blk = blk + jnp.where(qt >= EDGE - 1 - j, 0.0, NEG)
        rows.append(blk)
    st = jnp.concatenate(rows, axis=0)
    zero = jnp.zeros((1, NSA_GROUP * KT), F32)
    _, l, acc = _online_softmax(st, _pv_tiles(vw_ref, qt, EDGE), jnp.full((1, NSA_GROUP * KT), M_INIT, F32), zero,
                                jnp.zeros((NSA_DH, NSA_GROUP * KT), F32))
    _attn_store(ow_ref, acc / l)


def _selwin(z, mnt, ks, vs, vs4, kw, vw, asel, awin, *, n, t):
    nqt = t // KT
    ntile = nqt + 2 * KPAD
    gw = NSA_GROUP * NSA_DH
    gq = NSA_GROUP * KT
    kspec = pl.BlockSpec((None, None, ntile, KT, LANES + NSA_DH), lambda b, g, i: (b, g, 0, 0, 0))
    vspec = pl.BlockSpec((None, None, ntile, NSA_DH, KT), lambda b, g, i: (b, g, 0, 0, 0))
    v4spec = pl.BlockSpec((None, None, ntile // CHUNK, NSA_DH, CHUNK * KT), lambda b, g, i: (b, g, 0, 0, 0))
    ospec = pl.BlockSpec((KT, gw), lambda b, g, i: (b * nqt + i, g))
    return pl.pallas_call(
        _selwin_kernel,
        out_shape=(jax.ShapeDtypeStruct((n * t, NSA_WIDTH), F32), jax.ShapeDtypeStruct((n * t, NSA_WIDTH), F32)),
        grid=(n, NSA_KV_HEADS, nqt),
        in_specs=[pl.BlockSpec((KT, gw), lambda b, g, i: (b * nqt + i, C_NQ // gw + g)),
                  pl.BlockSpec((None, None, LANES, KT), lambda b, g, i: (b, g, 0, i)),
                  kspec, vspec, v4spec, kspec, vspec,
                  pl.BlockSpec((None, 2 * EDGE * KT, gq), lambda b, g, i: (g, 0, 0)),
                  pl.BlockSpec((None, EDGE * KT, gq), lambda b, g, i: (g, 0, 0))],
        out_specs=(ospec, ospec),
        scratch_shapes=[pltpu.VMEM((LANES + NSA_DH, gq), BF16),
                        pltpu.VMEM((CHUNK * KT, gq), F32),
                        pltpu.VMEM((CHUNK * KT, gq), F32),
                        pltpu.VMEM((1, gq), F32),
                        pltpu.VMEM((1, gq), F32),
                        pltpu.VMEM((NSA_DH, gq), F32)],
        compiler_params=_cparams(("parallel", "parallel", "arbitrary")),
        name="sel_win",
    )(z, mnt, ks, vs, vs4, kw, vw, asel, awin)


def _nsa_prompt(z, pe2, bd_ck, bd_cv, tab, *, n, t):
    assert t % KT == 0 and WINDOW == (EDGE - 1) * KT and KT % L_SEL == 0 and EDGE <= KPAD + 1 and EDGE <= CHUNK + 1
    nblk = t // L_CMP
    n_sel = -(-t // L_SEL)
    assert nblk <= 2 * LANES and n_sel <= LANES and nblk % 2 == 0
    kc, vc = _cmpkv(z, pe2, bd_ck, bd_cv, n=n, t=t)
    half = nblk // 2

    def perm(a):
        z0 = jnp.zeros((n, LANES - half, NSA_KV), F32)
        return jnp.concatenate([a[:, 0::2], z0, a[:, 1::2], z0], axis=1)

    slot = np.arange(LANES)
    big = 1 << 20
    cpos = np.concatenate([np.where(slot < half, (2 * slot) * L_CMP + L_CMP - 1, big),
                           np.where(slot < half, (2 * slot + 1) * L_CMP + L_CMP - 1, big)])
    cpos = np.broadcast_to(cpos[:, None], (2 * LANES, KT))
    sblk = np.broadcast_to(np.where(slot < n_sel, slot, big)[:, None], (LANES, KT))
    rowbt = _cmp_rowbias(tab, np.arange(KT), None).T
    o_cmp, mnt = _cmpsel(z, perm(kc), perm(vc).transpose(0, 2, 1), rowbt, jnp.asarray(cpos, I32), jnp.asarray(sblk, I32),
                         n=n, t=t, k_top=min(N_SEL, n_sel))

    onehot = jnp.asarray(np.arange(t)[:, None] // L_SEL == np.arange(LANES)[None, :], BF16)
    ks, vs, vs4, kw, vw = _kvpack(z, onehot, n=n, t=t)
    a0 = _toeplitz_T(tab, 0, KT) * LOG2E
    a1 = _toeplitz_T(tab, KT, KT) * LOG2E
    aw = _toeplitz_T(tab, WINDOW, KT, window=WINDOW) * LOG2E
    zeros = jnp.zeros_like(a0)
    asel = jnp.concatenate([zeros] * (EDGE - 1) + [a1, a0] + [jnp.full_like(a0, NEG)] * (EDGE - 1), axis=1)
    awin = jnp.concatenate([aw] + [zeros] * (EDGE - 3) + [a1, a0], axis=1)
    o_sel, o_win = _selwin(z, mnt, ks, vs, vs4, kw, vw, asel, awin, n=n, t=t)
    return o_cmp, o_sel, o_win


def _nsa_sample_kernel(pt_ref, *refs, npg, t, past, wbuf, k_top):
    del pt_ref
    cpages = refs[0:npg]
    spages = refs[npg:2 * npg]
    (q_ref, kvn_ref, cw_ref, pe_ref, wk_ref, wv_ref, rb_ref, cpos_ref, sblk_ref, oh_ref, asel_ref, awin_ref,
     oc_ref, os_ref, ow_ref, wout_ref,
     ksel, vsel, kwin, vwin, cm_scr, kcv_scr, qw_scr, qp_scr) = refs[2 * npg:]
    rows = NSA_HEADS * t
    half = NSA_GROUP * t
    lsel = ksel.shape[0]
    lwin = kwin.shape[0]
    kvn = kvn_ref[...]

    @pl.when(pl.program_id(0) == 0)
    def _():
        ksel[:, 0:LANES] = oh_ref[...]
        ksel[past:lsel, LANES:2 * LANES] = jnp.zeros((lsel - past, LANES), BF16)
        vsel[past:lsel, :] = jnp.zeros((lsel - past, LANES), BF16)
        kwin[wbuf:lwin, :] = jnp.zeros((lwin - wbuf, LANES), BF16)
        vwin[wbuf:lwin, :] = jnp.zeros((lwin - wbuf, LANES), BF16)
        kcv_scr[...] = jnp.zeros(kcv_scr.shape, F32)
        qw_scr[...] = jnp.zeros(qw_scr.shape, F32)

    bpp = PAGE_SIZE // L_CMP
    for p in range(npg):
        sp = spages[p]
        ksel[p * PAGE_SIZE:(p + 1) * PAGE_SIZE, LANES:2 * LANES] = sp[:, 0:NSA_KV].astype(BF16)
        vsel[p * PAGE_SIZE:(p + 1) * PAGE_SIZE, :] = sp[:, NSA_KV:2 * NSA_KV].astype(BF16)
        cm_scr[p * bpp:(p + 1) * bpp, :] = cpages[p][...].reshape(bpp, L_CMP, 2 * NSA_KV).sum(axis=1) * (1.0 / L_CMP)
    ksel[past:past + t, LANES:2 * LANES] = kvn[:, 2 * NSA_KV:3 * NSA_KV].astype(BF16)
    vsel[past:past + t, :] = kvn[:, 3 * NSA_KV:4 * NSA_KV].astype(BF16)
    cw = cw_ref[...]
    kwin[0:wbuf, :] = cw[:, 0:NSA_KV].astype(BF16)
    vwin[0:wbuf, :] = cw[:, NSA_KV:2 * NSA_KV].astype(BF16)
    kwin[wbuf:wbuf + t, :] = kvn[:, 4 * NSA_KV:5 * NSA_KV].astype(BF16)
    vwin[wbuf:wbuf + t, :] = kvn[:, 5 * NSA_KV:6 * NSA_KV].astype(BF16)
    keep = wout_ref.shape[0]
    wout_ref[0:keep - t, :] = cw[wbuf + t - keep:wbuf, :]
    wout_ref[keep - t:keep, :] = kvn[:, 4 * NSA_KV:6 * NSA_KV]

    for h in range(NSA_HEADS):
        g = h // NSA_GROUP
        qh = q_ref[:, h * NSA_DH:(h + 1) * NSA_DH] * (NSA_DH ** -0.5)
        qw_scr[h * t:(h + 1) * t, g * NSA_DH:(g + 1) * NSA_DH] = qh
    qw = qw_scr[...].astype(BF16)

    def pick(x):
        row = lax.broadcasted_iota(I32, (rows, NSA_DH), 0)
        return jnp.where(row < half, x[:, :NSA_DH], x[:, NSA_DH:])

    def store(o_ref, x):
        for h in range(NSA_HEADS):
            o_ref[:, h * NSA_DH:(h + 1) * NSA_DH] = x[h * t:(h + 1) * t, :]

    nblk = npg * bpp
    cm = cm_scr[...] + jnp.mean(pe_ref[...], axis=0, keepdims=True)
    kcv_scr[0:nblk, 0:NSA_KV] = _dot(cm[:, :NSA_KV].astype(BF16), wk_ref[...])
    kcv_scr[0:nblk, NSA_KV:] = _dot(cm[:, NSA_KV:].astype(BF16), wv_ref[...])
    kc = kcv_scr[:, 0:NSA_KV].astype(BF16)
    vc = kcv_scr[:, NSA_KV:].astype(BF16)
    qpos = lax.broadcasted_iota(I32, (rows, 1), 0) % t + past
    dist = qpos - cpos_ref[...]
    valid = dist >= 0
    rb = rb_ref[...]
    s = _nt(qw, kc) + _cmp_bias(dist, lambda m: rb[:, m + 1:m + 2])
    s = jnp.where(valid, s, M_INIT)
    mx = jnp.max(s, axis=-1, keepdims=True)
    e = jnp.where(valid, jnp.exp(s - mx), 0.0)
    p = e / jnp.maximum(jnp.sum(e, axis=-1, keepdims=True), 1e-30)
    store(oc_ref, pick(_dot(p.astype(BF16), vc)))

    imps = []
    for g in range(NSA_KV_HEADS):
        imp = p[g * half:g * half + t]
        for r in range(1, NSA_GROUP):
            imp = imp + p[g * half + r * t:g * half + (r + 1) * t]
        imps.append(imp + pltpu.roll(imp, LANES - 1, 1))
    imps.append(jnp.zeros((LANES - NSA_KV_HEADS * t, LANES), F32))
    imp_t = jnp.concatenate(imps, axis=0).T
    qpos_l = lax.broadcasted_iota(I32, (1, LANES), 1) % t + past
    mn = _select_mask(imp_t, sblk_ref[...], qpos_l, k_top, 0).T.astype(BF16)
    for g in range(NSA_KV_HEADS):
        for r in range(NSA_GROUP):
            qp_scr[g * half + r * t:g * half + (r + 1) * t, 0:LANES] = mn[g * t:(g + 1) * t]
    qp_scr[:, LANES:2 * LANES] = qw

    def attend(qmat, k_all, v_all, add):
        sc = _nt(qmat, k_all) + add
        mm = jnp.max(sc, axis=-1, keepdims=True)
        ee = jnp.exp(sc - mm)
        ll = jnp.sum(ee, axis=-1, keepdims=True)
        return pick(_dot(ee.astype(BF16), v_all)) / ll

    store(os_ref, attend(qp_scr[...], ksel[...], vsel[...], asel_ref[...]))
    store(ow_ref, attend(qw, kwin[...], vwin[...], awin_ref[...]))


def _nsa_sample(z, cache_cmp_l, cache_sel_l, cache_win_l, page_table, pe2, bd_ck, bd_cv, tab, *, n, t, past):
    npg = page_table.shape[1]
    assert past == npg * PAGE_SIZE and past % L_SEL == 0 and past % L_CMP == 0
    wbuf = cache_win_l.shape[1]
    keep = min(WINDOW, wbuf + t)
    nblk = past // L_CMP
    assert t < L_CMP and nblk <= LANES
    length = past + t
    n_sel = -(-length // L_SEL)
    assert 2 * (n_sel - 1) < LANES and NSA_KV_HEADS * t <= LANES
    rows = NSA_HEADS * t
    lsel = -(-(past + t) // KT) * KT
    lwin = -(-(wbuf + t) // KT) * KT
    big = 1 << 20
    slot = np.arange(LANES)
    cpos = np.where(slot < nblk, slot * L_CMP + L_CMP - 1, big)[None, :]
    sblk = np.broadcast_to(np.where((slot % 2 == 0) & (slot // 2 < n_sel), slot // 2, big)[:, None], (LANES, LANES))
    head_of_row = np.repeat(np.arange(NSA_HEADS), t)
    tok_of_row = np.tile(np.arange(t), NSA_HEADS)
    rowb = _cmp_rowbias(tab, past + tok_of_row, head_of_row)
    kpos_sel = np.arange(lsel)
    onehot = jnp.asarray((kpos_sel[:, None] < length) & (2 * (kpos_sel[:, None] // L_SEL) == slot[None, :]), BF16)

    def additive(kpos, exists, window):
        d = (past + np.arange(t))[:, None] - kpos[None, :]
        valid = (d >= 0) & exists[None, :]
        if window is not None:
            valid &= d < window
        near = valid & (d < 2 * KT)
        j0 = int(np.nonzero(near.any(axis=0))[0].min()) // KT * KT
        vals = tab[jnp.asarray(np.clip(d[:, j0:], 0, 2 * KT - 1))]
        vals = jnp.where(jnp.asarray(near[:, j0:])[:, :, None], vals, 0.0)
        vals = jnp.pad(vals, ((0, 0), (j0, 0), (0, 0)))
        vals = jnp.where(jnp.asarray(valid)[:, :, None], vals, NEG)
        return vals.transpose(2, 0, 1).reshape(NSA_HEADS * t, kpos.shape[0])

    asel = additive(kpos_sel, kpos_sel < length, None)
    kidx = np.arange(lwin)
    awin = additive(past - wbuf + kidx, kidx < wbuf + t, WINDOW)

    cmp3 = cache_cmp_l.reshape(cache_cmp_l.shape[0], PAGE_SIZE, 2 * NSA_KV)
    sel3 = cache_sel_l.reshape(cache_sel_l.shape[0], PAGE_SIZE, 2 * NSA_KV)
    win3 = cache_win_l.reshape(n, wbuf, 2 * NSA_KV)
    page_spec = lambda p: pl.BlockSpec((None, PAGE_SIZE, 2 * NSA_KV), lambda i, pt, p=p: (pt[i, p], 0, 0))
    const = lambda shape: pl.BlockSpec(shape, lambda i, pt: tuple(0 for _ in shape))
    ospec = pl.BlockSpec((t, NSA_WIDTH), lambda i, pt: (i, 0))
    kern = functools.partial(_nsa_sample_kernel, npg=npg, t=t, past=past, wbuf=wbuf, k_top=min(N_SEL, n_sel))
    grid_spec = pltpu.PrefetchScalarGridSpec(
        num_scalar_prefetch=1,
        grid=(n,),
        in_specs=[page_spec(p) for p in range(npg)] + [page_spec(p) for p in range(npg)] + [
            pl.BlockSpec((t, NSA_WIDTH), lambda i, pt: (i, C_NQ // NSA_WIDTH)),
            pl.BlockSpec((t, 1024), lambda i, pt: (i, C_KV // 1024)),
            pl.BlockSpec((None, wbuf, 2 * NSA_KV), lambda i, pt: (i, 0, 0)),
            const((L_CMP, 2 * NSA_KV)), const((NSA_KV, NSA_KV)), const((NSA_KV, NSA_KV)),
            const((rows, LANES)), const((1, LANES)), const((LANES, LANES)),
            const((lsel, LANES)), const((rows, lsel)), const((rows, lwin))],
        out_specs=(ospec, ospec, ospec, pl.BlockSpec((None, keep, 2 * NSA_KV), lambda i, pt: (i, 0, 0))),
        scratch_shapes=[pltpu.VMEM((lsel, 2 * LANES), BF16), pltpu.VMEM((lsel, LANES), BF16),
                        pltpu.VMEM((lwin, LANES), BF16), pltpu.VMEM((lwin, LANES), BF16),
                        pltpu.VMEM((nblk, 2 * NSA_KV), F32), pltpu.VMEM((LANES, 2 * NSA_KV), F32),
                        pltpu.VMEM((rows, NSA_KV), F32),
                        pltpu.VMEM((rows, 2 * LANES), BF16)])
    o_cmp, o_sel, o_win, win_new = pl.pallas_call(
        kern,
        grid_spec=grid_spec,
        out_shape=(jax.ShapeDtypeStruct((n * t, NSA_WIDTH), F32),) * 3 + (jax.ShapeDtypeStruct((n, keep, 2 * NSA_KV), F32),),
        compiler_params=_cparams(("arbitrary",)),
        name="nsa_sample",
    )(page_table, *([cmp3] * npg), *([sel3] * npg), z, z, win3, pe2, bd_ck, bd_cv, rowb,
      jnp.asarray(cpos, I32), jnp.asarray(sblk, I32), onehot, asel, awin)
    return o_cmp, o_sel, o_win, win_new


def _merge_kernel(x_ref, mg_ref, ga_ref, gb_ref, oc_ref, os_ref, ow_ref, nbg_ref, ng_ref,
                  wa_ref, wb_ref, wc_ref, wo_ref, fg_ref, y_ref, *, final):
    tm = x_ref.shape[0]
    gates = jax.nn.sigmoid(nbg_ref[...])
    lane = lax.broadcasted_iota(I32, (tm, LANES), 1)
    low = lane < NSA_DH
    parts = []
    for pi in range(NSA_HEADS // 2):
        cols = slice(pi * LANES, (pi + 1) * LANES)
        acc = jnp.zeros((tm, LANES), F32)
        for b, o_ref in enumerate((oc_ref, os_ref, ow_ref)):
            c0 = b * NSA_HEADS + 2 * pi
            gt = jnp.where(low, gates[:, c0:c0 + 1], gates[:, c0 + 1:c0 + 2])
            acc = acc + gt * o_ref[:, cols]
        parts.append((_silu(ng_ref[:, cols]) * acc).astype(BF16))
    gc = jnp.concatenate(parts, axis=1)
    b_a = _dot(ga_ref[...], wa_ref[...])
    b_b = _dot(gb_ref[...], wb_ref[...])
    b_c = _dot(gc, wc_ref[...])
    mg = mg_ref[...]
    merged = (jax.nn.sigmoid(mg[:, 0:D_MODEL]) * b_a + jax.nn.sigmoid(mg[:, D_MODEL:2 * D_MODEL]) * b_b
              + jax.nn.sigmoid(mg[:, 2 * D_MODEL:]) * b_c)
    y = x_ref[...] + _dot(merged.astype(BF16), wo_ref[...])
    if final:
        y = y * lax.rsqrt(jnp.mean(y * y, axis=-1, keepdims=True) + EPS) * fg_ref[...]
    y_ref[...] = y


def _merge(x2, z, ga, gb, o_cmp, o_sel, o_win, wa, wb, wc, wo, fgain, *, final):
    m = x2.shape[0]
    tm = min(256, m)
    row = lambda w, c=0: pl.BlockSpec((tm, w), lambda i, c=c: (i, c))
    const = lambda a: pl.BlockSpec(a.shape, lambda i: (0, 0))
    fg = fgain.reshape(1, D_MODEL)
    return pl.pallas_call(
        functools.partial(_merge_kernel, final=final),
        out_shape=jax.ShapeDtypeStruct((m, D_MODEL), F32),
        grid=(m // tm,),
        in_specs=[row(D_MODEL), row(3 * D_MODEL, C_MG // (3 * D_MODEL)), row(RET_V), row(POOL_WIDTH),
                  row(NSA_WIDTH), row(NSA_WIDTH), row(NSA_WIDTH),
                  row(LANES, C_NBG // LANES), row(NSA_WIDTH, C_NG // NSA_WIDTH),
                  const(wa), const(wb), const(wc), const(wo), const(fg)],
        out_specs=row(D_MODEL),
        compiler_params=_cparams(("parallel",)),
        name="merge",
    )(x2, z, ga, gb, o_cmp, o_sel, o_win, z, z, wa, wb, wc, wo, fg)


def _kv_rows(z, n, t):
    kv = z[:, C_KV:C_KV + 6 * NSA_KV].reshape(n, t, 3, 2, NSA_KV_HEADS, NSA_DH)
    return kv[:, :, 0], kv[:, :, 1], kv[:, :, 2]


def kernel(x_prompt, x_sample, state_ret, state_pool, cache_win, cache_cmp, cache_sel, page_table, norm_gain, w_in, w_pool, pool_scale, pe_cmp, w_ck, w_cv, w_br_a, w_br_b, w_br_c, w_out, rel_bias, final_gain):
    b, seq, _ = x_prompt.shape
    nb, dseq, _ = x_sample.shape
    depth = w_in.shape[0]
    past = page_table.shape[1] * PAGE_SIZE
    pt = page_table.astype(I32)

    w_in_p = _pack_w_in(w_in)
    w_pool_b = w_pool.astype(BF16)
    wa, wb, wc, wo = (w.astype(BF16) for w in (w_br_a, w_br_b, w_br_c, w_out))
    eye = jnp.eye(NSA_KV_HEADS, dtype=F32)
    bd_ck = jax.vmap(lambda w: jnp.kron(eye, w))(w_ck).astype(BF16)
    bd_cv = jax.vmap(lambda w: jnp.kron(eye, w))(w_cv).astype(BF16)
    pe2 = pe_cmp.reshape(depth, L_CMP, 2 * NSA_KV)
    tab = _bias_table(rel_bias)

    ret_chunk = 256 if seq % 256 == 0 else seq
    pool_tt = min(1024, seq)
    samp_nseq = 8 if nb % 8 == 0 else 1
    samp_nb = 16 if nb % 16 == 0 else 1

    hp = x_prompt.reshape(b * seq, D_MODEL)
    hs = x_sample.reshape(nb * dseq, D_MODEL)
    zero_state = jnp.zeros((b, RET_HEADS, RET_DK, RET_DV), F32)
    outs_p = [[] for _ in range(5)]
    outs_s = [[] for _ in range(5)]
    for l in range(depth):
        final = l == depth - 1
        z = _inproj(hp, norm_gain[l], w_in_p[l])
        ga, s_new = _retention(z, zero_state, n=b, t=seq, pos0=0, chunk=ret_chunk, nseq=1)
        u = z[:, C_PU:C_PU + POOL_WIDTH].reshape(b, seq, POOL_WIDTH)
        nbt = seq // pool_tt
        halo = u.reshape(b, nbt, pool_tt, POOL_WIDTH)[:, :, pool_tt - 16:, :]
        halo = jnp.concatenate([jnp.zeros((b, 1, 16, POOL_WIDTH), F32), halo[:, :nbt - 1]], axis=1)
        gb = _pool(z, halo.reshape(b * nbt, 16, POOL_WIDTH), w_pool_b[l], pool_scale[l], n=b, t=seq, pos0=0, tt=pool_tt, nb=1)
        o_cmp, o_sel, o_win = _nsa_prompt(z, pe2[l], bd_ck[l], bd_cv[l], tab, n=b, t=seq)
        cmp_new, sel_new, win_new = _kv_rows(z, b, seq)
        hp = _merge(hp, z, ga, gb, o_cmp, o_sel, o_win, wa[l], wb[l], wc[l], wo[l], final_gain, final=final)
        for j, v in enumerate((s_new, u[:, seq - POOL_BUF:], win_new[:, seq - min(WINDOW, seq):], cmp_new, sel_new)):
            outs_p[j].append(v)

        z = _inproj(hs, norm_gain[l], w_in_p[l])
        ga, s_new = _retention(z, state_ret[l], n=nb, t=dseq, pos0=past, chunk=dseq, nseq=samp_nseq)
        u = z[:, C_PU:C_PU + POOL_WIDTH].reshape(nb, dseq, POOL_WIDTH)
        ext = jnp.concatenate([state_pool[l], u], axis=1)
        halo = jnp.pad(state_pool[l], ((0, 0), (16 - POOL_BUF, 0), (0, 0)))
        gb = _pool(z, halo, w_pool_b[l], pool_scale[l], n=nb, t=dseq, pos0=past, tt=dseq, nb=samp_nb)
        o_cmp, o_sel, o_win, win_full = _nsa_sample(z, cache_cmp[l], cache_sel[l], cache_win[l], pt, pe2[l], bd_ck[l], bd_cv[l],
                                                     tab, n=nb, t=dseq, past=past)
        cmp_new, sel_new, _ = _kv_rows(z, nb, dseq)
        hs = _merge(hs, z, ga, gb, o_cmp, o_sel, o_win, wa[l], wb[l], wc[l], wo[l], final_gain, final=final)
        win_full = win_full.reshape(nb, win_full.shape[1], 2, NSA_KV_HEADS, NSA_DH)
        for j, v in enumerate((s_new, ext[:, ext.shape[1] - POOL_BUF:], win_full, cmp_new, sel_new)):
            outs_s[j].append(v)

    y_prompt = hp.reshape(b, seq, D_MODEL)
    y_sample = hs.reshape(nb, dseq, D_MODEL)
    st = lambda xs: jnp.stack(xs)
    return (y_prompt, y_sample,
            st(outs_p[0]), st(outs_s[0]), st(outs_p[1]), st(outs_s[1]), st(outs_p[2]), st(outs_s[2]),
            st(outs_p[3]), st(outs_s[3]), st(outs_p[4]), st(outs_s[4]))
```

```python
import functools
import math

import numpy as np
import jax
import jax.numpy as jnp
from jax import lax
from jax.experimental import pallas as pl
from jax.experimental.pallas import tpu as pltpu

F32 = jnp.float32
BF16 = jnp.bfloat16
I32 = jnp.int32

D_MODEL = 1024
PAGE_SIZE = 128
RET_HEADS, RET_DK, RET_DV = 4, 128, 256
RET_QK, RET_V = RET_HEADS * RET_DK, RET_HEADS * RET_DV
ROPE_BASE = 10000.0
POOL_WINDOWS = (2, 4, 8, 16)
POOL_GROUPS, POOL_GDIM = 4, 128
POOL_WIDTH = POOL_GROUPS * POOL_GDIM
POOL_BUF = 15
NSA_HEADS, NSA_KV_HEADS, NSA_DH = 8, 2, 64
NSA_GROUP = NSA_HEADS // NSA_KV_HEADS
NSA_WIDTH, NSA_KV = NSA_HEADS * NSA_DH, NSA_KV_HEADS * NSA_DH
L_CMP, L_SEL, N_SEL, WINDOW = 32, 64, 16, 512
FORCE_SCORE = 1e4
N_BUCKETS, MAX_DIST = 32, 128
EPS = 1e-6
IN_SIZES = (RET_QK, RET_QK, RET_V, RET_V, POOL_WIDTH, POOL_WIDTH, NSA_WIDTH,
            NSA_KV, NSA_KV, NSA_KV, NSA_KV, NSA_KV, NSA_KV, 3 * NSA_HEADS, NSA_WIDTH, 3 * D_MODEL)

LANES = 128
VMEM_LIMIT = 56 * 1024 * 1024

C_MG, C_RV, C_RG, C_RQ, C_RK = 0, 3072, 4096, 5120, 5632
C_PU, C_PG, C_NQ, C_NG, C_KV, C_NBG = 6144, 6656, 7168, 7680, 8192, 8960
NP = 9216
KT = 128
KPAD = 4
CHUNK = 4
EDGE = 5
NEG = -float(2.0 ** 100)
M_INIT = -1e30
LOG2E = 1.4426950408889634


def _cparams(sem):
    return pltpu.CompilerParams(dimension_semantics=sem, vmem_limit_bytes=VMEM_LIMIT)


def _nt(a, b):
    return lax.dot_general(a, b, (((1,), (1,)), ((), ())), preferred_element_type=F32)


def _tn(a, b):
    return lax.dot_general(a, b, (((0,), (0,)), ((), ())), preferred_element_type=F32)


def _dot(a, b):
    return jnp.dot(a, b, preferred_element_type=F32)


def _silu(x):
    return x * jax.nn.sigmoid(x)


def _inproj_kernel(x_ref, g_ref, w_ref, z_ref, xn_ref):
    @pl.when(pl.program_id(1) == 0)
    def _():
        x = x_ref[...]
        ms = jnp.mean(x * x, axis=-1, keepdims=True)
        xn_ref[...] = (x * lax.rsqrt(ms + EPS) * g_ref[...]).astype(BF16)

    z_ref[...] = _dot(xn_ref[...], w_ref[...])


def _inproj(x2, gain, w_p):
    m = x2.shape[0]
    tm = min(1024, m)
    tn = 1024
    return pl.pallas_call(
        _inproj_kernel,
        out_shape=jax.ShapeDtypeStruct((m, NP), F32),
        grid=(m // tm, NP // tn),
        in_specs=[pl.BlockSpec((tm, D_MODEL), lambda i, j: (i, 0)),
                  pl.BlockSpec((1, D_MODEL), lambda i, j: (0, 0)),
                  pl.BlockSpec((D_MODEL, tn), lambda i, j: (0, j))],
        out_specs=pl.BlockSpec((tm, tn), lambda i, j: (i, j)),
        scratch_shapes=[pltpu.VMEM((tm, D_MODEL), BF16)],
        compiler_params=_cparams(("parallel", "arbitrary")),
        name="inproj",
    )(x2, gain.reshape(1, D_MODEL), w_p)


def _pack_w_in(w_in):
    offs = np.cumsum((0,) + IN_SIZES)
    f = [w_in[..., offs[i]:offs[i + 1]] for i in range(len(IN_SIZES))]
    (rq, rk, rv, rg, pu, pg, nq, ck, cv, sk, sv, wk, wv, nbg, ng, mg) = f
    pad = lambda n: jnp.zeros(w_in.shape[:-1] + (n,), w_in.dtype)
    cols = [mg, rv, rg, rq, rk, pu, pg, nq, ng, ck, cv, sk, sv, wk, wv, nbg, pad(LANES - 3 * NSA_HEADS), pad(NP - C_NBG - LANES)]
    return jnp.concatenate(cols, axis=-1).astype(BF16)


def _retention_kernel(q_ref, k_ref, v_ref, g_ref, cos_ref, sin_ref, dm_ref, cross_ref, tail_ref, s0_ref,
                      ga_ref, s_ref, qr_scr, kt_scr, o_scr, *, nseq, chunk, decay):
    rows = nseq * chunk

    @pl.when(pl.program_id(1) == 0)
    def _():
        s_ref[...] = s0_ref[...]

    cos = cos_ref[...]
    sin = sin_ref[...]
    lane = lax.broadcasted_iota(I32, (rows, RET_DK), 1)
    even = (lane % 2) == 0

    def rot(x):
        sw = jnp.where(even, pltpu.roll(x, RET_DK - 1, 1), pltpu.roll(x, 1, 1))
        return x * cos + sw * sin

    for h in range(RET_HEADS):
        qr = rot(q_ref[:, h * RET_DK:(h + 1) * RET_DK])
        kr = rot(k_ref[:, h * RET_DK:(h + 1) * RET_DK]) * (RET_DK ** -0.5)
        att = _nt(qr.astype(BF16), kr.astype(BF16)) * dm_ref[h]
        o_scr[:, h * RET_DV:(h + 1) * RET_DV] = _dot(att.astype(BF16), v_ref[:, h * RET_DV:(h + 1) * RET_DV].astype(BF16))
        qr_scr[h] = qr
        kt_scr[h] = kr * tail_ref[h]

    def per_seq(j, carry):
        r0 = pl.multiple_of(j * chunk, chunk)
        for h in range(RET_HEADS):
            s = s_ref[j, h]
            cols = slice(h * RET_DV, (h + 1) * RET_DV)
            inter = _dot(qr_scr[h, pl.ds(r0, chunk), :].astype(BF16), s.astype(BF16)) * cross_ref[h]
            o_scr[pl.ds(r0, chunk), cols] = o_scr[pl.ds(r0, chunk), cols] + inter
            upd = _tn(kt_scr[h, pl.ds(r0, chunk), :].astype(BF16), v_ref[pl.ds(r0, chunk), cols].astype(BF16))
            s_ref[j, h] = s * decay[h] + upd
        return carry

    lax.fori_loop(0, nseq, per_seq, 0)

    for h in range(RET_HEADS):
        cols = slice(h * RET_DV, (h + 1) * RET_DV)
        o = o_scr[:, cols]
        o = o * lax.rsqrt(jnp.mean(o * o, axis=-1, keepdims=True) + EPS)
        ga_ref[:, cols] = (_silu(g_ref[:, cols]) * o).astype(BF16)


def _retention(z, s0, *, n, t, pos0, chunk, nseq, layer=None, s_buf=None):
    nc = t // chunk
    rows = nseq * chunk
    assert t % chunk == 0 and n % nseq == 0 and (nseq == 1 or nc == 1)
    h = np.arange(RET_HEADS, dtype=np.float64)
    log_g = np.log1p(-np.exp2(-5.0 - h))
    idx = np.arange(chunk, dtype=np.float64)
    rel = idx[:, None] - idx[None, :]
    dmask = np.where(rel[None] >= 0, np.exp(np.maximum(rel, 0.0)[None] * log_g[:, None, None]), 0.0)
    dmask = np.stack([np.kron(np.eye(nseq), dmask[i]) for i in range(RET_HEADS)])
    cross = np.exp((idx + 1.0)[None, :] * log_g[:, None])
    tail = np.exp((chunk - 1.0 - idx)[None, :] * log_g[:, None])
    decay = tuple(float(x) for x in np.exp(chunk * log_g))
    cross_b = np.broadcast_to(cross[:, :, None], (RET_HEADS, chunk, RET_DV))
    tail_b = np.broadcast_to(np.tile(tail, (1, nseq))[:, :, None], (RET_HEADS, rows, RET_DK))
    pos = pos0 + np.arange(t, dtype=np.float64)
    inv = ROPE_BASE ** (-np.arange(0, RET_DK, 2, dtype=np.float64) / RET_DK)
    ang = (pos.astype(np.float32)[:, None] * inv.astype(np.float32)[None, :]).astype(np.float64)
    cos = np.repeat(np.cos(ang), 2, axis=1)
    sgn = np.tile(np.array([-1.0, 1.0]), RET_DK // 2)
    sin = np.repeat(np.sin(ang), 2, axis=1) * sgn[None, :]
    if nseq > 1:
        cos, sin = np.tile(cos, (nseq, 1)), np.tile(sin, (nseq, 1))
    c = lambda a: jnp.asarray(a, F32)
    kern = functools.partial(_retention_kernel, nseq=nseq, chunk=chunk, decay=decay)
    row_blk = lambda i, j: i * nc + j
    if layer is None:
        s_spec = pl.BlockSpec((nseq, RET_HEADS, RET_DK, RET_DV), lambda i, j: (i, 0, 0, 0))
    else:
        s_spec = pl.BlockSpec((None, nseq, RET_HEADS, RET_DK, RET_DV), lambda i, j: (layer, i, 0, 0, 0))
    extra, extra_specs, alias = (), [], {}
    if s_buf is not None:
        extra, extra_specs, alias = (s_buf,), [pl.BlockSpec(memory_space=pl.ANY)], {10: 1}
        kern = lambda *refs, _k=kern: _k(*refs[:10], *refs[11:])
    ga, s_new = pl.pallas_call(
        kern,
        out_shape=(jax.ShapeDtypeStruct((n * t, RET_V), BF16), jax.ShapeDtypeStruct(s0.shape, F32)),
        grid=(n // nseq, nc),
        input_output_aliases=alias,
        in_specs=[pl.BlockSpec((rows, RET_QK), lambda i, j: (row_blk(i, j), C_RQ // RET_QK)),
                  pl.BlockSpec((rows, RET_QK), lambda i, j: (row_blk(i, j), C_RK // RET_QK)),
                  pl.BlockSpec((rows, RET_V), lambda i, j: (row_blk(i, j), C_RV // RET_V)),
                  pl.BlockSpec((rows, RET_V), lambda i, j: (row_blk(i, j), C_RG // RET_V)),
                  pl.BlockSpec((rows, RET_DK), lambda i, j: (j, 0)),
                  pl.BlockSpec((rows, RET_DK), lambda i, j: (j, 0)),
                  pl.BlockSpec((RET_HEADS, rows, rows), lambda i, j: (0, 0, 0)),
                  pl.BlockSpec((RET_HEADS, chunk, RET_DV), lambda i, j: (0, 0, 0)),
                  pl.BlockSpec((RET_HEADS, rows, RET_DK), lambda i, j: (0, 0, 0)),
                  s_spec] + extra_specs,
        out_specs=(pl.BlockSpec((rows, RET_V), lambda i, j: (row_blk(i, j), 0)), s_spec),
        scratch_shapes=[pltpu.VMEM((RET_HEADS, rows, RET_DK), F32),
                        pltpu.VMEM((RET_HEADS, rows, RET_DK), F32),
                        pltpu.VMEM((rows, RET_V), F32)],
        compiler_params=_cparams(("parallel", "arbitrary")),
        name="retention",
    )(z, z, z, z, c(cos), c(sin), c(dmask), c(cross_b), c(tail_b), s0, *extra)
    return ga, s_new


def _pool_kernel(u_ref, pg_ref, halo_ref, w_ref, sc_ref, gb_ref, ext_scr, *, nb, tt, pos0, nbt):
    hal = 16
    ext_scr[:, 0:hal, :] = halo_ref[...]
    ext_scr[:, hal:hal + tt, :] = u_ref[...]
    pos_base = pos0 + (pl.program_id(0) * nb % nbt) * tt
    p1 = lax.broadcasted_iota(I32, (nb, tt, POOL_GDIM), 1) + pos_base + 1
    for g, w in enumerate(POOL_WINDOWS):
        cols = slice(g * POOL_GDIM, (g + 1) * POOL_GDIM)
        acc = ext_scr[:, hal:hal + tt, cols]
        for j in range(1, w):
            acc = acc + ext_scr[:, hal - j:hal - j + tt, cols]
        cnt = jnp.minimum(p1, w).astype(F32)
        d = acc / cnt - u_ref[:, :, cols]
        y = _dot(d.reshape(nb * tt, POOL_GDIM).astype(BF16), w_ref[g])
        y = y * sc_ref[:, cols]
        gate = _silu(pg_ref[:, :, cols]).reshape(nb * tt, POOL_GDIM)
        gb_ref[:, cols] = (gate * y).astype(BF16)


def _pool(z, halo, w_pool_b, scale, *, n, t, pos0, tt, nb):
    nbt = t // tt
    g_tot = n * nbt
    assert g_tot % nb == 0 and (nb == 1 or nbt == 1)
    z3 = z.reshape(g_tot, tt, NP)
    kern = functools.partial(_pool_kernel, nb=nb, tt=tt, pos0=pos0, nbt=nbt)
    return pl.pallas_call(
        kern,
        out_shape=jax.ShapeDtypeStruct((n * t, POOL_WIDTH), BF16),
        grid=(g_tot // nb,),
        in_specs=[pl.BlockSpec((nb, tt, POOL_WIDTH), lambda i: (i, 0, C_PU // POOL_WIDTH)),
                  pl.BlockSpec((nb, tt, POOL_WIDTH), lambda i: (i, 0, C_PG // POOL_WIDTH)),
                  pl.BlockSpec((nb, 16, POOL_WIDTH), lambda i: (i, 0, 0)),
                  pl.BlockSpec((POOL_GROUPS, POOL_GDIM, POOL_GDIM), lambda i: (0, 0, 0)),
                  pl.BlockSpec((1, POOL_WIDTH), lambda i: (0, 0))],
        out_specs=pl.BlockSpec((nb * tt, POOL_WIDTH), lambda i: (i, 0)),
        scratch_shapes=[pltpu.VMEM((nb, 16 + tt, POOL_WIDTH), F32)],
        compiler_params=_cparams(("parallel",)),
        name="pool",
    )(z3, z3, halo, w_pool_b, scale.reshape(1, POOL_WIDTH))


def _t5_bucket(dist):
    n = jnp.maximum(dist, 0)
    exact = N_BUCKETS // 2
    nf = jnp.maximum(n, 1).astype(F32)
    large = exact + (jnp.log(nf / exact) / math.log(MAX_DIST / exact) * (N_BUCKETS - exact)).astype(I32)
    large = jnp.minimum(large, N_BUCKETS - 1)
    return jnp.where(n < exact, n, large)


def _bias_table(rel_bias):
    rb = rel_bias.astype(F32)
    tab = rb[_t5_bucket(jnp.arange(2 * KT, dtype=I32))]
    return tab - rb[N_BUCKETS - 1][None, :]


def _toeplitz_T(tab, delta, rows_q, window=None):
    i = np.arange(rows_q)[None, :]
    j = np.arange(KT)[:, None]
    d = delta + i - j
    valid = d >= 0
    if window is not None:
        valid &= d < window
    dc = np.clip(d, 0, 2 * KT - 1)
    near = (d < 2 * KT) & valid
    vals = tab[jnp.asarray(dc)]
    vals = jnp.where(jnp.asarray(near)[:, :, None], vals, 0.0)
    vals = jnp.where(jnp.asarray(valid)[:, :, None], vals, NEG)
    vals = vals.reshape(KT, rows_q, NSA_KV_HEADS, NSA_GROUP).transpose(2, 0, 3, 1)
    return vals.reshape(NSA_KV_HEADS, KT, NSA_GROUP * rows_q)


def _cmp_rowbias(tab, pos, per_row_head):
    b = np.asarray(pos) % L_CMP
    cols = []
    for m in range(-1, 4):
        if m < 0:
            v = jnp.where(jnp.asarray(b == L_CMP - 1)[:, None], tab[0][None, :], NEG)
        else:
            v = tab[jnp.asarray(L_CMP * m + b + 1)]
        cols.append(v)
    r = jnp.stack(cols, axis=1)
    if per_row_head is None:
        r = r.reshape(r.shape[0], 5 * NSA_HEADS)
        return jnp.pad(r, ((0, 0), (0, LANES - 5 * NSA_HEADS)))
    r = r[jnp.arange(r.shape[0]), :, jnp.asarray(per_row_head)]
    return jnp.pad(r, ((0, 0), (0, LANES - 5)))


def _cmp_bias(dist, col):
    b = jnp.where(dist <= 4 * L_CMP, col(3), 0.0)
    b = jnp.where(dist <= 3 * L_CMP, col(2), b)
    b = jnp.where(dist <= 2 * L_CMP, col(1), b)
    b = jnp.where(dist <= L_CMP, col(0), b)
    return jnp.where(dist <= 0, col(-1), b)


def _select_mask(imp, sblk, qpos, k_top, axis):
    forced = (sblk == 0) | (sblk == qpos // L_SEL)
    avail = sblk * L_SEL <= qpos
    w = jnp.where(forced, FORCE_SCORE, imp)
    w = jnp.where(avail, w, -jnp.inf)
    idx = lax.broadcasted_iota(I32, w.shape, axis).astype(F32)
    sel = jnp.zeros(w.shape, F32)
    for _ in range(k_top):
        mx = jnp.max(w, axis=axis, keepdims=True)
        first = jnp.min(jnp.where(w == mx, idx, float(4 * LANES)), axis=axis, keepdims=True)
        pick = idx == first
        sel = jnp.where(pick, 1.0, sel)
        w = jnp.where(pick, -jnp.inf, w)
    return jnp.where((sel > 0.0) & avail, 0.0, NEG)


def _cmpkv_kernel(kv_ref, pe_ref, wk_ref, wv_ref, kc_ref, vc_ref):
    x = kv_ref[...]
    nblk = x.shape[0] // L_CMP
    cm = x.reshape(nblk, L_CMP, 2 * NSA_KV).sum(axis=1) * (1.0 / L_CMP)
    cm = cm + jnp.mean(pe_ref[...], axis=0, keepdims=True)
    kc_ref[...] = _dot(cm[:, :NSA_KV].astype(BF16), wk_ref[...])
    vc_ref[...] = _dot(cm[:, NSA_KV:].astype(BF16), wv_ref[...])


def _cmpkv(z, pe2, bd_ck, bd_cv, *, n, t):
    nblk = t // L_CMP
    z3 = z.reshape(n, t, NP)
    return pl.pallas_call(
        _cmpkv_kernel,
        out_shape=(jax.ShapeDtypeStruct((n, nblk, NSA_KV), F32), jax.ShapeDtypeStruct((n, nblk, NSA_KV), F32)),
        grid=(n,),
        in_specs=[pl.BlockSpec((None, nblk * L_CMP, 2 * NSA_KV), lambda i: (i, 0, C_KV // (2 * NSA_KV))),
                  pl.BlockSpec((L_CMP, 2 * NSA_KV), lambda i: (0, 0)),
                  pl.BlockSpec((NSA_KV, NSA_KV), lambda i: (0, 0)),
                  pl.BlockSpec((NSA_KV, NSA_KV), lambda i: (0, 0))],
        out_specs=(pl.BlockSpec((None, nblk, NSA_KV), lambda i: (i, 0, 0)),
                   pl.BlockSpec((None, nblk, NSA_KV), lambda i: (i, 0, 0))),
        compiler_params=_cparams(("parallel",)),
        name="cmp_kv",
    )(z3, pe2, bd_ck, bd_cv)


def _cmpsel_kernel(q_ref, kc_ref, vct_ref, rbt_ref, cpos_ref, sblk_ref, oc_ref, mn_ref, *, tq, k_top):
    qs = pl.program_id(1) * tq
    qpos = lax.broadcasted_iota(I32, (1, tq), 1) + qs
    dist = qpos - cpos_ref[...]
    valid = dist >= 0
    rbt = rbt_ref[...]
    kc = kc_ref[...].astype(BF16)
    vct = vct_ref[...].astype(BF16)
    for g in range(NSA_KV_HEADS):
        imp = jnp.zeros((2 * LANES, tq), F32)
        kg = kc[:, g * NSA_DH:(g + 1) * NSA_DH]
        vg = vct[g * NSA_DH:(g + 1) * NSA_DH, :]
        outs = []
        for r in range(NSA_GROUP):
            h = g * NSA_GROUP + r
            q = (q_ref[:, h * NSA_DH:(h + 1) * NSA_DH] * (NSA_DH ** -0.5)).astype(BF16)
            s = _nt(kg, q) + _cmp_bias(dist, lambda m, h=h: rbt[(m + 1) * NSA_HEADS + h:(m + 1) * NSA_HEADS + h + 1, :])
            s = jnp.where(valid, s, M_INIT)
            mx = jnp.max(s, axis=0, keepdims=True)
            e = jnp.where(valid, jnp.exp(s - mx), 0.0)
            p = e / jnp.maximum(jnp.sum(e, axis=0, keepdims=True), 1e-30)
            outs.append(_dot(vg, p.astype(BF16)))
            imp = imp + p
        for k in range(NSA_GROUP // 2):
            pair = jnp.concatenate([outs[2 * k], outs[2 * k + 1]], axis=0).T
            c0 = (g * NSA_GROUP + 2 * k) * NSA_DH
            oc_ref[:, c0:c0 + 2 * NSA_DH] = pair
        imp = imp[:LANES] + imp[LANES:]
        mn_ref[g] = _select_mask(imp, sblk_ref[...], qpos, k_top, 0).astype(BF16)


def _cmpsel(z, kcp, vctp, rowbt, cpos, sblk, *, n, t, k_top):
    tq = KT
    nqt = t // tq
    kern = functools.partial(_cmpsel_kernel, tq=tq, k_top=k_top)
    return pl.pallas_call(
        kern,
        out_shape=(jax.ShapeDtypeStruct((n * t, NSA_WIDTH), F32),
                   jax.ShapeDtypeStruct((n, NSA_KV_HEADS, LANES, t), BF16)),
        grid=(n, nqt),
        in_specs=[pl.BlockSpec((tq, NSA_WIDTH), lambda b, i: (b * nqt + i, C_NQ // NSA_WIDTH)),
                  pl.BlockSpec((None, 2 * LANES, NSA_KV), lambda b, i: (b, 0, 0)),
                  pl.BlockSpec((None, NSA_KV, 2 * LANES), lambda b, i: (b, 0, 0)),
                  pl.BlockSpec((LANES, tq), lambda b, i: (0, 0)),
                  pl.BlockSpec((2 * LANES, tq), lambda b, i: (0, 0)),
                  pl.BlockSpec((LANES, tq), lambda b, i: (0, 0))],
        out_specs=(pl.BlockSpec((tq, NSA_WIDTH), lambda b, i: (b * nqt + i, 0)),
                   pl.BlockSpec((None, NSA_KV_HEADS, LANES, tq), lambda b, i: (b, 0, 0, i))),
        compiler_params=_cparams(("parallel", "parallel")),
        name="cmp_select",
    )(z, kcp, vctp, rowbt, cpos, sblk)


def _kvpack_kernel(z_ref, oh_ref, *refs, nqt):
    ks_ref, vs_ref, vs4_ref, kw_ref, vw_ref, ct_ref, st_ref, wt_ref = refs[-8:]
    i = pl.program_id(1)
    real = (i >= KPAD) & (i < KPAD + nqt)
    kv = z_ref[...]
    kvt = [kv[:, j * NSA_KV:(j + 1) * NSA_KV].T for j in range(6)]
    for o_ref, j in ((ct_ref, 0), (st_ref, 2), (wt_ref, 4)):
        o_ref[0:NSA_KV, :] = kvt[j]
        o_ref[NSA_KV:2 * NSA_KV, :] = kvt[j + 1]

    @pl.when(real)
    def _():
        for k_ref, v_refs, j, lead in ((ks_ref, (vs_ref, vs4_ref), 2, oh_ref[...]),
                                       (kw_ref, (vw_ref,), 4, jnp.zeros((KT, LANES), BF16))):
            vt = kvt[j + 1].astype(BF16)
            for g in range(NSA_KV_HEADS):
                k_ref[g, :, 0:LANES] = lead
                k_ref[g, :, LANES:LANES + NSA_DH] = kv[:, j * NSA_KV + g * NSA_DH:j * NSA_KV + (g + 1) * NSA_DH].astype(BF16)
                for v_ref in v_refs:
                    v_ref[g] = vt[g * NSA_DH:(g + 1) * NSA_DH, :]

    @pl.when(jnp.logical_not(real))
    def _():
        for ref in (ks_ref, vs_ref, vs4_ref, kw_ref, vw_ref):
            ref[...] = jnp.zeros(ref.shape, BF16)


def _kvpack(z, onehot, *, n, t, layer, depth, bufs):
    nqt = t // KT
    ntile = nqt + 2 * KPAD
    assert ntile % CHUNK == 0
    tile = lambda i: jnp.clip(i - KPAD, 0, nqt - 1)
    kshape = jax.ShapeDtypeStruct((n, NSA_KV_HEADS, ntile, KT, LANES + NSA_DH), BF16)
    vshape = jax.ShapeDtypeStruct((n, NSA_KV_HEADS, ntile, NSA_DH, KT), BF16)
    v4shape = jax.ShapeDtypeStruct((n, NSA_KV_HEADS, ntile // CHUNK, NSA_DH, CHUNK * KT), BF16)
    rshape = jax.ShapeDtypeStruct((depth, n, 2 * NSA_KV, t), F32)
    kspec = pl.BlockSpec((None, NSA_KV_HEADS, None, KT, LANES + NSA_DH), lambda b, i: (b, 0, i, 0, 0))
    vspec = pl.BlockSpec((None, NSA_KV_HEADS, None, NSA_DH, KT), lambda b, i: (b, 0, i, 0, 0))
    v4spec = pl.BlockSpec((None, NSA_KV_HEADS, None, NSA_DH, KT), lambda b, i: (b, 0, i // CHUNK, 0, i % CHUNK))
    rspec = pl.BlockSpec((None, None, 2 * NSA_KV, KT), lambda b, i: (layer, b, 0, tile(i)))
    extra, extra_specs, alias = (), [], {}
    if bufs is not None:
        extra, extra_specs, alias = tuple(bufs), [pl.BlockSpec(memory_space=pl.ANY)] * 3, {2: 5, 3: 6, 4: 7}
    return pl.pallas_call(
        functools.partial(_kvpack_kernel, nqt=nqt),
        out_shape=(kshape, vshape, v4shape, kshape, vshape, rshape, rshape, rshape),
        grid=(n, ntile),
        in_specs=[pl.BlockSpec((KT, 1024), lambda b, i: (b * nqt + tile(i), C_KV // 1024)),
                  pl.BlockSpec((KT, LANES), lambda b, i: (tile(i), 0))] + extra_specs,
        out_specs=(kspec, vspec, v4spec, kspec, vspec, rspec, rspec, rspec),
        input_output_aliases=alias,
        compiler_params=_cparams(("parallel", "arbitrary")),
        name="kv_pack",
    )(z, onehot, *extra)


def _qk_chunk(k_ref, t0, ntile, qpt_ref):
    k = k_ref[pl.ds(t0, ntile)].reshape(ntile * KT, LANES + NSA_DH)
    return _dot(k, qpt_ref[...])


def _online_softmax(st, pv, m, l, acc):
    m_new = jnp.maximum(m, jnp.max(st, axis=0, keepdims=True))
    alpha = jnp.exp2(m - m_new)
    p = jnp.exp2(st - m_new)
    l_new = alpha * l + jnp.sum(p, axis=0, keepdims=True)
    return m_new, l_new, alpha * acc + pv(p.astype(BF16))


def _pv_tiles(v_ref, t0, ntile):
    def pv(pb):
        out = _dot(v_ref[t0], pb[0:KT])
        for j in range(1, ntile):
            out = out + _dot(v_ref[t0 + j], pb[j * KT:(j + 1) * KT])
        return out
    return pv


def _attn_store(o_ref, ot):
    for k in range(NSA_GROUP // 2):
        o_ref[:, 2 * k * NSA_DH:(2 * k + 2) * NSA_DH] = jnp.concatenate(
            [ot[:, 2 * k * KT:(2 * k + 1) * KT], ot[:, (2 * k + 1) * KT:(2 * k + 2) * KT]], axis=0).T


def _selwin_kernel(q_ref, mn_ref, ks_ref, vs_ref, vs4_ref, kw_ref, vw_ref, asel_ref, awin_ref, os_ref, ow_ref,
                   qpt_scr, sta_scr, stb_scr, m_scr, l_scr, acc_scr):
    qt = pl.program_id(2)
    mn = mn_ref[...]
    for k in range(NSA_GROUP // 2):
        qt2 = (q_ref[:, 2 * k * NSA_DH:(2 * k + 2) * NSA_DH] * (NSA_DH ** -0.5 * LOG2E)).T.astype(BF16)
        for u in range(2):
            r = 2 * k + u
            qpt_scr[0:LANES, r * KT:(r + 1) * KT] = mn
            qpt_scr[LANES:LANES + NSA_DH, r * KT:(r + 1) * KT] = qt2[u * NSA_DH:(u + 1) * NSA_DH]

    m_scr[...] = jnp.full(m_scr.shape, M_INIT, F32)
    l_scr[...] = jnp.zeros(l_scr.shape, F32)
    acc_scr[...] = jnp.zeros(acc_scr.shape, F32)
    nplain = jnp.maximum(qt - 1, 0) // CHUNK

    @pl.when(nplain > 0)
    def _():
        sta_scr[...] = _qk_chunk(ks_ref, KPAD, CHUNK, qpt_scr)

    def update(st_ref, c):
        m, l, acc = _online_softmax(st_ref[...], lambda pb: _dot(vs4_ref[KPAD // CHUNK + c], pb),
                                    m_scr[...], l_scr[...], acc_scr[...])
        m_scr[...] = m
        l_scr[...] = l
        acc_scr[...] = acc

    def pair(i, carry):
        c0 = 2 * i
        stb_scr[...] = _qk_chunk(ks_ref, KPAD + CHUNK * (c0 + 1), CHUNK, qpt_scr)
        update(sta_scr, c0)
        sta_scr[...] = _qk_chunk(ks_ref, KPAD + CHUNK * jnp.minimum(c0 + 2, nplain - 1), CHUNK, qpt_scr)
        update(stb_scr, c0 + 1)
        return carry

    lax.fori_loop(0, nplain // 2, pair, 0)

    @pl.when(nplain % 2 == 1)
    def _():
        update(sta_scr, nplain - 1)

    e0 = CHUNK * nplain
    off = pl.multiple_of((EDGE - (qt - e0)) * KT, KT)
    st = _qk_chunk(ks_ref, KPAD + e0, EDGE, qpt_scr) + asel_ref[pl.ds(off, EDGE * KT), :]
    _, l, acc = _online_softmax(st, _pv_tiles(vs_ref, KPAD + e0, EDGE), m_scr[...], l_scr[...], acc_scr[...])
    _attn_store(os_ref, acc / l)

    st = _qk_chunk(kw_ref, qt, EDGE, qpt_scr) + awin_ref[...]
    rows = []
    for j in range(EDGE):
        blk = st[j * KT:(j + 1) * KT]
        if j < EDGE - 1:
            blk = blk + jnp.where(qt >= EDGE - 1 - j, 0.0, NEG)
        rows.append(blk)
    st = jnp.concatenate(rows, axis=0)
    zero = jnp.zeros((1, NSA_GROUP * KT), F32)
    _, l, acc = _online_softmax(st, _pv_tiles(vw_ref, qt, EDGE), jnp.full((1, NSA_GROUP * KT), M_INIT, F32), zero,
                                jnp.zeros((NSA_DH, NSA_GROUP * KT), F32))
    _attn_store(ow_ref, acc / l)


def _selwin(z, mnt, ks, vs, vs4, kw, vw, asel, awin, *, n, t):
    nqt = t // KT
    ntile = nqt + 2 * KPAD
    gw = NSA_GROUP * NSA_DH
    gq = NSA_GROUP * KT
    kspec = pl.BlockSpec((None, None, ntile, KT, LANES + NSA_DH), lambda b, g, i: (b, g, 0, 0, 0))
    vspec = pl.BlockSpec((None, None, ntile, NSA_DH, KT), lambda b, g, i: (b, g, 0, 0, 0))
    v4spec = pl.BlockSpec((None, None, ntile // CHUNK, NSA_DH, CHUNK * KT), lambda b, g, i: (b, g, 0, 0, 0))
    ospec = pl.BlockSpec((KT, gw), lambda b, g, i: (b * nqt + i, g))
    return pl.pallas_call(
        _selwin_kernel,
        out_shape=(jax.ShapeDtypeStruct((n * t, NSA_WIDTH), F32), jax.ShapeDtypeStruct((n * t, NSA_WIDTH), F32)),
        grid=(n, NSA_KV_HEADS, nqt),
        in_specs=[pl.BlockSpec((KT, gw), lambda b, g, i: (b * nqt + i, C_NQ // gw + g)),
                  pl.BlockSpec((None, None, LANES, KT), lambda b, g, i: (b, g, 0, i)),
                  kspec, vspec, v4spec, kspec, vspec,
                  pl.BlockSpec((None, 2 * EDGE * KT, gq), lambda b, g, i: (g, 0, 0)),
                  pl.BlockSpec((None, EDGE * KT, gq), lambda b, g, i: (g, 0, 0))],
        out_specs=(ospec, ospec),
        scratch_shapes=[pltpu.VMEM((LANES + NSA_DH, gq), BF16),
                        pltpu.VMEM((CHUNK * KT, gq), F32),
                        pltpu.VMEM((CHUNK * KT, gq), F32),
                        pltpu.VMEM((1, gq), F32),
                        pltpu.VMEM((1, gq), F32),
                        pltpu.VMEM((NSA_DH, gq), F32)],
        compiler_params=_cparams(("parallel", "parallel", "arbitrary")),
        name="sel_win",
    )(z, mnt, ks, vs, vs4, kw, vw, asel, awin)


def _nsa_prompt(z, pe2, bd_ck, bd_cv, tab, *, n, t, layer, depth, row_bufs):
    assert t % KT == 0 and WINDOW == (EDGE - 1) * KT and KT % L_SEL == 0 and EDGE <= KPAD + 1 and EDGE <= CHUNK + 1
    nblk = t // L_CMP
    n_sel = -(-t // L_SEL)
    assert nblk <= 2 * LANES and n_sel <= LANES and nblk % 2 == 0
    kc, vc = _cmpkv(z, pe2, bd_ck, bd_cv, n=n, t=t)
    half = nblk // 2

    def perm(a):
        z0 = jnp.zeros((n, LANES - half, NSA_KV), F32)
        return jnp.concatenate([a[:, 0::2], z0, a[:, 1::2], z0], axis=1)

    slot = np.arange(LANES)
    big = 1 << 20
    cpos = np.concatenate([np.where(slot < half, (2 * slot) * L_CMP + L_CMP - 1, big),
                           np.where(slot < half, (2 * slot + 1) * L_CMP + L_CMP - 1, big)])
    cpos = np.broadcast_to(cpos[:, None], (2 * LANES, KT))
    sblk = np.broadcast_to(np.where(slot < n_sel, slot, big)[:, None], (LANES, KT))
    rowbt = _cmp_rowbias(tab, np.arange(KT), None).T
    o_cmp, mnt = _cmpsel(z, perm(kc), perm(vc).transpose(0, 2, 1), rowbt, jnp.asarray(cpos, I32), jnp.asarray(sblk, I32),
                         n=n, t=t, k_top=min(N_SEL, n_sel))

    onehot = jnp.asarray(np.arange(t)[:, None] // L_SEL == np.arange(LANES)[None, :], BF16)
    ks, vs, vs4, kw, vw, *row_bufs = _kvpack(z, onehot, n=n, t=t, layer=layer, depth=depth, bufs=row_bufs)
    a0 = _toeplitz_T(tab, 0, KT) * LOG2E
    a1 = _toeplitz_T(tab, KT, KT) * LOG2E
    aw = _toeplitz_T(tab, WINDOW, KT, window=WINDOW) * LOG2E
    zeros = jnp.zeros_like(a0)
    asel = jnp.concatenate([zeros] * (EDGE - 1) + [a1, a0] + [jnp.full_like(a0, NEG)] * (EDGE - 1), axis=1)
    awin = jnp.concatenate([aw] + [zeros] * (EDGE - 3) + [a1, a0], axis=1)
    o_sel, o_win = _selwin(z, mnt, ks, vs, vs4, kw, vw, asel, awin, n=n, t=t)
    return o_cmp, o_sel, o_win, row_bufs


def _nsa_sample_kernel(pt_ref, *refs, npg, t, past, wbuf, k_top, aliased):
    del pt_ref
    cpages = refs[0:npg]
    spages = refs[npg:2 * npg]
    rest = refs[2 * npg:]
    (q_ref, kvn_ref, cw_ref, pe_ref, wk_ref, wv_ref, rb_ref, cpos_ref, sblk_ref, oht_ref, asel_ref, awin_ref) = rest[:12]
    (oc_ref, os_ref, ow_ref, wout_ref, kselt, vselt, kwint, vwint, cm_scr, kcv_scr, qw_scr, qp_scr) = rest[12 + aliased:]
    rows = NSA_HEADS * t
    half = NSA_GROUP * t
    kvn = kvn_ref[...]

    @pl.when(pl.program_id(0) == 0)
    def _():
        kselt[0:LANES, :] = oht_ref[...]
        kcv_scr[...] = jnp.zeros(kcv_scr.shape, F32)
        qw_scr[...] = jnp.zeros(qw_scr.shape, F32)

    kvpad = jnp.concatenate([kvn[:, 0:6 * NSA_KV], jnp.zeros((KT - t, 6 * NSA_KV), F32)], axis=0)
    new_t = [kvpad[:, j * NSA_KV:(j + 1) * NSA_KV].T for j in range(2, 6)]

    bpp = PAGE_SIZE // L_CMP
    for p in range(npg):
        kselt[LANES:2 * LANES, p * PAGE_SIZE:(p + 1) * PAGE_SIZE] = spages[p][0:NSA_KV, :].astype(BF16)
        vselt[:, p * PAGE_SIZE:(p + 1) * PAGE_SIZE] = spages[p][NSA_KV:2 * NSA_KV, :].astype(BF16)
        for c in range(2):
            blk = cpages[p][c * NSA_KV:(c + 1) * NSA_KV, :].T
            cm_scr[p * bpp:(p + 1) * bpp, c * NSA_KV:(c + 1) * NSA_KV] = blk.reshape(bpp, L_CMP, NSA_KV).sum(axis=1) * (1.0 / L_CMP)
    kselt[LANES:2 * LANES, past:past + KT] = new_t[0].astype(BF16)
    vselt[:, past:past + KT] = new_t[1].astype(BF16)
    cw = cw_ref[...]
    kwint[:, 0:wbuf] = cw[0:NSA_KV, :].astype(BF16)
    vwint[:, 0:wbuf] = cw[NSA_KV:2 * NSA_KV, :].astype(BF16)
    kwint[:, wbuf:wbuf + KT] = new_t[2].astype(BF16)
    vwint[:, wbuf:wbuf + KT] = new_t[3].astype(BF16)
    keep = wout_ref.shape[1]
    for c in range(2):
        wout_ref[c * NSA_KV:(c + 1) * NSA_KV, :] = jnp.concatenate(
            [cw[c * NSA_KV:(c + 1) * NSA_KV, wbuf + t - keep:wbuf], new_t[2 + c][:, 0:t]], axis=1)

    for h in range(NSA_HEADS):
        g = h // NSA_GROUP
        qh = q_ref[:, h * NSA_DH:(h + 1) * NSA_DH] * (NSA_DH ** -0.5)
        qw_scr[h * t:(h + 1) * t, g * NSA_DH:(g + 1) * NSA_DH] = qh
    qw = qw_scr[...].astype(BF16)

    def pick(x):
        row = lax.broadcasted_iota(I32, (rows, NSA_DH), 0)
        return jnp.where(row < half, x[:, :NSA_DH], x[:, NSA_DH:])

    def store(o_ref, x):
        for h in range(NSA_HEADS):
            o_ref[:, h * NSA_DH:(h + 1) * NSA_DH] = x[h * t:(h + 1) * t, :]

    nblk = npg * bpp
    cm = cm_scr[...] + jnp.mean(pe_ref[...], axis=0, keepdims=True)
    kcv_scr[0:nblk, 0:NSA_KV] = _dot(cm[:, :NSA_KV].astype(BF16), wk_ref[...])
    kcv_scr[0:nblk, NSA_KV:] = _dot(cm[:, NSA_KV:].astype(BF16), wv_ref[...])
    kc = kcv_scr[:, 0:NSA_KV].astype(BF16)
    vc = kcv_scr[:, NSA_KV:].astype(BF16)
    qpos = lax.broadcasted_iota(I32, (rows, 1), 0) % t + past
    dist = qpos - cpos_ref[...]
    valid = dist >= 0
    rb = rb_ref[...]
    s = _nt(qw, kc) + _cmp_bias(dist, lambda m: rb[:, m + 1:m + 2])
    s = jnp.where(valid, s, M_INIT)
    mx = jnp.max(s, axis=-1, keepdims=True)
    e = jnp.where(valid, jnp.exp(s - mx), 0.0)
    p = e / jnp.maximum(jnp.sum(e, axis=-1, keepdims=True), 1e-30)
    store(oc_ref, pick(_dot(p.astype(BF16), vc)))

    imps = []
    for g in range(NSA_KV_HEADS):
        imp = p[g * half:g * half + t]
        for r in range(1, NSA_GROUP):
            imp = imp + p[g * half + r * t:g * half + (r + 1) * t]
        imps.append(imp + pltpu.roll(imp, LANES - 1, 1))
    imps.append(jnp.zeros((LANES - NSA_KV_HEADS * t, LANES), F32))
    imp_t = jnp.concatenate(imps, axis=0).T
    qpos_l = lax.broadcasted_iota(I32, (1, LANES), 1) % t + past
    mn = _select_mask(imp_t, sblk_ref[...], qpos_l, k_top, 0).T.astype(BF16)
    for g in range(NSA_KV_HEADS):
        for r in range(NSA_GROUP):
            qp_scr[g * half + r * t:g * half + (r + 1) * t, 0:LANES] = mn[g * t:(g + 1) * t]
    qp_scr[:, LANES:2 * LANES] = qw

    def attend(qmat, kt_all, vt_all, add):
        sc = _dot(qmat, kt_all) + add
        mm = jnp.max(sc, axis=-1, keepdims=True)
        ee = jnp.exp(sc - mm)
        ll = jnp.sum(ee, axis=-1, keepdims=True)
        return pick(_nt(ee.astype(BF16), vt_all)) / ll

    store(os_ref, attend(qp_scr[...], kselt[...], vselt[...], asel_ref[...]))
    store(ow_ref, attend(qw, kwint[...], vwint[...], awin_ref[...]))


def _nsa_sample(z, cache_cmp_t, cache_sel_t, cache_win_t, page_table, pe2, bd_ck, bd_cv, tab, *, n, t, past, layer, win_buf):
    npg = page_table.shape[1]
    depth = cache_win_t.shape[0]
    assert past == npg * PAGE_SIZE and past % L_SEL == 0 and past % L_CMP == 0 and PAGE_SIZE == KT
    wbuf = cache_win_t.shape[3]
    keep = min(WINDOW, wbuf + t)
    nblk = past // L_CMP
    assert t < L_CMP and nblk <= LANES and wbuf % KT == 0 and keep > t
    length = past + t
    n_sel = -(-length // L_SEL)
    assert 2 * (n_sel - 1) < LANES and NSA_KV_HEADS * t <= LANES
    rows = NSA_HEADS * t
    lsel = past + KT
    lwin = wbuf + KT
    big = 1 << 20
    slot = np.arange(LANES)
    cpos = np.where(slot < nblk, slot * L_CMP + L_CMP - 1, big)[None, :]
    sblk = np.broadcast_to(np.where((slot % 2 == 0) & (slot // 2 < n_sel), slot // 2, big)[:, None], (LANES, LANES))
    head_of_row = np.repeat(np.arange(NSA_HEADS), t)
    tok_of_row = np.tile(np.arange(t), NSA_HEADS)
    rowb = _cmp_rowbias(tab, past + tok_of_row, head_of_row)
    kpos_sel = np.arange(lsel)
    onehot_t = jnp.asarray((kpos_sel[None, :] < length) & (2 * (kpos_sel[None, :] // L_SEL) == slot[:, None]), BF16)

    def additive(kpos, exists, window):
        d = (past + np.arange(t))[:, None] - kpos[None, :]
        valid = (d >= 0) & exists[None, :]
        if window is not None:
            valid &= d < window
        near = valid & (d < 2 * KT)
        j0 = int(np.nonzero(near.any(axis=0))[0].min()) // KT * KT
        vals = tab[jnp.asarray(np.clip(d[:, j0:], 0, 2 * KT - 1))]
        vals = jnp.where(jnp.asarray(near[:, j0:])[:, :, None], vals, 0.0)
        vals = jnp.pad(vals, ((0, 0), (j0, 0), (0, 0)))
        vals = jnp.where(jnp.asarray(valid)[:, :, None], vals, NEG)
        return vals.transpose(2, 0, 1).reshape(NSA_HEADS * t, kpos.shape[0])

    asel = additive(kpos_sel, kpos_sel < length, None)
    kidx = np.arange(lwin)
    awin = additive(past - wbuf + kidx, kidx < wbuf + t, WINDOW)

    page_spec = lambda p: pl.BlockSpec((None, None, 2 * NSA_KV, PAGE_SIZE), lambda i, pt, p=p: (layer, pt[i, p], 0, 0))
    const = lambda shape: pl.BlockSpec(shape, lambda i, pt: tuple(0 for _ in shape))
    ospec = pl.BlockSpec((t, NSA_WIDTH), lambda i, pt: (i, 0))
    aliased = win_buf is not None
    extra, extra_specs, alias = (), [], {}
    if aliased:
        extra, extra_specs, alias = (win_buf,), [pl.BlockSpec(memory_space=pl.ANY)], {1 + 2 * npg + 12: 3}
    kern = functools.partial(_nsa_sample_kernel, npg=npg, t=t, past=past, wbuf=wbuf, k_top=min(N_SEL, n_sel), aliased=int(aliased))
    grid_spec = pltpu.PrefetchScalarGridSpec(
        num_scalar_prefetch=1,
        grid=(n,),
        in_specs=[page_spec(p) for p in range(npg)] + [page_spec(p) for p in range(npg)] + [
            pl.BlockSpec((t, NSA_WIDTH), lambda i, pt: (i, C_NQ // NSA_WIDTH)),
            pl.BlockSpec((t, 1024), lambda i, pt: (i, C_KV // 1024)),
            pl.BlockSpec((None, None, 2 * NSA_KV, wbuf), lambda i, pt: (layer, i, 0, 0)),
            const((L_CMP, 2 * NSA_KV)), const((NSA_KV, NSA_KV)), const((NSA_KV, NSA_KV)),
            const((rows, LANES)), const((1, LANES)), const((LANES, LANES)),
            const((LANES, lsel)), const((rows, lsel)), const((rows, lwin))] + extra_specs,
        out_specs=(ospec, ospec, ospec, pl.BlockSpec((None, None, 2 * NSA_KV, keep), lambda i, pt: (layer, i, 0, 0))),
        scratch_shapes=[pltpu.VMEM((2 * LANES, lsel), BF16), pltpu.VMEM((LANES, lsel), BF16),
                        pltpu.VMEM((LANES, lwin), BF16), pltpu.VMEM((LANES, lwin), BF16),
                        pltpu.VMEM((nblk, 2 * NSA_KV), F32), pltpu.VMEM((LANES, 2 * NSA_KV), F32),
                        pltpu.VMEM((rows, NSA_KV), F32),
                        pltpu.VMEM((rows, 2 * LANES), BF16)])
    o_cmp, o_sel, o_win, win_new = pl.pallas_call(
        kern,
        grid_spec=grid_spec,
        out_shape=(jax.ShapeDtypeStruct((n * t, NSA_WIDTH), F32),) * 3
        + (jax.ShapeDtypeStruct((depth, n, 2 * NSA_KV, keep), F32),),
        input_output_aliases=alias,
        compiler_params=_cparams(("arbitrary",)),
        name="nsa_sample",
    )(page_table, *([cache_cmp_t] * npg), *([cache_sel_t] * npg), z, z, cache_win_t, pe2, bd_ck, bd_cv, rowb,
      jnp.asarray(cpos, I32), jnp.asarray(sblk, I32), onehot_t, asel, awin, *extra)
    return o_cmp, o_sel, o_win, win_new


def _merge_kernel(x_ref, mg_ref, ga_ref, gb_ref, oc_ref, os_ref, ow_ref, nbg_ref, ng_ref,
                  wa_ref, wb_ref, wc_ref, wo_ref, fg_ref, y_ref, *, final):
    tm = x_ref.shape[0]
    gates = jax.nn.sigmoid(nbg_ref[...])
    lane = lax.broadcasted_iota(I32, (tm, LANES), 1)
    low = lane < NSA_DH
    parts = []
    for pi in range(NSA_HEADS // 2):
        cols = slice(pi * LANES, (pi + 1) * LANES)
        acc = jnp.zeros((tm, LANES), F32)
        for b, o_ref in enumerate((oc_ref, os_ref, ow_ref)):
            c0 = b * NSA_HEADS + 2 * pi
            gt = jnp.where(low, gates[:, c0:c0 + 1], gates[:, c0 + 1:c0 + 2])
            acc = acc + gt * o_ref[:, cols]
        parts.append((_silu(ng_ref[:, cols]) * acc).astype(BF16))
    gc = jnp.concatenate(parts, axis=1)
    b_a = _dot(ga_ref[...], wa_ref[...])
    b_b = _dot(gb_ref[...], wb_ref[...])
    b_c = _dot(gc, wc_ref[...])
    mg = mg_ref[...]
    merged = (jax.nn.sigmoid(mg[:, 0:D_MODEL]) * b_a + jax.nn.sigmoid(mg[:, D_MODEL:2 * D_MODEL]) * b_b
              + jax.nn.sigmoid(mg[:, 2 * D_MODEL:]) * b_c)
    y = x_ref[...] + _dot(merged.astype(BF16), wo_ref[...])
    if final:
        y = y * lax.rsqrt(jnp.mean(y * y, axis=-1, keepdims=True) + EPS) * fg_ref[...]
    y_ref[...] = y


def _merge(x2, z, ga, gb, o_cmp, o_sel, o_win, wa, wb, wc, wo, fgain, *, final):
    m = x2.shape[0]
    tm = min(256, m)
    row = lambda w, c=0: pl.BlockSpec((tm, w), lambda i, c=c: (i, c))
    const = lambda a: pl.BlockSpec(a.shape, lambda i: (0, 0))
    fg = fgain.reshape(1, D_MODEL)
    return pl.pallas_call(
        functools.partial(_merge_kernel, final=final),
        out_shape=jax.ShapeDtypeStruct((m, D_MODEL), F32),
        grid=(m // tm,),
        in_specs=[row(D_MODEL), row(3 * D_MODEL, C_MG // (3 * D_MODEL)), row(RET_V), row(POOL_WIDTH),
                  row(NSA_WIDTH), row(NSA_WIDTH), row(NSA_WIDTH),
                  row(LANES, C_NBG // LANES), row(NSA_WIDTH, C_NG // NSA_WIDTH),
                  const(wa), const(wb), const(wc), const(wo), const(fg)],
        out_specs=row(D_MODEL),
        compiler_params=_cparams(("parallel",)),
        name="merge",
    )(x2, z, ga, gb, o_cmp, o_sel, o_win, z, z, wa, wb, wc, wo, fg)


def _kv_rows(z, n, t):
    kv = z[:, C_KV:C_KV + 6 * NSA_KV].reshape(n, t, 3, 2, NSA_KV_HEADS, NSA_DH)
    return kv[:, :, 0], kv[:, :, 1], kv[:, :, 2]


def _slots_minor(cache):
    d, x, slots = cache.shape[:3]
    return cache.transpose(0, 1, 3, 4, 5, 2).reshape(d, x, 2 * NSA_KV, slots)


def _slots_major(rows_t):
    d, n, _, slots = rows_t.shape
    return rows_t.reshape(d, n, 2, NSA_KV_HEADS, NSA_DH, slots).transpose(0, 1, 5, 2, 3, 4)


def kernel(x_prompt, x_sample, state_ret, state_pool, cache_win, cache_cmp, cache_sel, page_table, norm_gain, w_in, w_pool, pool_scale, pe_cmp, w_ck, w_cv, w_br_a, w_br_b, w_br_c, w_out, rel_bias, final_gain):
    b, seq, _ = x_prompt.shape
    nb, dseq, _ = x_sample.shape
    depth = w_in.shape[0]
    past = page_table.shape[1] * PAGE_SIZE
    pt = page_table.astype(I32)

    w_in_p = _pack_w_in(w_in)
    w_pool_b = w_pool.astype(BF16)
    wa, wb, wc, wo = (w.astype(BF16) for w in (w_br_a, w_br_b, w_br_c, w_out))
    eye = jnp.eye(NSA_KV_HEADS, dtype=F32)
    bd_ck = jax.vmap(lambda w: jnp.kron(eye, w))(w_ck).astype(BF16)
    bd_cv = jax.vmap(lambda w: jnp.kron(eye, w))(w_cv).astype(BF16)
    pe2 = pe_cmp.reshape(depth, L_CMP, 2 * NSA_KV)
    tab = _bias_table(rel_bias)
    cmp_t, sel_t, win_t = _slots_minor(cache_cmp), _slots_minor(cache_sel), _slots_minor(cache_win)

    ret_chunk = 256 if seq % 256 == 0 else seq
    pool_tt = min(1024, seq)
    samp_nseq = 8 if nb % 8 == 0 else 1
    samp_nb = 16 if nb % 16 == 0 else 1

    hp = x_prompt.reshape(b * seq, D_MODEL)
    hs = x_sample.reshape(nb * dseq, D_MODEL)
    zero_state = jnp.zeros((b, RET_HEADS, RET_DK, RET_DV), F32)
    ret_p, pool_p, pool_s, cmp_s, sel_s = [], [], [], [], []
    row_bufs = ret_s = win_s = None
    for l in range(depth):
        final = l == depth - 1
        z = _inproj(hp, norm_gain[l], w_in_p[l])
        ga, s_new = _retention(z, zero_state, n=b, t=seq, pos0=0, chunk=ret_chunk, nseq=1)
        u = z[:, C_PU:C_PU + POOL_WIDTH].reshape(b, seq, POOL_WIDTH)
        nbt = seq // pool_tt
        halo = u.reshape(b, nbt, pool_tt, POOL_WIDTH)[:, :, pool_tt - 16:, :]
        halo = jnp.concatenate([jnp.zeros((b, 1, 16, POOL_WIDTH), F32), halo[:, :nbt - 1]], axis=1)
        gb = _pool(z, halo.reshape(b * nbt, 16, POOL_WIDTH), w_pool_b[l], pool_scale[l], n=b, t=seq, pos0=0, tt=pool_tt, nb=1)
        o_cmp, o_sel, o_win, row_bufs = _nsa_prompt(z, pe2[l], bd_ck[l], bd_cv[l], tab, n=b, t=seq, layer=l, depth=depth,
                                                    row_bufs=row_bufs)
        hp = _merge(hp, z, ga, gb, o_cmp, o_sel, o_win, wa[l], wb[l], wc[l], wo[l], final_gain, final=final)
        ret_p.append(s_new)
        pool_p.append(u[:, seq - POOL_BUF:])

        z = _inproj(hs, norm_gain[l], w_in_p[l])
        ga, ret_s = _retention(z, state_ret, n=nb, t=dseq, pos0=past, chunk=dseq, nseq=samp_nseq, layer=l, s_buf=ret_s)
        u = z[:, C_PU:C_PU + POOL_WIDTH].reshape(nb, dseq, POOL_WIDTH)
        ext = jnp.concatenate([state_pool[l], u], axis=1)
        halo = jnp.pad(state_pool[l], ((0, 0), (16 - POOL_BUF, 0), (0, 0)))
        gb = _pool(z, halo, w_pool_b[l], pool_scale[l], n=nb, t=dseq, pos0=past, tt=dseq, nb=samp_nb)
        o_cmp, o_sel, o_win, win_s = _nsa_sample(z, cmp_t, sel_t, win_t, pt, pe2[l], bd_ck[l], bd_cv[l], tab,
                                                 n=nb, t=dseq, past=past, layer=l, win_buf=win_s)
        cmp_new, sel_new, _ = _kv_rows(z, nb, dseq)
        hs = _merge(hs, z, ga, gb, o_cmp, o_sel, o_win, wa[l], wb[l], wc[l], wo[l], final_gain, final=final)
        pool_s.append(ext[:, ext.shape[1] - POOL_BUF:])
        cmp_s.append(cmp_new)
        sel_s.append(sel_new)

    y_prompt = hp.reshape(b, seq, D_MODEL)
    y_sample = hs.reshape(nb, dseq, D_MODEL)
    cmp_p, sel_p, win_p = row_bufs
    keep_p = min(WINDOW, seq)
    st = lambda xs: jnp.stack(xs)
    return (y_prompt, y_sample, st(ret_p), ret_s, st(pool_p), st(pool_s),
            _slots_major(win_p[:, :, :, seq - keep_p:]), _slots_major(win_s),
            _slots_major(cmp_p), st(cmp_s), _slots_major(sel_p), st(sel_s))
```

```python
import functools
import math

import numpy as np
import jax
import jax.numpy as jnp
from jax import lax
from jax.experimental import pallas as pl
from jax.experimental.pallas import tpu as pltpu

F32 = jnp.float32
BF16 = jnp.bfloat16
I32 = jnp.int32

D_MODEL = 1024
PAGE_SIZE = 128
RET_HEADS, RET_DK, RET_DV = 4, 128, 256
RET_QK, RET_V = RET_HEADS * RET_DK, RET_HEADS * RET_DV
ROPE_BASE = 10000.0
POOL_WINDOWS = (2, 4, 8, 16)
POOL_GROUPS, POOL_GDIM = 4, 128
POOL_WIDTH = POOL_GROUPS * POOL_GDIM
POOL_BUF = 15
NSA_HEADS, NSA_KV_HEADS, NSA_DH = 8, 2, 64
NSA_GROUP = NSA_HEADS // NSA_KV_HEADS
NSA_WIDTH, NSA_KV = NSA_HEADS * NSA_DH, NSA_KV_HEADS * NSA_DH
L_CMP, L_SEL, N_SEL, WINDOW = 32, 64, 16, 512
FORCE_SCORE = 1e4
N_BUCKETS, MAX_DIST = 32, 128
EPS = 1e-6
IN_SIZES = (RET_QK, RET_QK, RET_V, RET_V, POOL_WIDTH, POOL_WIDTH, NSA_WIDTH,
            NSA_KV, NSA_KV, NSA_KV, NSA_KV, NSA_KV, NSA_KV, 3 * NSA_HEADS, NSA_WIDTH, 3 * D_MODEL)

LANES = 128
VMEM_LIMIT = 56 * 1024 * 1024

C_MG, C_RV, C_RG, C_RQ, C_RK = 0, 3072, 4096, 5120, 5632
C_PU, C_PG, C_NQ, C_NG, C_KV, C_NBG = 6144, 6656, 7168, 7680, 8192, 8960
NP = 9216
TM_INPROJ, TN_INPROJ = 256, 1024
KT = 128
KPAD = 4
CHUNK = 4
EDGE = 5
NEAR = 16
NEG = -float(2.0 ** 100)
M_INIT = -1e30
LOG2E = 1.4426950408889634


def _cparams(sem):
    return pltpu.CompilerParams(dimension_semantics=sem, vmem_limit_bytes=VMEM_LIMIT)


def _nt(a, b):
    return lax.dot_general(a, b, (((1,), (1,)), ((), ())), preferred_element_type=F32)


def _tn(a, b):
    return lax.dot_general(a, b, (((0,), (0,)), ((), ())), preferred_element_type=F32)


def _dot(a, b):
    return jnp.dot(a, b, preferred_element_type=F32)


def _silu(x):
    return x * jax.nn.sigmoid(x)


def _inproj_kernel(x_ref, g_ref, w_ref, z_ref):
    x = x_ref[...]
    ms = jnp.mean(x * x, axis=-1, keepdims=True)
    xn = (x * lax.rsqrt(ms + EPS) * g_ref[...]).astype(BF16)
    for j in range(NP // TN_INPROJ):
        cols = slice(j * TN_INPROJ, (j + 1) * TN_INPROJ)
        z_ref[:, cols] = _dot(xn, w_ref[:, cols])


def _inproj(x2, gain, w_p):
    m = x2.shape[0]
    tm = min(TM_INPROJ, m)
    return pl.pallas_call(
        _inproj_kernel,
        out_shape=jax.ShapeDtypeStruct((m, NP), F32),
        grid=(m // tm,),
        in_specs=[pl.BlockSpec((tm, D_MODEL), lambda i: (i, 0)),
                  pl.BlockSpec((1, D_MODEL), lambda i: (0, 0)),
                  pl.BlockSpec((D_MODEL, NP), lambda i: (0, 0), pipeline_mode=pl.Buffered(1))],
        out_specs=pl.BlockSpec((tm, NP), lambda i: (i, 0)),
        compiler_params=_cparams(("parallel",)),
        name="inproj",
    )(x2, gain.reshape(1, D_MODEL), w_p)


def _pack_w_in(w_in):
    offs = np.cumsum((0,) + IN_SIZES)
    f = [w_in[..., offs[i]:offs[i + 1]] for i in range(len(IN_SIZES))]
    (rq, rk, rv, rg, pu, pg, nq, ck, cv, sk, sv, wk, wv, nbg, ng, mg) = f
    pad = lambda n: jnp.zeros(w_in.shape[:-1] + (n,), w_in.dtype)
    cols = [mg, rv, rg, rq, rk, pu, pg, nq, ng, ck, cv, sk, sv, wk, wv, nbg, pad(LANES - 3 * NSA_HEADS), pad(NP - C_NBG - LANES)]
    return jnp.concatenate(cols, axis=-1).astype(BF16)


def _retention_kernel(q_ref, k_ref, v_ref, g_ref, cos_ref, sin_ref, dm_ref, cross_ref, tail_ref, s0_ref,
                      ga_ref, s_ref, qr_scr, kt_scr, o_scr, *, nseq, chunk, decay):
    rows = nseq * chunk

    @pl.when(pl.program_id(1) == 0)
    def _():
        s_ref[...] = s0_ref[...]

    cos = cos_ref[...]
    sin = sin_ref[...]
    lane = lax.broadcasted_iota(I32, (rows, RET_DK), 1)
    even = (lane % 2) == 0

    def rot(x):
        sw = jnp.where(even, pltpu.roll(x, RET_DK - 1, 1), pltpu.roll(x, 1, 1))
        return x * cos + sw * sin

    for h in range(RET_HEADS):
        qr = rot(q_ref[:, h * RET_DK:(h + 1) * RET_DK])
        kr = rot(k_ref[:, h * RET_DK:(h + 1) * RET_DK]) * (RET_DK ** -0.5)
        att = _nt(qr.astype(BF16), kr.astype(BF16)) * dm_ref[h]
        o_scr[:, h * RET_DV:(h + 1) * RET_DV] = _dot(att.astype(BF16), v_ref[:, h * RET_DV:(h + 1) * RET_DV].astype(BF16))
        qr_scr[h] = qr
        kt_scr[h] = kr * tail_ref[h]

    def per_seq(j, carry):
        r0 = pl.multiple_of(j * chunk, chunk)
        for h in range(RET_HEADS):
            s = s_ref[j, h]
            cols = slice(h * RET_DV, (h + 1) * RET_DV)
            inter = _dot(qr_scr[h, pl.ds(r0, chunk), :].astype(BF16), s.astype(BF16)) * cross_ref[h]
            o_scr[pl.ds(r0, chunk), cols] = o_scr[pl.ds(r0, chunk), cols] + inter
            upd = _tn(kt_scr[h, pl.ds(r0, chunk), :].astype(BF16), v_ref[pl.ds(r0, chunk), cols].astype(BF16))
            s_ref[j, h] = s * decay[h] + upd
        return carry

    lax.fori_loop(0, nseq, per_seq, 0)

    for h in range(RET_HEADS):
        cols = slice(h * RET_DV, (h + 1) * RET_DV)
        o = o_scr[:, cols]
        o = o * lax.rsqrt(jnp.mean(o * o, axis=-1, keepdims=True) + EPS)
        ga_ref[:, cols] = (_silu(g_ref[:, cols]) * o).astype(BF16)


def _retention(z, s0, *, n, t, pos0, chunk, nseq, layer=None, s_buf=None):
    nc = t // chunk
    rows = nseq * chunk
    assert t % chunk == 0 and n % nseq == 0 and (nseq == 1 or nc == 1)
    h = np.arange(RET_HEADS, dtype=np.float64)
    log_g = np.log1p(-np.exp2(-5.0 - h))
    idx = np.arange(chunk, dtype=np.float64)
    rel = idx[:, None] - idx[None, :]
    dmask = np.where(rel[None] >= 0, np.exp(np.maximum(rel, 0.0)[None] * log_g[:, None, None]), 0.0)
    dmask = np.stack([np.kron(np.eye(nseq), dmask[i]) for i in range(RET_HEADS)])
    cross = np.exp((idx + 1.0)[None, :] * log_g[:, None])
    tail = np.exp((chunk - 1.0 - idx)[None, :] * log_g[:, None])
    decay = tuple(float(x) for x in np.exp(chunk * log_g))
    cross_b = np.broadcast_to(cross[:, :, None], (RET_HEADS, chunk, RET_DV))
    tail_b = np.broadcast_to(np.tile(tail, (1, nseq))[:, :, None], (RET_HEADS, rows, RET_DK))
    pos = pos0 + np.arange(t, dtype=np.float64)
    inv = ROPE_BASE ** (-np.arange(0, RET_DK, 2, dtype=np.float64) / RET_DK)
    ang = pos[:, None] * inv[None, :]
    cos = np.repeat(np.cos(ang), 2, axis=1)
    sgn = np.tile(np.array([-1.0, 1.0]), RET_DK // 2)
    sin = np.repeat(np.sin(ang), 2, axis=1) * sgn[None, :]
    if nseq > 1:
        cos, sin = np.tile(cos, (nseq, 1)), np.tile(sin, (nseq, 1))
    c = lambda a: jnp.asarray(a, F32)
    kern = functools.partial(_retention_kernel, nseq=nseq, chunk=chunk, decay=decay)
    row_blk = lambda i, j: i * nc + j
    if layer is None:
        s_spec = pl.BlockSpec((nseq, RET_HEADS, RET_DK, RET_DV), lambda i, j: (i, 0, 0, 0))
    else:
        s_spec = pl.BlockSpec((None, nseq, RET_HEADS, RET_DK, RET_DV), lambda i, j: (layer, i, 0, 0, 0))
    extra, extra_specs, alias = (), [], {}
    if s_buf is not None:
        extra, extra_specs, alias = (s_buf,), [pl.BlockSpec(memory_space=pl.ANY)], {10: 1}
        kern = lambda *refs, _k=kern: _k(*refs[:10], *refs[11:])
    ga, s_new = pl.pallas_call(
        kern,
        out_shape=(jax.ShapeDtypeStruct((n * t, RET_V), BF16), jax.ShapeDtypeStruct(s0.shape, F32)),
        grid=(n // nseq, nc),
        input_output_aliases=alias,
        in_specs=[pl.BlockSpec((rows, RET_QK), lambda i, j: (row_blk(i, j), C_RQ // RET_QK)),
                  pl.BlockSpec((rows, RET_QK), lambda i, j: (row_blk(i, j), C_RK // RET_QK)),
                  pl.BlockSpec((rows, RET_V), lambda i, j: (row_blk(i, j), C_RV // RET_V)),
                  pl.BlockSpec((rows, RET_V), lambda i, j: (row_blk(i, j), C_RG // RET_V)),
                  pl.BlockSpec((rows, RET_DK), lambda i, j: (j, 0)),
                  pl.BlockSpec((rows, RET_DK), lambda i, j: (j, 0)),
                  pl.BlockSpec((RET_HEADS, rows, rows), lambda i, j: (0, 0, 0)),
                  pl.BlockSpec((RET_HEADS, chunk, RET_DV), lambda i, j: (0, 0, 0)),
                  pl.BlockSpec((RET_HEADS, rows, RET_DK), lambda i, j: (0, 0, 0)),
                  s_spec] + extra_specs,
        out_specs=(pl.BlockSpec((rows, RET_V), lambda i, j: (row_blk(i, j), 0)), s_spec),
        scratch_shapes=[pltpu.VMEM((RET_HEADS, rows, RET_DK), F32),
                        pltpu.VMEM((RET_HEADS, rows, RET_DK), F32),
                        pltpu.VMEM((rows, RET_V), F32)],
        compiler_params=_cparams(("parallel", "arbitrary")),
        name="retention",
    )(z, z, z, z, c(cos), c(sin), c(dmask), c(cross_b), c(tail_b), s0, *extra)
    return ga, s_new


def _pool_kernel(u_ref, pg_ref, halo_ref, w_ref, sc_ref, gb_ref, ext_scr, *, nb, tt, pos0, nbt):
    hal = 16
    ext_scr[:, 0:hal, :] = halo_ref[...]
    ext_scr[:, hal:hal + tt, :] = u_ref[...]
    pos_base = pos0 + (pl.program_id(0) * nb % nbt) * tt
    p1 = lax.broadcasted_iota(I32, (nb, tt, POOL_GDIM), 1) + pos_base + 1
    for g, w in enumerate(POOL_WINDOWS):
        cols = slice(g * POOL_GDIM, (g + 1) * POOL_GDIM)
        acc = ext_scr[:, hal:hal + tt, cols]
        for j in range(1, w):
            acc = acc + ext_scr[:, hal - j:hal - j + tt, cols]
        cnt = jnp.minimum(p1, w).astype(F32)
        d = acc / cnt - u_ref[:, :, cols]
        y = _dot(d.reshape(nb * tt, POOL_GDIM).astype(BF16), w_ref[g])
        y = y * sc_ref[:, cols]
        gate = _silu(pg_ref[:, :, cols]).reshape(nb * tt, POOL_GDIM)
        gb_ref[:, cols] = (gate * y).astype(BF16)


def _pool(z, halo, w_pool_b, scale, *, n, t, pos0, tt, nb):
    nbt = t // tt
    g_tot = n * nbt
    assert g_tot % nb == 0 and (nb == 1 or nbt == 1)
    z3 = z.reshape(g_tot, tt, NP)
    kern = functools.partial(_pool_kernel, nb=nb, tt=tt, pos0=pos0, nbt=nbt)
    return pl.pallas_call(
        kern,
        out_shape=jax.ShapeDtypeStruct((n * t, POOL_WIDTH), BF16),
        grid=(g_tot // nb,),
        in_specs=[pl.BlockSpec((nb, tt, POOL_WIDTH), lambda i: (i, 0, C_PU // POOL_WIDTH)),
                  pl.BlockSpec((nb, tt, POOL_WIDTH), lambda i: (i, 0, C_PG // POOL_WIDTH)),
                  pl.BlockSpec((nb, 16, POOL_WIDTH), lambda i: (i, 0, 0)),
                  pl.BlockSpec((POOL_GROUPS, POOL_GDIM, POOL_GDIM), lambda i: (0, 0, 0)),
                  pl.BlockSpec((1, POOL_WIDTH), lambda i: (0, 0))],
        out_specs=pl.BlockSpec((nb * tt, POOL_WIDTH), lambda i: (i, 0)),
        scratch_shapes=[pltpu.VMEM((nb, 16 + tt, POOL_WIDTH), F32)],
        compiler_params=_cparams(("parallel",)),
        name="pool",
    )(z3, z3, halo, w_pool_b, scale.reshape(1, POOL_WIDTH))


def _t5_bucket(dist):
    n = jnp.maximum(dist, 0)
    exact = N_BUCKETS // 2
    nf = jnp.maximum(n, 1).astype(F32)
    large = exact + (jnp.log(nf / exact) / math.log(MAX_DIST / exact) * (N_BUCKETS - exact)).astype(I32)
    large = jnp.minimum(large, N_BUCKETS - 1)
    return jnp.where(n < exact, n, large)


def _bias_table(rel_bias):
    rb = rel_bias.astype(F32)
    tab = rb[_t5_bucket(jnp.arange(2 * KT, dtype=I32))]
    return tab - rb[N_BUCKETS - 1][None, :]


def _toeplitz_T(tab, delta, rows_q, window=None):
    assert rows_q == KT
    d = delta + np.arange(-(KT - 1), KT)
    valid = d >= 0
    if window is not None:
        valid &= d < window
    near = valid & (d < 2 * KT)
    lo = int(np.clip(d[0], 0, 2 * KT))
    hi = int(np.clip(d[-1] + 1, 0, 2 * KT))
    seg = tab[lo:hi]
    r = jnp.zeros((2 * KT - 1, NSA_HEADS), F32)
    if hi > lo:
        r = lax.dynamic_update_slice(r, seg, (int(np.nonzero(d == lo)[0][0]), 0))
    r = jnp.where(jnp.asarray(near)[:, None], r, 0.0)
    r = jnp.where(jnp.asarray(valid)[:, None], r, NEG)
    rp = jnp.concatenate([r, jnp.zeros((1, NSA_HEADS), F32)], axis=0).T
    skew = jnp.broadcast_to(rp[:, None, :], (NSA_HEADS, KT, 2 * KT)).reshape(NSA_HEADS, KT * 2 * KT)
    skew = skew[:, :KT * (2 * KT - 1)].reshape(NSA_HEADS, KT, 2 * KT - 1)
    vals = skew[:, :, KT - 1:]
    return vals.reshape(NSA_KV_HEADS, NSA_GROUP, KT, rows_q).transpose(0, 2, 1, 3).reshape(NSA_KV_HEADS, KT, NSA_GROUP * rows_q)


def _cmp_rowbias(tab, pos, per_row_head):
    b = np.asarray(pos) % L_CMP
    cols = []
    for m in range(-1, 4):
        if m < 0:
            v = jnp.where(jnp.asarray(b == L_CMP - 1)[:, None], tab[0][None, :], NEG)
        else:
            v = tab[jnp.asarray(L_CMP * m + b + 1)]
        cols.append(v)
    r = jnp.stack(cols, axis=1)
    if per_row_head is None:
        r = r.reshape(r.shape[0], 5 * NSA_HEADS)
        return jnp.pad(r, ((0, 0), (0, LANES - 5 * NSA_HEADS)))
    r = r[jnp.arange(r.shape[0]), :, jnp.asarray(per_row_head)]
    return jnp.pad(r, ((0, 0), (0, LANES - 5)))


def _cmp_bias(dist, col):
    b = jnp.where(dist <= 4 * L_CMP, col(3), 0.0)
    b = jnp.where(dist <= 3 * L_CMP, col(2), b)
    b = jnp.where(dist <= 2 * L_CMP, col(1), b)
    b = jnp.where(dist <= L_CMP, col(0), b)
    return jnp.where(dist <= 0, col(-1), b)


def _select_mask(imp, sblk, qpos, k_top, axis):
    forced = (sblk == 0) | (sblk == qpos // L_SEL)
    avail = sblk * L_SEL <= qpos
    w = jnp.where(forced, FORCE_SCORE, imp)
    w = jnp.where(avail, w, -jnp.inf)
    idx = lax.broadcasted_iota(I32, w.shape, axis).astype(F32)
    for _ in range(k_top):
        mx = jnp.max(w, axis=axis, keepdims=True)
        first = jnp.min(jnp.where(w == mx, idx, float(4 * LANES)), axis=axis, keepdims=True)
        w = jnp.where(idx == first, -jnp.inf, w)
    return jnp.where(avail & (w == -jnp.inf), 0.0, NEG)


def _cmpkv_kernel(kv_ref, pe_ref, wk_ref, wv_ref, kc_ref, vc_ref):
    x = kv_ref[...]
    nblk = x.shape[0] // L_CMP
    cm = x.reshape(nblk, L_CMP, 2 * NSA_KV).sum(axis=1) * (1.0 / L_CMP)
    cm = cm + jnp.mean(pe_ref[...], axis=0, keepdims=True)
    kc_ref[...] = _dot(cm[:, :NSA_KV].astype(BF16), wk_ref[...])
    vc_ref[...] = _dot(cm[:, NSA_KV:].astype(BF16), wv_ref[...])


def _cmpkv(z, pe2, bd_ck, bd_cv, *, n, t):
    nblk = t // L_CMP
    z3 = z.reshape(n, t, NP)
    return pl.pallas_call(
        _cmpkv_kernel,
        out_shape=(jax.ShapeDtypeStruct((n, nblk, NSA_KV), F32), jax.ShapeDtypeStruct((n, nblk, NSA_KV), F32)),
        grid=(n,),
        in_specs=[pl.BlockSpec((None, nblk * L_CMP, 2 * NSA_KV), lambda i: (i, 0, C_KV // (2 * NSA_KV))),
                  pl.BlockSpec((L_CMP, 2 * NSA_KV), lambda i: (0, 0)),
                  pl.BlockSpec((NSA_KV, NSA_KV), lambda i: (0, 0)),
                  pl.BlockSpec((NSA_KV, NSA_KV), lambda i: (0, 0))],
        out_specs=(pl.BlockSpec((None, nblk, NSA_KV), lambda i: (i, 0, 0)),
                   pl.BlockSpec((None, nblk, NSA_KV), lambda i: (i, 0, 0))),
        compiler_params=_cparams(("parallel",)),
        name="cmp_kv",
    )(z3, pe2, bd_ck, bd_cv)


def _attn_store(o_ref, g, ot):
    for k in range(NSA_GROUP // 2):
        c0 = (g * NSA_GROUP + 2 * k) * NSA_DH
        o_ref[:, c0:c0 + 2 * NSA_DH] = jnp.concatenate(
            [ot[:, 2 * k * KT:(2 * k + 1) * KT], ot[:, (2 * k + 1) * KT:(2 * k + 2) * KT]], axis=0).T


def _cmpsel_kernel(q_ref, kc_ref, vct_ref, rbt_ref, cpos_ref, sblk_ref, oc_ref, mn_ref, s_scr, *, tq, k_top):
    gq = NSA_GROUP * tq
    qs = pl.program_id(1) * tq
    qpos = lax.broadcasted_iota(I32, (1, gq), 1) % tq + qs
    valid = qpos - cpos_ref[...] >= 0
    kc = kc_ref[...].astype(BF16)
    vct = vct_ref[...].astype(BF16)
    w0 = jnp.clip((2 * pl.program_id(1) - 2) // 8 * 8, 0, LANES - NEAR)
    imps = []
    for g in range(NSA_KV_HEADS):
        qg = q_ref[:, g * NSA_GROUP * NSA_DH:(g + 1) * NSA_GROUP * NSA_DH] * (NSA_DH ** -0.5)
        qt = jnp.concatenate([qg[:, k * LANES:(k + 1) * LANES].T for k in range(NSA_GROUP // 2)], axis=0)
        qt = jnp.concatenate([qt[r * NSA_DH:(r + 1) * NSA_DH] for r in range(NSA_GROUP)], axis=1).astype(BF16)
        s_scr[...] = _dot(kc[:, g * NSA_DH:(g + 1) * NSA_DH], qt)
        rbt = rbt_ref[g]
        for half in range(2):
            rows_w = pl.ds(pl.multiple_of(half * LANES + w0, 8), NEAR)
            dw = qpos - cpos_ref[rows_w, :]
            s_scr[rows_w, :] = s_scr[rows_w, :] + _cmp_bias(dw, lambda m: rbt[m + 1:m + 2, :])
        s = jnp.where(valid, s_scr[...], M_INIT)
        mx = jnp.max(s, axis=0, keepdims=True)
        e = jnp.where(valid, jnp.exp(s - mx), 0.0)
        p = e / jnp.maximum(jnp.sum(e, axis=0, keepdims=True), 1e-30)
        _attn_store(oc_ref, g, _dot(vct[g * NSA_DH:(g + 1) * NSA_DH, :], p.astype(BF16)))
        imp = p[:, 0:tq]
        for r in range(1, NSA_GROUP):
            imp = imp + p[:, r * tq:(r + 1) * tq]
        imps.append(imp[:LANES] + imp[LANES:])
    mn = _select_mask(jnp.concatenate(imps, axis=1), sblk_ref[...], qpos[:, 0:NSA_KV_HEADS * tq], k_top, 0).astype(BF16)
    for g in range(NSA_KV_HEADS):
        mn_ref[g] = mn[:, g * tq:(g + 1) * tq]


def _cmpsel(z, kcp, vctp, rowbt, cpos, sblk, *, n, t, k_top):
    tq = KT
    assert tq == 4 * L_CMP and 4 * L_CMP == KT
    nqt = t // tq
    gq = NSA_GROUP * tq
    kern = functools.partial(_cmpsel_kernel, tq=tq, k_top=k_top)
    return pl.pallas_call(
        kern,
        out_shape=(jax.ShapeDtypeStruct((n * t, NSA_WIDTH), F32),
                   jax.ShapeDtypeStruct((n, NSA_KV_HEADS, LANES, t), BF16)),
        grid=(n, nqt),
        in_specs=[pl.BlockSpec((tq, NSA_WIDTH), lambda b, i: (b * nqt + i, C_NQ // NSA_WIDTH)),
                  pl.BlockSpec((None, 2 * LANES, NSA_KV), lambda b, i: (b, 0, 0)),
                  pl.BlockSpec((None, NSA_KV, 2 * LANES), lambda b, i: (b, 0, 0)),
                  pl.BlockSpec((NSA_KV_HEADS, 8, gq), lambda b, i: (0, 0, 0)),
                  pl.BlockSpec((2 * LANES, gq), lambda b, i: (0, 0)),
                  pl.BlockSpec((LANES, NSA_KV_HEADS * tq), lambda b, i: (0, 0))],
        out_specs=(pl.BlockSpec((tq, NSA_WIDTH), lambda b, i: (b * nqt + i, 0)),
                   pl.BlockSpec((None, NSA_KV_HEADS, LANES, tq), lambda b, i: (b, 0, 0, i))),
        scratch_shapes=[pltpu.VMEM((2 * LANES, gq), F32)],
        compiler_params=_cparams(("parallel", "parallel")),
        name="cmp_select",
    )(z, kcp, vctp, rowbt, cpos, sblk)


def _kvpack_kernel(z_ref, oh_ref, *refs, nqt):
    ks_ref, vs_ref, vs4_ref, kw_ref, vw_ref, ct_ref, st_ref, wt_ref = refs[-8:]
    i = pl.program_id(1)
    real = (i >= KPAD) & (i < KPAD + nqt)
    kv = z_ref[...]
    kvt = [kv[:, j * NSA_KV:(j + 1) * NSA_KV].T for j in range(6)]
    for o_ref, j in ((ct_ref, 0), (st_ref, 2), (wt_ref, 4)):
        o_ref[0:NSA_KV, :] = kvt[j]
        o_ref[NSA_KV:2 * NSA_KV, :] = kvt[j + 1]

    @pl.when(real)
    def _():
        for k_ref, v_refs, j, lead in ((ks_ref, (vs_ref, vs4_ref), 2, oh_ref[...]),
                                       (kw_ref, (vw_ref,), 4, jnp.zeros((KT, LANES), BF16))):
            vt = kvt[j + 1].astype(BF16)
            for g in range(NSA_KV_HEADS):
                k_ref[g, :, 0:LANES] = lead
                k_ref[g, :, LANES:LANES + NSA_DH] = kv[:, j * NSA_KV + g * NSA_DH:j * NSA_KV + (g + 1) * NSA_DH].astype(BF16)
                for v_ref in v_refs:
                    v_ref[g] = vt[g * NSA_DH:(g + 1) * NSA_DH, :]

    @pl.when(jnp.logical_not(real))
    def _():
        for ref in (ks_ref, vs_ref, vs4_ref, kw_ref, vw_ref):
            ref[...] = jnp.zeros(ref.shape, BF16)


def _kvpack(z, onehot, *, n, t, layer, depth, bufs):
    nqt = t // KT
    ntile = nqt + 2 * KPAD
    assert ntile % CHUNK == 0
    tile = lambda i: jnp.clip(i - KPAD, 0, nqt - 1)
    kshape = jax.ShapeDtypeStruct((n, NSA_KV_HEADS, ntile, KT, LANES + NSA_DH), BF16)
    vshape = jax.ShapeDtypeStruct((n, NSA_KV_HEADS, ntile, NSA_DH, KT), BF16)
    v4shape = jax.ShapeDtypeStruct((n, NSA_KV_HEADS, ntile // CHUNK, NSA_DH, CHUNK * KT), BF16)
    rshape = jax.ShapeDtypeStruct((depth, n, 2 * NSA_KV, t), F32)
    kspec = pl.BlockSpec((None, NSA_KV_HEADS, None, KT, LANES + NSA_DH), lambda b, i: (b, 0, i, 0, 0))
    vspec = pl.BlockSpec((None, NSA_KV_HEADS, None, NSA_DH, KT), lambda b, i: (b, 0, i, 0, 0))
    v4spec = pl.BlockSpec((None, NSA_KV_HEADS, None, NSA_DH, KT), lambda b, i: (b, 0, i // CHUNK, 0, i % CHUNK))
    rspec = pl.BlockSpec((None, None, 2 * NSA_KV, KT), lambda b, i: (layer, b, 0, tile(i)))
    extra, extra_specs, alias = (), [], {}
    if bufs is not None:
        extra, extra_specs, alias = tuple(bufs), [pl.BlockSpec(memory_space=pl.ANY)] * 3, {2: 5, 3: 6, 4: 7}
    return pl.pallas_call(
        functools.partial(_kvpack_kernel, nqt=nqt),
        out_shape=(kshape, vshape, v4shape, kshape, vshape, rshape, rshape, rshape),
        grid=(n, ntile),
        in_specs=[pl.BlockSpec((KT, 1024), lambda b, i: (b * nqt + tile(i), C_KV // 1024)),
                  pl.BlockSpec((KT, LANES), lambda b, i: (tile(i), 0))] + extra_specs,
        out_specs=(kspec, vspec, v4spec, kspec, vspec, rspec, rspec, rspec),
        input_output_aliases=alias,
        compiler_params=_cparams(("parallel", "arbitrary")),
        name="kv_pack",
    )(z, onehot, *extra)


def _qk_chunk(k_ref, t0, ntile, qpt_ref):
    k = k_ref[pl.ds(t0, ntile)].reshape(ntile * KT, LANES + NSA_DH)
    return _dot(k, qpt_ref[...])


def _online_softmax(st, pv, m, l, acc):
    m_new = jnp.maximum(m, jnp.max(st, axis=0, keepdims=True))
    alpha = jnp.exp2(m - m_new)
    p = jnp.exp2(st - m_new)
    l_new = alpha * l + jnp.sum(p, axis=0, keepdims=True)
    return m_new, l_new, alpha * acc + pv(p.astype(BF16))


def _pv_tiles(v_ref, t0, ntile):
    def pv(pb):
        out = _dot(v_ref[t0], pb[0:KT])
        for j in range(1, ntile):
            out = out + _dot(v_ref[t0 + j], pb[j * KT:(j + 1) * KT])
        return out
    return pv


def _selwin_kernel(q_ref, mn_ref, ks_ref, vs_ref, vs4_ref, kw_ref, vw_ref, asel_ref, awin_ref, os_ref, ow_ref,
                   qpt_scr, sta_scr, stb_scr, m_scr, l_scr, acc_scr):
    qt = pl.program_id(2)
    mn = mn_ref[...]
    for k in range(NSA_GROUP // 2):
        qt2 = (q_ref[:, 2 * k * NSA_DH:(2 * k + 2) * NSA_DH] * (NSA_DH ** -0.5 * LOG2E)).T.astype(BF16)
        for u in range(2):
            r = 2 * k + u
            qpt_scr[0:LANES, r * KT:(r + 1) * KT] = mn
            qpt_scr[LANES:LANES + NSA_DH, r * KT:(r + 1) * KT] = qt2[u * NSA_DH:(u + 1) * NSA_DH]

    m_scr[...] = jnp.full(m_scr.shape, M_INIT, F32)
    l_scr[...] = jnp.zeros(l_scr.shape, F32)
    acc_scr[...] = jnp.zeros(acc_scr.shape, F32)
    nplain = jnp.maximum(qt - 1, 0) // CHUNK

    @pl.when(nplain > 0)
    def _():
        sta_scr[...] = _qk_chunk(ks_ref, KPAD, CHUNK, qpt_scr)

    def update(st_ref, c):
        m, l, acc = _online_softmax(st_ref[...], lambda pb: _dot(vs4_ref[KPAD // CHUNK + c], pb),
                                    m_scr[...], l_scr[...], acc_scr[...])
        m_scr[...] = m
        l_scr[...] = l
        acc_scr[...] = acc

    def pair(i, carry):
        c0 = 2 * i
        stb_scr[...] = _qk_chunk(ks_ref, KPAD + CHUNK * (c0 + 1), CHUNK, qpt_scr)
        update(sta_scr, c0)
        sta_scr[...] = _qk_chunk(ks_ref, KPAD + CHUNK * jnp.minimum(c0 + 2, nplain - 1), CHUNK, qpt_scr)
        update(stb_scr, c0 + 1)
        return carry

    lax.fori_loop(0, nplain // 2, pair, 0)

    @pl.when(nplain % 2 == 1)
    def _():
        update(sta_scr, nplain - 1)

    e0 = CHUNK * nplain
    off = pl.multiple_of((EDGE - (qt - e0)) * KT, KT)
    st = _qk_chunk(ks_ref, KPAD + e0, EDGE, qpt_scr) + asel_ref[pl.ds(off, EDGE * KT), :]
    _, l, acc = _online_softmax(st, _pv_tiles(vs_ref, KPAD + e0, EDGE), m_scr[...], l_scr[...], acc_scr[...])
    _attn_store(os_ref, 0, acc / l)

    st = _qk_chunk(kw_ref, qt, EDGE, qpt_scr) + awin_ref[...]
    rows = []
    for j in range(EDGE):
        blk = st[j * KT:(j + 1) * KT]
        if j < EDGE - 1:
            blk = blk + jnp.where(qt >= EDGE - 1 - j, 0.0, NEG)
        rows.append(blk)
    st = jnp.concatenate(rows, axis=0)
    zero = jnp.zeros((1, NSA_GROUP * KT), F32)
    _, l, acc = _online_softmax(st, _pv_tiles(vw_ref, qt, EDGE), jnp.full((1, NSA_GROUP * KT), M_INIT, F32), zero,
                                jnp.zeros((NSA_DH, NSA_GROUP * KT), F32))
    _attn_store(ow_ref, 0, acc / l)


def _selwin(z, mnt, ks, vs, vs4, kw, vw, asel, awin, *, n, t):
    nqt = t // KT
    ntile = nqt + 2 * KPAD
    gw = NSA_GROUP * NSA_DH
    gq = NSA_GROUP * KT
    kspec = pl.BlockSpec((None, None, ntile, KT, LANES + NSA_DH), lambda b, g, i: (b, g, 0, 0, 0))
    vspec = pl.BlockSpec((None, None, ntile, NSA_DH, KT), lambda b, g, i: (b, g, 0, 0, 0))
    v4spec = pl.BlockSpec((None, None, ntile // CHUNK, NSA_DH, CHUNK * KT), lambda b, g, i: (b, g, 0, 0, 0))
    ospec = pl.BlockSpec((KT, gw), lambda b, g, i: (b * nqt + i, g))
    return pl.pallas_call(
        _selwin_kernel,
        out_shape=(jax.ShapeDtypeStruct((n * t, NSA_WIDTH), F32), jax.ShapeDtypeStruct((n * t, NSA_WIDTH), F32)),
        grid=(n, NSA_KV_HEADS, nqt),
        in_specs=[pl.BlockSpec((KT, gw), lambda b, g, i: (b * nqt + i, C_NQ // gw + g)),
                  pl.BlockSpec((None, None, LANES, KT), lambda b, g, i: (b, g, 0, i)),
                  kspec, vspec, v4spec, kspec, vspec,
                  pl.BlockSpec((None, 2 * EDGE * KT, gq), lambda b, g, i: (g, 0, 0)),
                  pl.BlockSpec((None, EDGE * KT, gq), lambda b, g, i: (g, 0, 0))],
        out_specs=(ospec, ospec),
        scratch_shapes=[pltpu.VMEM((LANES + NSA_DH, gq), BF16),
                        pltpu.VMEM((CHUNK * KT, gq), F32),
                        pltpu.VMEM((CHUNK * KT, gq), F32),
                        pltpu.VMEM((1, gq), F32),
                        pltpu.VMEM((1, gq), F32),
                        pltpu.VMEM((NSA_DH, gq), F32)],
        compiler_params=_cparams(("parallel", "parallel", "arbitrary")),
        name="sel_win",
    )(z, mnt, ks, vs, vs4, kw, vw, asel, awin)


def _nsa_prompt(z, pe2, bd_ck, bd_cv, tab, *, n, t, layer, depth, row_bufs):
    assert t % KT == 0 and WINDOW == (EDGE - 1) * KT and KT % L_SEL == 0 and EDGE <= KPAD + 1 and EDGE <= CHUNK + 1
    nblk = t // L_CMP
    n_sel = -(-t // L_SEL)
    assert nblk <= 2 * LANES and n_sel <= LANES and nblk % 2 == 0
    kc, vc = _cmpkv(z, pe2, bd_ck, bd_cv, n=n, t=t)
    half = nblk // 2

    def perm(a):
        z0 = jnp.zeros((n, LANES - half, NSA_KV), F32)
        return jnp.concatenate([a[:, 0::2], z0, a[:, 1::2], z0], axis=1)

    slot = np.arange(LANES)
    big = 1 << 20
    cpos = np.concatenate([np.where(slot < half, (2 * slot) * L_CMP + L_CMP - 1, big),
                           np.where(slot < half, (2 * slot + 1) * L_CMP + L_CMP - 1, big)])
    cpos = np.broadcast_to(cpos[:, None], (2 * LANES, NSA_GROUP * KT))
    sblk = np.broadcast_to(np.where(slot < n_sel, slot, big)[:, None], (LANES, NSA_KV_HEADS * KT))
    rowb = _cmp_rowbias(tab, np.arange(KT), None)[:, :5 * NSA_HEADS].reshape(KT, 5, NSA_KV_HEADS, NSA_GROUP)
    rowbt = jnp.pad(rowb.transpose(2, 1, 3, 0).reshape(NSA_KV_HEADS, 5, NSA_GROUP * KT), ((0, 0), (0, 3), (0, 0)))
    o_cmp, mnt = _cmpsel(z, perm(kc), perm(vc).transpose(0, 2, 1), rowbt, jnp.asarray(cpos, I32), jnp.asarray(sblk, I32),
                         n=n, t=t, k_top=min(N_SEL, n_sel))

    onehot = jnp.asarray(np.arange(t)[:, None] // L_SEL == np.arange(LANES)[None, :], BF16)
    ks, vs, vs4, kw, vw, *row_bufs = _kvpack(z, onehot, n=n, t=t, layer=layer, depth=depth, bufs=row_bufs)
    a0 = _toeplitz_T(tab, 0, KT) * LOG2E
    a1 = _toeplitz_T(tab, KT, KT) * LOG2E
    aw = _toeplitz_T(tab, WINDOW, KT, window=WINDOW) * LOG2E
    zeros = jnp.zeros_like(a0)
    asel = jnp.concatenate([zeros] * (EDGE - 1) + [a1, a0] + [jnp.full_like(a0, NEG)] * (EDGE - 1), axis=1)
    awin = jnp.concatenate([aw] + [zeros] * (EDGE - 3) + [a1, a0], axis=1)
    o_sel, o_win = _selwin(z, mnt, ks, vs, vs4, kw, vw, asel, awin, n=n, t=t)
    return o_cmp, o_sel, o_win, row_bufs


def _nsa_sample_kernel(pt_ref, *refs, npg, t, past, wbuf, k_top, aliased):
    del pt_ref
    cpages = refs[0:npg]
    spages = refs[npg:2 * npg]
    rest = refs[2 * npg:]
    (q_ref, kvn_ref, cw_ref, pe_ref, wk_ref, wv_ref, rb_ref, cpos_ref, sblk_ref, oht_ref, asel_ref, awin_ref) = rest[:12]
    (oc_ref, os_ref, ow_ref, wout_ref, kselt, vselt, kwint, vwint, cm_scr, kcv_scr, qw_scr, qp_scr) = rest[12 + aliased:]
    rows = NSA_HEADS * t
    half = NSA_GROUP * t
    kvn = kvn_ref[...]

    @pl.when(pl.program_id(0) == 0)
    def _():
        kselt[0:LANES, :] = oht_ref[...]
        kcv_scr[...] = jnp.zeros(kcv_scr.shape, F32)
        qw_scr[...] = jnp.zeros(qw_scr.shape, F32)

    kvpad = jnp.concatenate([kvn[:, 0:6 * NSA_KV], jnp.zeros((KT - t, 6 * NSA_KV), F32)], axis=0)
    new_t = [kvpad[:, j * NSA_KV:(j + 1) * NSA_KV].T for j in range(2, 6)]

    bpp = PAGE_SIZE // L_CMP
    for p in range(npg):
        kselt[LANES:2 * LANES, p * PAGE_SIZE:(p + 1) * PAGE_SIZE] = spages[p][0:NSA_KV, :].astype(BF16)
        vselt[:, p * PAGE_SIZE:(p + 1) * PAGE_SIZE] = spages[p][NSA_KV:2 * NSA_KV, :].astype(BF16)
        for c in range(2):
            blk = cpages[p][c * NSA_KV:(c + 1) * NSA_KV, :].T
            cm_scr[p * bpp:(p + 1) * bpp, c * NSA_KV:(c + 1) * NSA_KV] = blk.reshape(bpp, L_CMP, NSA_KV).sum(axis=1) * (1.0 / L_CMP)
    kselt[LANES:2 * LANES, past:past + KT] = new_t[0].astype(BF16)
    vselt[:, past:past + KT] = new_t[1].astype(BF16)
    cw = cw_ref[...]
    kwint[:, 0:wbuf] = cw[0:NSA_KV, :].astype(BF16)
    vwint[:, 0:wbuf] = cw[NSA_KV:2 * NSA_KV, :].astype(BF16)
    kwint[:, wbuf:wbuf + KT] = new_t[2].astype(BF16)
    vwint[:, wbuf:wbuf + KT] = new_t[3].astype(BF16)
    keep = wout_ref.shape[1]
    for c in range(2):
        wout_ref[c * NSA_KV:(c + 1) * NSA_KV, :] = jnp.concatenate(
            [cw[c * NSA_KV:(c + 1) * NSA_KV, wbuf + t - keep:wbuf], new_t[2 + c][:, 0:t]], axis=1)

    for h in range(NSA_HEADS):
        g = h // NSA_GROUP
        qh = q_ref[:, h * NSA_DH:(h + 1) * NSA_DH] * (NSA_DH ** -0.5)
        qw_scr[h * t:(h + 1) * t, g * NSA_DH:(g + 1) * NSA_DH] = qh
    qw = qw_scr[...].astype(BF16)

    def pick(x):
        row = lax.broadcasted_iota(I32, (rows, NSA_DH), 0)
        return jnp.where(row < half, x[:, :NSA_DH], x[:, NSA_DH:])

    def store(o_ref, x):
        for h in range(NSA_HEADS):
            o_ref[:, h * NSA_DH:(h + 1) * NSA_DH] = x[h * t:(h + 1) * t, :]

    nblk = npg * bpp
    cm = cm_scr[...] + jnp.mean(pe_ref[...], axis=0, keepdims=True)
    kcv_scr[0:nblk, 0:NSA_KV] = _dot(cm[:, :NSA_KV].astype(BF16), wk_ref[...])
    kcv_scr[0:nblk, NSA_KV:] = _dot(cm[:, NSA_KV:].astype(BF16), wv_ref[...])
    kc = kcv_scr[:, 0:NSA_KV].astype(BF16)
    vc = kcv_scr[:, NSA_KV:].astype(BF16)
    qpos = lax.broadcasted_iota(I32, (rows, 1), 0) % t + past
    dist = qpos - cpos_ref[...]
    valid = dist >= 0
    rb = rb_ref[...]
    s = _nt(qw, kc) + _cmp_bias(dist, lambda m: rb[:, m + 1:m + 2])
    s = jnp.where(valid, s, M_INIT)
    mx = jnp.max(s, axis=-1, keepdims=True)
    e = jnp.where(valid, jnp.exp(s - mx), 0.0)
    p = e / jnp.maximum(jnp.sum(e, axis=-1, keepdims=True), 1e-30)
    store(oc_ref, pick(_dot(p.astype(BF16), vc)))

    imps = []
    for g in range(NSA_KV_HEADS):
        imp = p[g * half:g * half + t]
        for r in range(1, NSA_GROUP):
            imp = imp + p[g * half + r * t:g * half + (r + 1) * t]
        imps.append(imp + pltpu.roll(imp, LANES - 1, 1))
    imps.append(jnp.zeros((LANES - NSA_KV_HEADS * t, LANES), F32))
    imp_t = jnp.concatenate(imps, axis=0).T
    qpos_l = lax.broadcasted_iota(I32, (1, LANES), 1) % t + past
    mn = _select_mask(imp_t, sblk_ref[...], qpos_l, k_top, 0).T.astype(BF16)
    for g in range(NSA_KV_HEADS):
        for r in range(NSA_GROUP):
            qp_scr[g * half + r * t:g * half + (r + 1) * t, 0:LANES] = mn[g * t:(g + 1) * t]
    qp_scr[:, LANES:2 * LANES] = qw

    def attend(qmat, kt_all, vt_all, add):
        sc = _dot(qmat, kt_all) + add
        mm = jnp.max(sc, axis=-1, keepdims=True)
        ee = jnp.exp(sc - mm)
        ll = jnp.sum(ee, axis=-1, keepdims=True)
        return pick(_nt(ee.astype(BF16), vt_all)) / ll

    store(os_ref, attend(qp_scr[...], kselt[...], vselt[...], asel_ref[...]))
    store(ow_ref, attend(qw, kwint[...], vwint[...], awin_ref[...]))


def _nsa_sample(z, cache_cmp_t, cache_sel_t, cache_win_t, page_table, pe2, bd_ck, bd_cv, tab, *, n, t, past, layer, win_buf):
    npg = page_table.shape[1]
    depth = cache_win_t.shape[0]
    assert past == npg * PAGE_SIZE and past % L_SEL == 0 and past % L_CMP == 0 and PAGE_SIZE == KT
    wbuf = cache_win_t.shape[3]
    keep = min(WINDOW, wbuf + t)
    nblk = past // L_CMP
    assert t < L_CMP and nblk <= LANES and wbuf % KT == 0 and keep > t
    length = past + t
    n_sel = -(-length // L_SEL)
    assert 2 * (n_sel - 1) < LANES and NSA_KV_HEADS * t <= LANES
    rows = NSA_HEADS * t
    lsel = past + KT
    lwin = wbuf + KT
    big = 1 << 20
    slot = np.arange(LANES)
    cpos = np.where(slot < nblk, slot * L_CMP + L_CMP - 1, big)[None, :]
    sblk = np.broadcast_to(np.where((slot % 2 == 0) & (slot // 2 < n_sel), slot // 2, big)[:, None], (LANES, LANES))
    head_of_row = np.repeat(np.arange(NSA_HEADS), t)
    tok_of_row = np.tile(np.arange(t), NSA_HEADS)
    rowb = _cmp_rowbias(tab, past + tok_of_row, head_of_row)
    kpos_sel = np.arange(lsel)
    onehot_t = jnp.asarray((kpos_sel[None, :] < length) & (2 * (kpos_sel[None, :] // L_SEL) == slot[:, None]), BF16)

    def additive(kpos, exists, window):
        d = (past + np.arange(t))[:, None] - kpos[None, :]
        valid = (d >= 0) & exists[None, :]
        if window is not None:
            valid &= d < window
        near = valid & (d < 2 * KT)
        j0 = int(np.nonzero(near.any(axis=0))[0].min()) // KT * KT
        vals = tab[jnp.asarray(np.clip(d[:, j0:], 0, 2 * KT - 1))]
        vals = jnp.where(jnp.asarray(near[:, j0:])[:, :, None], vals, 0.0)
        vals = jnp.pad(vals, ((0, 0), (j0, 0), (0, 0)))
        vals = jnp.where(jnp.asarray(valid)[:, :, None], vals, NEG)
        return vals.transpose(2, 0, 1).reshape(NSA_HEADS * t, kpos.shape[0])

    asel = additive(kpos_sel, kpos_sel < length, None)
    kidx = np.arange(lwin)
    awin = additive(past - wbuf + kidx, kidx < wbuf + t, WINDOW)

    page_spec = lambda p: pl.BlockSpec((None, None, 2 * NSA_KV, PAGE_SIZE), lambda i, pt, p=p: (layer, pt[i, p], 0, 0))
    const = lambda shape: pl.BlockSpec(shape, lambda i, pt: tuple(0 for _ in shape))
    ospec = pl.BlockSpec((t, NSA_WIDTH), lambda i, pt: (i, 0))
    aliased = win_buf is not None
    extra, extra_specs, alias = (), [], {}
    if aliased:
        extra, extra_specs, alias = (win_buf,), [pl.BlockSpec(memory_space=pl.ANY)], {1 + 2 * npg + 12: 3}
    kern = functools.partial(_nsa_sample_kernel, npg=npg, t=t, past=past, wbuf=wbuf, k_top=min(N_SEL, n_sel), aliased=int(aliased))
    grid_spec = pltpu.PrefetchScalarGridSpec(
        num_scalar_prefetch=1,
        grid=(n,),
        in_specs=[page_spec(p) for p in range(npg)] + [page_spec(p) for p in range(npg)] + [
            pl.BlockSpec((t, NSA_WIDTH), lambda i, pt: (i, C_NQ // NSA_WIDTH)),
            pl.BlockSpec((t, 1024), lambda i, pt: (i, C_KV // 1024)),
            pl.BlockSpec((None, None, 2 * NSA_KV, wbuf), lambda i, pt: (layer, i, 0, 0)),
            const((L_CMP, 2 * NSA_KV)), const((NSA_KV, NSA_KV)), const((NSA_KV, NSA_KV)),
            const((rows, LANES)), const((1, LANES)), const((LANES, LANES)),
            const((LANES, lsel)), const((rows, lsel)), const((rows, lwin))] + extra_specs,
        out_specs=(ospec, ospec, ospec, pl.BlockSpec((None, None, 2 * NSA_KV, keep), lambda i, pt: (layer, i, 0, 0))),
        scratch_shapes=[pltpu.VMEM((2 * LANES, lsel), BF16), pltpu.VMEM((LANES, lsel), BF16),
                        pltpu.VMEM((LANES, lwin), BF16), pltpu.VMEM((LANES, lwin), BF16),
                        pltpu.VMEM((nblk, 2 * NSA_KV), F32), pltpu.VMEM((LANES, 2 * NSA_KV), F32),
                        pltpu.VMEM((rows, NSA_KV), F32),
                        pltpu.VMEM((rows, 2 * LANES), BF16)])
    o_cmp, o_sel, o_win, win_new = pl.pallas_call(
        kern,
        grid_spec=grid_spec,
        out_shape=(jax.ShapeDtypeStruct((n * t, NSA_WIDTH), F32),) * 3
        + (jax.ShapeDtypeStruct((depth, n, 2 * NSA_KV, keep), F32),),
        input_output_aliases=alias,
        compiler_params=_cparams(("arbitrary",)),
        name="nsa_sample",
    )(page_table, *([cache_cmp_t] * npg), *([cache_sel_t] * npg), z, z, cache_win_t, pe2, bd_ck, bd_cv, rowb,
      jnp.asarray(cpos, I32), jnp.asarray(sblk, I32), onehot_t, asel, awin, *extra)
    return o_cmp, o_sel, o_win, win_new


def _merge_kernel(x_ref, mg_ref, ga_ref, gb_ref, oc_ref, os_ref, ow_ref, nbg_ref, ng_ref,
                  wa_ref, wb_ref, wc_ref, wo_ref, fg_ref, y_ref, *, final):
    tm = x_ref.shape[0]
    gates = jax.nn.sigmoid(nbg_ref[...])
    lane = lax.broadcasted_iota(I32, (tm, LANES), 1)
    low = lane < NSA_DH
    parts = []
    for pi in range(NSA_HEADS // 2):
        cols = slice(pi * LANES, (pi + 1) * LANES)
        acc = jnp.zeros((tm, LANES), F32)
        for b, o_ref in enumerate((oc_ref, os_ref, ow_ref)):
            c0 = b * NSA_HEADS + 2 * pi
            gt = jnp.where(low, gates[:, c0:c0 + 1], gates[:, c0 + 1:c0 + 2])
            acc = acc + gt * o_ref[:, cols]
        parts.append((_silu(ng_ref[:, cols]) * acc).astype(BF16))
    gc = jnp.concatenate(parts, axis=1)
    b_a = _dot(ga_ref[...], wa_ref[...])
    b_b = _dot(gb_ref[...], wb_ref[...])
    b_c = _dot(gc, wc_ref[...])
    mg = mg_ref[...]
    merged = (jax.nn.sigmoid(mg[:, 0:D_MODEL]) * b_a + jax.nn.sigmoid(mg[:, D_MODEL:2 * D_MODEL]) * b_b
              + jax.nn.sigmoid(mg[:, 2 * D_MODEL:]) * b_c)
    y = x_ref[...] + _dot(merged.astype(BF16), wo_ref[...])
    if final:
        y = y * lax.rsqrt(jnp.mean(y * y, axis=-1, keepdims=True) + EPS) * fg_ref[...]
    y_ref[...] = y


def _merge(x2, z, ga, gb, o_cmp, o_sel, o_win, wa, wb, wc, wo, fgain, *, final):
    m = x2.shape[0]
    tm = min(512, m)
    row = lambda w, c=0: pl.BlockSpec((tm, w), lambda i, c=c: (i, c))
    const = lambda a: pl.BlockSpec(a.shape, lambda i: (0, 0), pipeline_mode=pl.Buffered(1))
    fg = fgain.reshape(1, D_MODEL)
    return pl.pallas_call(
        functools.partial(_merge_kernel, final=final),
        out_shape=jax.ShapeDtypeStruct((m, D_MODEL), F32),
        grid=(m // tm,),
        in_specs=[row(D_MODEL), row(3 * D_MODEL, C_MG // (3 * D_MODEL)), row(RET_V), row(POOL_WIDTH),
                  row(NSA_WIDTH), row(NSA_WIDTH), row(NSA_WIDTH),
                  row(LANES, C_NBG // LANES), row(NSA_WIDTH, C_NG // NSA_WIDTH),
                  const(wa), const(wb), const(wc), const(wo), const(fg)],
        out_specs=row(D_MODEL),
        compiler_params=_cparams(("parallel",)),
        name="merge",
    )(x2, z, ga, gb, o_cmp, o_sel, o_win, z, z, wa, wb, wc, wo, fg)


def _kv_rows(z, n, t):
    kv = z[:, C_KV:C_KV + 6 * NSA_KV].reshape(n, t, 3, 2, NSA_KV_HEADS, NSA_DH)
    return kv[:, :, 0], kv[:, :, 1], kv[:, :, 2]


def _slots_minor(cache):
    d, x, slots = cache.shape[:3]
    return cache.transpose(0, 1, 3, 4, 5, 2).reshape(d, x, 2 * NSA_KV, slots)


def _slots_major(rows_t):
    d, n, _, slots = rows_t.shape
    return rows_t.reshape(d, n, 2, NSA_KV_HEADS, NSA_DH, slots).transpose(0, 1, 5, 2, 3, 4)


def kernel(x_prompt, x_sample, state_ret, state_pool, cache_win, cache_cmp, cache_sel, page_table, norm_gain, w_in, w_pool, pool_scale, pe_cmp, w_ck, w_cv, w_br_a, w_br_b, w_br_c, w_out, rel_bias, final_gain):
    b, seq, _ = x_prompt.shape
    nb, dseq, _ = x_sample.shape
    depth = w_in.shape[0]
    past = page_table.shape[1] * PAGE_SIZE
    pt = page_table.astype(I32)

    w_in_p = _pack_w_in(w_in)
    w_pool_b = w_pool.astype(BF16)
    wa, wb, wc, wo = (w.astype(BF16) for w in (w_br_a, w_br_b, w_br_c, w_out))
    eye = jnp.eye(NSA_KV_HEADS, dtype=F32)
    bd_ck = jax.vmap(lambda w: jnp.kron(eye, w))(w_ck).astype(BF16)
    bd_cv = jax.vmap(lambda w: jnp.kron(eye, w))(w_cv).astype(BF16)
    pe2 = pe_cmp.reshape(depth, L_CMP, 2 * NSA_KV)
    tab = _bias_table(rel_bias)
    cmp_t, sel_t, win_t = _slots_minor(cache_cmp), _slots_minor(cache_sel), _slots_minor(cache_win)

    ret_chunk = 256 if seq % 256 == 0 else seq
    pool_tt = min(1024, seq)
    samp_nseq = 8 if nb % 8 == 0 else 1
    samp_nb = 16 if nb % 16 == 0 else 1

    hp = x_prompt.reshape(b * seq, D_MODEL)
    hs = x_sample.reshape(nb * dseq, D_MODEL)
    zero_state = jnp.zeros((b, RET_HEADS, RET_DK, RET_DV), F32)
    ret_p, pool_p, pool_s, cmp_s, sel_s = [], [], [], [], []
    row_bufs = ret_s = win_s = None
    for l in range(depth):
        final = l == depth - 1
        z = _inproj(hp, norm_gain[l], w_in_p[l])
        ga, s_new = _retention(z, zero_state, n=b, t=seq, pos0=0, chunk=ret_chunk, nseq=1)
        u = z[:, C_PU:C_PU + POOL_WIDTH].reshape(b, seq, POOL_WIDTH)
        nbt = seq // pool_tt
        halo = u.reshape(b, nbt, pool_tt, POOL_WIDTH)[:, :, pool_tt - 16:, :]
        halo = jnp.concatenate([jnp.zeros((b, 1, 16, POOL_WIDTH), F32), halo[:, :nbt - 1]], axis=1)
        gb = _pool(z, halo.reshape(b * nbt, 16, POOL_WIDTH), w_pool_b[l], pool_scale[l], n=b, t=seq, pos0=0, tt=pool_tt, nb=1)
        o_cmp, o_sel, o_win, row_bufs = _nsa_prompt(z, pe2[l], bd_ck[l], bd_cv[l], tab, n=b, t=seq, layer=l, depth=depth,
                                                    row_bufs=row_bufs)
        hp = _merge(hp, z, ga, gb, o_cmp, o_sel, o_win, wa[l], wb[l], wc[l], wo[l], final_gain, final=final)
        ret_p.append(s_new)
        pool_p.append(u[:, seq - POOL_BUF:])

        z = _inproj(hs, norm_gain[l], w_in_p[l])
        ga, ret_s = _retention(z, state_ret, n=nb, t=dseq, pos0=past, chunk=dseq, nseq=samp_nseq, layer=l, s_buf=ret_s)
        u = z[:, C_PU:C_PU + POOL_WIDTH].reshape(nb, dseq, POOL_WIDTH)
        ext = jnp.concatenate([state_pool[l], u], axis=1)
        halo = jnp.pad(state_pool[l], ((0, 0), (16 - POOL_BUF, 0), (0, 0)))
        gb = _pool(z, halo, w_pool_b[l], pool_scale[l], n=nb, t=dseq, pos0=past, tt=dseq, nb=samp_nb)
        o_cmp, o_sel, o_win, win_s = _nsa_sample(z, cmp_t, sel_t, win_t, pt, pe2[l], bd_ck[l], bd_cv[l], tab,
                                                 n=nb, t=dseq, past=past, layer=l, win_buf=win_s)
        cmp_new, sel_new, _ = _kv_rows(z, nb, dseq)
        hs = _merge(hs, z, ga, gb, o_cmp, o_sel, o_win, wa[l], wb[l], wc[l], wo[l], final_gain, final=final)
        pool_s.append(ext[:, ext.shape[1] - POOL_BUF:])
        cmp_s.append(cmp_new)
        sel_s.append(sel_new)

    y_prompt = hp.reshape(b, seq, D_MODEL)
    y_sample = hs.reshape(nb, dseq, D_MODEL)
    cmp_p, sel_p, win_p = row_bufs
    keep_p = min(WINDOW, seq)
    st = lambda xs: jnp.stack(xs)
    return (y_prompt, y_sample, st(ret_p), ret_s, st(pool_p), st(pool_s),
            _slots_major(win_p[:, :, :, seq - keep_p:]), _slots_major(win_s),
            _slots_major(cmp_p), st(cmp_s), _slots_major(sel_p), st(sel_s))
```

```python
import functools
import math

import numpy as np
import jax
import jax.numpy as jnp
from jax import lax
from jax.experimental import pallas as pl
from jax.experimental.pallas import tpu as pltpu

F32 = jnp.float32
BF16 = jnp.bfloat16
I32 = jnp.int32

D_MODEL = 1024
PAGE_SIZE = 128
RET_HEADS, RET_DK, RET_DV = 4, 128, 256
RET_QK, RET_V = RET_HEADS * RET_DK, RET_HEADS * RET_DV
ROPE_BASE = 10000.0
POOL_WINDOWS = (2, 4, 8, 16)
POOL_GROUPS, POOL_GDIM = 4, 128
POOL_WIDTH = POOL_GROUPS * POOL_GDIM
POOL_BUF = 15
NSA_HEADS, NSA_KV_HEADS, NSA_DH = 8, 2, 64
NSA_GROUP = NSA_HEADS // NSA_KV_HEADS
NSA_WIDTH, NSA_KV = NSA_HEADS * NSA_DH, NSA_KV_HEADS * NSA_DH
L_CMP, L_SEL, N_SEL, WINDOW = 32, 64, 16, 512
FORCE_SCORE = 1e4
N_BUCKETS, MAX_DIST = 32, 128
EPS = 1e-6
IN_SIZES = (RET_QK, RET_QK, RET_V, RET_V, POOL_WIDTH, POOL_WIDTH, NSA_WIDTH,
            NSA_KV, NSA_KV, NSA_KV, NSA_KV, NSA_KV, NSA_KV, 3 * NSA_HEADS, NSA_WIDTH, 3 * D_MODEL)

LANES = 128
VMEM_LIMIT = 56 * 1024 * 1024

C_MG, C_RV, C_RG, C_RQ, C_RK = 0, 3072, 4096, 5120, 5632
C_PU, C_PG, C_NQ, C_NG, C_KV, C_NBG = 6144, 6656, 7168, 7680, 8192, 8960
NP = 9216
TM_INPROJ, TN_INPROJ = 256, 1024
KT = 128
KPAD = 4
CHUNK = 4
EDGE = 5
NEAR = 16
NEG = -float(2.0 ** 100)
M_INIT = -1e30
LOG2E = 1.4426950408889634


def _cparams(sem):
    return pltpu.CompilerParams(dimension_semantics=sem, vmem_limit_bytes=VMEM_LIMIT)


def _nt(a, b):
    return lax.dot_general(a, b, (((1,), (1,)), ((), ())), preferred_element_type=F32)


def _tn(a, b):
    return lax.dot_general(a, b, (((0,), (0,)), ((), ())), preferred_element_type=F32)


def _dot(a, b):
    return jnp.dot(a, b, preferred_element_type=F32)


def _silu(x):
    return x * jax.nn.sigmoid(x)


def _inproj_kernel(x_ref, g_ref, w_ref, z_ref):
    x = x_ref[...]
    ms = jnp.mean(x * x, axis=-1, keepdims=True)
    xn = (x * lax.rsqrt(ms + EPS) * g_ref[...]).astype(BF16)
    for j in range(NP // TN_INPROJ):
        cols = slice(j * TN_INPROJ, (j + 1) * TN_INPROJ)
        z_ref[:, cols] = _dot(xn, w_ref[:, cols])


def _inproj(x2, gain, w_p):
    m = x2.shape[0]
    tm = min(TM_INPROJ, m)
    return pl.pallas_call(
        _inproj_kernel,
        out_shape=jax.ShapeDtypeStruct((m, NP), F32),
        grid=(m // tm,),
        in_specs=[pl.BlockSpec((tm, D_MODEL), lambda i: (i, 0)),
                  pl.BlockSpec((1, D_MODEL), lambda i: (0, 0)),
                  pl.BlockSpec((D_MODEL, NP), lambda i: (0, 0), pipeline_mode=pl.Buffered(1))],
        out_specs=pl.BlockSpec((tm, NP), lambda i: (i, 0)),
        compiler_params=_cparams(("parallel",)),
        name="inproj",
    )(x2, gain.reshape(1, D_MODEL), w_p)


def _pack_w_in(w_in):
    offs = np.cumsum((0,) + IN_SIZES)
    f = [w_in[..., offs[i]:offs[i + 1]] for i in range(len(IN_SIZES))]
    (rq, rk, rv, rg, pu, pg, nq, ck, cv, sk, sv, wk, wv, nbg, ng, mg) = f
    pad = lambda n: jnp.zeros(w_in.shape[:-1] + (n,), w_in.dtype)
    cols = [mg, rv, rg, rq, rk, pu, pg, nq, ng, ck, cv, sk, sv, wk, wv, nbg, pad(LANES - 3 * NSA_HEADS), pad(NP - C_NBG - LANES)]
    return jnp.concatenate(cols, axis=-1).astype(BF16)


def _retention_kernel(q_ref, k_ref, v_ref, g_ref, cos_ref, sin_ref, dm_ref, cross_ref, tail_ref, s0_ref,
                      ga_ref, s_ref, qr_scr, kt_scr, o_scr, *, nseq, chunk, decay):
    rows = nseq * chunk

    @pl.when(pl.program_id(1) == 0)
    def _():
        s_ref[...] = s0_ref[...]

    cos = cos_ref[...]
    sin = sin_ref[...]
    lane = lax.broadcasted_iota(I32, (rows, RET_DK), 1)
    even = (lane % 2) == 0

    def rot(x):
        sw = jnp.where(even, pltpu.roll(x, RET_DK - 1, 1), pltpu.roll(x, 1, 1))
        return x * cos + sw * sin

    for h in range(RET_HEADS):
        qr = rot(q_ref[:, h * RET_DK:(h + 1) * RET_DK])
        kr = rot(k_ref[:, h * RET_DK:(h + 1) * RET_DK]) * (RET_DK ** -0.5)
        att = _nt(qr.astype(BF16), kr.astype(BF16)) * dm_ref[h]
        o_scr[:, h * RET_DV:(h + 1) * RET_DV] = _dot(att.astype(BF16), v_ref[:, h * RET_DV:(h + 1) * RET_DV].astype(BF16))
        qr_scr[h] = qr
        kt_scr[h] = kr * tail_ref[h]

    def per_seq(j, carry):
        r0 = pl.multiple_of(j * chunk, chunk)
        for h in range(RET_HEADS):
            s = s_ref[j, h]
            cols = slice(h * RET_DV, (h + 1) * RET_DV)
            inter = _dot(qr_scr[h, pl.ds(r0, chunk), :].astype(BF16), s.astype(BF16)) * cross_ref[h]
            o_scr[pl.ds(r0, chunk), cols] = o_scr[pl.ds(r0, chunk), cols] + inter
            upd = _tn(kt_scr[h, pl.ds(r0, chunk), :].astype(BF16), v_ref[pl.ds(r0, chunk), cols].astype(BF16))
            s_ref[j, h] = s * decay[h] + upd
        return carry

    lax.fori_loop(0, nseq, per_seq, 0)

    for h in range(RET_HEADS):
        cols = slice(h * RET_DV, (h + 1) * RET_DV)
        o = o_scr[:, cols]
        o = o * lax.rsqrt(jnp.mean(o * o, axis=-1, keepdims=True) + EPS)
        ga_ref[:, cols] = (_silu(g_ref[:, cols]) * o).astype(BF16)


def _retention(z, s0, *, n, t, pos0, chunk, nseq, layer=None, s_buf=None):
    nc = t // chunk
    rows = nseq * chunk
    assert t % chunk == 0 and n % nseq == 0 and (nseq == 1 or nc == 1)
    h = np.arange(RET_HEADS, dtype=np.float64)
    log_g = np.log1p(-np.exp2(-5.0 - h))
    idx = np.arange(chunk, dtype=np.float64)
    rel = idx[:, None] - idx[None, :]
    dmask = np.where(rel[None] >= 0, np.exp(np.maximum(rel, 0.0)[None] * log_g[:, None, None]), 0.0)
    dmask = np.stack([np.kron(np.eye(nseq), dmask[i]) for i in range(RET_HEADS)])
    cross = np.exp((idx + 1.0)[None, :] * log_g[:, None])
    tail = np.exp((chunk - 1.0 - idx)[None, :] * log_g[:, None])
    decay = tuple(float(x) for x in np.exp(chunk * log_g))
    cross_b = np.broadcast_to(cross[:, :, None], (RET_HEADS, chunk, RET_DV))
    tail_b = np.broadcast_to(np.tile(tail, (1, nseq))[:, :, None], (RET_HEADS, rows, RET_DK))
    pos = pos0 + np.arange(t, dtype=np.float64)
    inv = ROPE_BASE ** (-np.arange(0, RET_DK, 2, dtype=np.float64) / RET_DK)
    ang = pos[:, None] * inv[None, :]
    cos = np.repeat(np.cos(ang), 2, axis=1)
    sgn = np.tile(np.array([-1.0, 1.0]), RET_DK // 2)
    sin = np.repeat(np.sin(ang), 2, axis=1) * sgn[None, :]
    if nseq > 1:
        cos, sin = np.tile(cos, (nseq, 1)), np.tile(sin, (nseq, 1))
    c = lambda a: jnp.asarray(a, F32)
    kern = functools.partial(_retention_kernel, nseq=nseq, chunk=chunk, decay=decay)
    row_blk = lambda i, j: i * nc + j
    if layer is None:
        s_spec = pl.BlockSpec((nseq, RET_HEADS, RET_DK, RET_DV), lambda i, j: (i, 0, 0, 0))
    else:
        s_spec = pl.BlockSpec((None, nseq, RET_HEADS, RET_DK, RET_DV), lambda i, j: (layer, i, 0, 0, 0))
    extra, extra_specs, alias = (), [], {}
    if s_buf is not None:
        extra, extra_specs, alias = (s_buf,), [pl.BlockSpec(memory_space=pl.ANY)], {10: 1}
        kern = lambda *refs, _k=kern: _k(*refs[:10], *refs[11:])
    ga, s_new = pl.pallas_call(
        kern,
        out_shape=(jax.ShapeDtypeStruct((n * t, RET_V), BF16), jax.ShapeDtypeStruct(s0.shape, F32)),
        grid=(n // nseq, nc),
        input_output_aliases=alias,
        in_specs=[pl.BlockSpec((rows, RET_QK), lambda i, j: (row_blk(i, j), C_RQ // RET_QK)),
                  pl.BlockSpec((rows, RET_QK), lambda i, j: (row_blk(i, j), C_RK // RET_QK)),
                  pl.BlockSpec((rows, RET_V), lambda i, j: (row_blk(i, j), C_RV // RET_V)),
                  pl.BlockSpec((rows, RET_V), lambda i, j: (row_blk(i, j), C_RG // RET_V)),
                  pl.BlockSpec((rows, RET_DK), lambda i, j: (j, 0)),
                  pl.BlockSpec((rows, RET_DK), lambda i, j: (j, 0)),
                  pl.BlockSpec((RET_HEADS, rows, rows), lambda i, j: (0, 0, 0)),
                  pl.BlockSpec((RET_HEADS, chunk, RET_DV), lambda i, j: (0, 0, 0)),
                  pl.BlockSpec((RET_HEADS, rows, RET_DK), lambda i, j: (0, 0, 0)),
                  s_spec] + extra_specs,
        out_specs=(pl.BlockSpec((rows, RET_V), lambda i, j: (row_blk(i, j), 0)), s_spec),
        scratch_shapes=[pltpu.VMEM((RET_HEADS, rows, RET_DK), F32),
                        pltpu.VMEM((RET_HEADS, rows, RET_DK), F32),
                        pltpu.VMEM((rows, RET_V), F32)],
        compiler_params=_cparams(("parallel", "arbitrary")),
        name="retention",
    )(z, z, z, z, c(cos), c(sin), c(dmask), c(cross_b), c(tail_b), s0, *extra)
    return ga, s_new


def _pool_kernel(u_ref, pg_ref, halo_ref, w_ref, sc_ref, gb_ref, ext_scr, *, nb, tt, pos0, nbt):
    hal = 16
    ext_scr[:, 0:hal, :] = halo_ref[...]
    ext_scr[:, hal:hal + tt, :] = u_ref[...]
    pos_base = pos0 + (pl.program_id(0) * nb % nbt) * tt
    p1 = lax.broadcasted_iota(I32, (nb, tt, POOL_GDIM), 1) + pos_base + 1
    for g, w in enumerate(POOL_WINDOWS):
        cols = slice(g * POOL_GDIM, (g + 1) * POOL_GDIM)
        acc = ext_scr[:, hal:hal + tt, cols]
        for j in range(1, w):
            acc = acc + ext_scr[:, hal - j:hal - j + tt, cols]
        cnt = jnp.minimum(p1, w).astype(F32)
        d = acc / cnt - u_ref[:, :, cols]
        y = _dot(d.reshape(nb * tt, POOL_GDIM).astype(BF16), w_ref[g])
        y = y * sc_ref[:, cols]
        gate = _silu(pg_ref[:, :, cols]).reshape(nb * tt, POOL_GDIM)
        gb_ref[:, cols] = (gate * y).astype(BF16)


def _pool(z, halo, w_pool_b, scale, *, n, t, pos0, tt, nb):
    nbt = t // tt
    g_tot = n * nbt
    assert g_tot % nb == 0 and (nb == 1 or nbt == 1)
    z3 = z.reshape(g_tot, tt, NP)
    kern = functools.partial(_pool_kernel, nb=nb, tt=tt, pos0=pos0, nbt=nbt)
    return pl.pallas_call(
        kern,
        out_shape=jax.ShapeDtypeStruct((n * t, POOL_WIDTH), BF16),
        grid=(g_tot // nb,),
        in_specs=[pl.BlockSpec((nb, tt, POOL_WIDTH), lambda i: (i, 0, C_PU // POOL_WIDTH)),
                  pl.BlockSpec((nb, tt, POOL_WIDTH), lambda i: (i, 0, C_PG // POOL_WIDTH)),
                  pl.BlockSpec((nb, 16, POOL_WIDTH), lambda i: (i, 0, 0)),
                  pl.BlockSpec((POOL_GROUPS, POOL_GDIM, POOL_GDIM), lambda i: (0, 0, 0)),
                  pl.BlockSpec((1, POOL_WIDTH), lambda i: (0, 0))],
        out_specs=pl.BlockSpec((nb * tt, POOL_WIDTH), lambda i: (i, 0)),
        scratch_shapes=[pltpu.VMEM((nb, 16 + tt, POOL_WIDTH), F32)],
        compiler_params=_cparams(("parallel",)),
        name="pool",
    )(z3, z3, halo, w_pool_b, scale.reshape(1, POOL_WIDTH))


def _t5_bucket(dist):
    n = jnp.maximum(dist, 0)
    exact = N_BUCKETS // 2
    nf = jnp.maximum(n, 1).astype(F32)
    large = exact + (jnp.log(nf / exact) / math.log(MAX_DIST / exact) * (N_BUCKETS - exact)).astype(I32)
    large = jnp.minimum(large, N_BUCKETS - 1)
    return jnp.where(n < exact, n, large)


def _bias_table(rel_bias):
    rb = rel_bias.astype(F32)
    tab = rb[_t5_bucket(jnp.arange(2 * KT, dtype=I32))]
    return tab - rb[N_BUCKETS - 1][None, :]


def _toeplitz_T(tab, delta, rows_q, window=None):
    assert rows_q == KT
    d = delta + np.arange(-(KT - 1), KT)
    valid = d >= 0
    if window is not None:
        valid &= d < window
    near = valid & (d < 2 * KT)
    lo = int(np.clip(d[0], 0, 2 * KT))
    hi = int(np.clip(d[-1] + 1, 0, 2 * KT))
    seg = tab[lo:hi]
    r = jnp.zeros((2 * KT - 1, NSA_HEADS), F32)
    if hi > lo:
        r = lax.dynamic_update_slice(r, seg, (int(np.nonzero(d == lo)[0][0]), 0))
    r = jnp.where(jnp.asarray(near)[:, None], r, 0.0)
    r = jnp.where(jnp.asarray(valid)[:, None], r, NEG)
    rp = jnp.concatenate([r, jnp.zeros((1, NSA_HEADS), F32)], axis=0).T
    skew = jnp.broadcast_to(rp[:, None, :], (NSA_HEADS, KT, 2 * KT)).reshape(NSA_HEADS, KT * 2 * KT)
    skew = skew[:, :KT * (2 * KT - 1)].reshape(NSA_HEADS, KT, 2 * KT - 1)
    vals = skew[:, :, KT - 1:]
    return vals.reshape(NSA_KV_HEADS, NSA_GROUP, KT, rows_q).transpose(0, 2, 1, 3).reshape(NSA_KV_HEADS, KT, NSA_GROUP * rows_q)


def _cmp_rowbias(tab, pos, per_row_head):
    b = np.asarray(pos) % L_CMP
    cols = []
    for m in range(-1, 4):
        if m < 0:
            v = jnp.where(jnp.asarray(b == L_CMP - 1)[:, None], tab[0][None, :], NEG)
        else:
            v = tab[jnp.asarray(L_CMP * m + b + 1)]
        cols.append(v)
    r = jnp.stack(cols, axis=1)
    if per_row_head is None:
        r = r.reshape(r.shape[0], 5 * NSA_HEADS)
        return jnp.pad(r, ((0, 0), (0, LANES - 5 * NSA_HEADS)))
    r = r[jnp.arange(r.shape[0]), :, jnp.asarray(per_row_head)]
    return jnp.pad(r, ((0, 0), (0, LANES - 5)))


def _cmp_bias(dist, col):
    b = jnp.where(dist <= 4 * L_CMP, col(3), 0.0)
    b = jnp.where(dist <= 3 * L_CMP, col(2), b)
    b = jnp.where(dist <= 2 * L_CMP, col(1), b)
    b = jnp.where(dist <= L_CMP, col(0), b)
    return jnp.where(dist <= 0, col(-1), b)


def _select_mask(imp, sblk, qpos, k_top, axis):
    forced = (sblk == 0) | (sblk == qpos // L_SEL)
    avail = sblk * L_SEL <= qpos
    w = jnp.where(forced, FORCE_SCORE, imp)
    w = jnp.where(avail, w, -jnp.inf)
    idx = lax.broadcasted_iota(I32, w.shape, axis).astype(F32)
    for _ in range(k_top):
        mx = jnp.max(w, axis=axis, keepdims=True)
        first = jnp.min(jnp.where(w == mx, idx, float(4 * LANES)), axis=axis, keepdims=True)
        w = jnp.where(idx == first, -jnp.inf, w)
    return jnp.where(avail & (w == -jnp.inf), 0.0, NEG)


def _cmpkv_kernel(kv_ref, pe_ref, wk_ref, wv_ref, kc_ref, vc_ref):
    x = kv_ref[...]
    nblk = x.shape[0] // L_CMP
    cm = x.reshape(nblk, L_CMP, 2 * NSA_KV).sum(axis=1) * (1.0 / L_CMP)
    cm = cm + jnp.mean(pe_ref[...], axis=0, keepdims=True)
    kc_ref[...] = _dot(cm[:, :NSA_KV].astype(BF16), wk_ref[...])
    vc_ref[...] = _dot(cm[:, NSA_KV:].astype(BF16), wv_ref[...])


def _cmpkv(z, pe2, bd_ck, bd_cv, *, n, t):
    nblk = t // L_CMP
    z3 = z.reshape(n, t, NP)
    return pl.pallas_call(
        _cmpkv_kernel,
        out_shape=(jax.ShapeDtypeStruct((n, nblk, NSA_KV), F32), jax.ShapeDtypeStruct((n, nblk, NSA_KV), F32)),
        grid=(n,),
        in_specs=[pl.BlockSpec((None, nblk * L_CMP, 2 * NSA_KV), lambda i: (i, 0, C_KV // (2 * NSA_KV))),
                  pl.BlockSpec((L_CMP, 2 * NSA_KV), lambda i: (0, 0)),
                  pl.BlockSpec((NSA_KV, NSA_KV), lambda i: (0, 0)),
                  pl.BlockSpec((NSA_KV, NSA_KV), lambda i: (0, 0))],
        out_specs=(pl.BlockSpec((None, nblk, NSA_KV), lambda i: (i, 0, 0)),
                   pl.BlockSpec((None, nblk, NSA_KV), lambda i: (i, 0, 0))),
        compiler_params=_cparams(("parallel",)),
        name="cmp_kv",
    )(z3, pe2, bd_ck, bd_cv)


def _attn_store(o_ref, g, ot):
    for k in range(NSA_GROUP // 2):
        c0 = (g * NSA_GROUP + 2 * k) * NSA_DH
        o_ref[:, c0:c0 + 2 * NSA_DH] = jnp.concatenate(
            [ot[:, 2 * k * KT:(2 * k + 1) * KT], ot[:, (2 * k + 1) * KT:(2 * k + 2) * KT]], axis=0).T


def _cmpsel_kernel(q_ref, kc_ref, vct_ref, rbt_ref, cpos_ref, sblk_ref, oc_ref, mn_ref, s_scr, *, tq, k_top):
    gq = NSA_GROUP * tq
    qs = pl.program_id(1) * tq
    qpos = lax.broadcasted_iota(I32, (1, gq), 1) % tq + qs
    valid = qpos - cpos_ref[...] >= 0
    kc = kc_ref[...].astype(BF16)
    vct = vct_ref[...].astype(BF16)
    w0 = jnp.clip((2 * pl.program_id(1) - 2) // 8 * 8, 0, LANES - NEAR)
    imps = []
    for g in range(NSA_KV_HEADS):
        qg = q_ref[:, g * NSA_GROUP * NSA_DH:(g + 1) * NSA_GROUP * NSA_DH] * (NSA_DH ** -0.5)
        qt = jnp.concatenate([qg[:, k * LANES:(k + 1) * LANES].T for k in range(NSA_GROUP // 2)], axis=0)
        qt = jnp.concatenate([qt[r * NSA_DH:(r + 1) * NSA_DH] for r in range(NSA_GROUP)], axis=1).astype(BF16)
        s_scr[...] = _dot(kc[:, g * NSA_DH:(g + 1) * NSA_DH], qt)
        rbt = rbt_ref[g]
        for half in range(2):
            rows_w = pl.ds(pl.multiple_of(half * LANES + w0, 8), NEAR)
            dw = qpos - cpos_ref[rows_w, :]
            s_scr[rows_w, :] = s_scr[rows_w, :] + _cmp_bias(dw, lambda m: rbt[m + 1:m + 2, :])
        s = jnp.where(valid, s_scr[...], M_INIT)
        mx = jnp.max(s, axis=0, keepdims=True)
        e = jnp.where(valid, jnp.exp(s - mx), 0.0)
        p = e / jnp.maximum(jnp.sum(e, axis=0, keepdims=True), 1e-30)
        _attn_store(oc_ref, g, _dot(vct[g * NSA_DH:(g + 1) * NSA_DH, :], p.astype(BF16)))
        imp = p[:, 0:tq]
        for r in range(1, NSA_GROUP):
            imp = imp + p[:, r * tq:(r + 1) * tq]
        imps.append(imp[:LANES] + imp[LANES:])
    mn = _select_mask(jnp.concatenate(imps, axis=1), sblk_ref[...], qpos[:, 0:NSA_KV_HEADS * tq], k_top, 0).astype(BF16)
    for g in range(NSA_KV_HEADS):
        mn_ref[g] = mn[:, g * tq:(g + 1) * tq]


def _cmpsel(z, kcp, vctp, rowbt, cpos, sblk, *, n, t, k_top):
    tq = KT
    assert tq == 4 * L_CMP and 4 * L_CMP == KT
    nqt = t // tq
    gq = NSA_GROUP * tq
    kern = functools.partial(_cmpsel_kernel, tq=tq, k_top=k_top)
    return pl.pallas_call(
        kern,
        out_shape=(jax.ShapeDtypeStruct((n * t, NSA_WIDTH), F32),
                   jax.ShapeDtypeStruct((n, NSA_KV_HEADS, LANES, t), BF16)),
        grid=(n, nqt),
        in_specs=[pl.BlockSpec((tq, NSA_WIDTH), lambda b, i: (b * nqt + i, C_NQ // NSA_WIDTH)),
                  pl.BlockSpec((None, 2 * LANES, NSA_KV), lambda b, i: (b, 0, 0)),
                  pl.BlockSpec((None, NSA_KV, 2 * LANES), lambda b, i: (b, 0, 0)),
                  pl.BlockSpec((NSA_KV_HEADS, 8, gq), lambda b, i: (0, 0, 0)),
                  pl.BlockSpec((2 * LANES, gq), lambda b, i: (0, 0)),
                  pl.BlockSpec((LANES, NSA_KV_HEADS * tq), lambda b, i: (0, 0))],
        out_specs=(pl.BlockSpec((tq, NSA_WIDTH), lambda b, i: (b * nqt + i, 0)),
                   pl.BlockSpec((None, NSA_KV_HEADS, LANES, tq), lambda b, i: (b, 0, 0, i))),
        scratch_shapes=[pltpu.VMEM((2 * LANES, gq), F32)],
        compiler_params=_cparams(("parallel", "parallel")),
        name="cmp_select",
    )(z, kcp, vctp, rowbt, cpos, sblk)


def _kvpack_kernel(z_ref, oh_ref, *refs, nqt):
    ks_ref, vs_ref, vs4_ref, kw_ref, vw_ref, ct_ref, st_ref, wt_ref = refs[-8:]
    i = pl.program_id(1)
    real = (i >= KPAD) & (i < KPAD + nqt)
    kv = z_ref[...]
    kvt = [kv[:, j * NSA_KV:(j + 1) * NSA_KV].T for j in range(6)]
    for o_ref, j in ((ct_ref, 0), (st_ref, 2), (wt_ref, 4)):
        o_ref[0:NSA_KV, :] = kvt[j]
        o_ref[NSA_KV:2 * NSA_KV, :] = kvt[j + 1]

    @pl.when(real)
    def _():
        for k_ref, v_refs, j, lead in ((ks_ref, (vs_ref, vs4_ref), 2, oh_ref[...]),
                                       (kw_ref, (vw_ref,), 4, jnp.zeros((KT, LANES), BF16))):
            vt = kvt[j + 1].astype(BF16)
            for g in range(NSA_KV_HEADS):
                k_ref[g, :, 0:LANES] = lead
                k_ref[g, :, LANES:LANES + NSA_DH] = kv[:, j * NSA_KV + g * NSA_DH:j * NSA_KV + (g + 1) * NSA_DH].astype(BF16)
                for v_ref in v_refs:
                    v_ref[g] = vt[g * NSA_DH:(g + 1) * NSA_DH, :]

    @pl.when(jnp.logical_not(real))
    def _():
        for ref in (ks_ref, vs_ref, vs4_ref, kw_ref, vw_ref):
            ref[...] = jnp.zeros(ref.shape, BF16)


def _kvpack(z, onehot, *, n, t, layer, depth, bufs):
    nqt = t // KT
    ntile = nqt + 2 * KPAD
    assert ntile % CHUNK == 0
    tile = lambda i: jnp.clip(i - KPAD, 0, nqt - 1)
    kshape = jax.ShapeDtypeStruct((n, NSA_KV_HEADS, ntile, KT, LANES + NSA_DH), BF16)
    vshape = jax.ShapeDtypeStruct((n, NSA_KV_HEADS, ntile, NSA_DH, KT), BF16)
    v4shape = jax.ShapeDtypeStruct((n, NSA_KV_HEADS, ntile // CHUNK, NSA_DH, CHUNK * KT), BF16)
    rshape = jax.ShapeDtypeStruct((depth, n, 2 * NSA_KV, t), F32)
    kspec = pl.BlockSpec((None, NSA_KV_HEADS, None, KT, LANES + NSA_DH), lambda b, i: (b, 0, i, 0, 0))
    vspec = pl.BlockSpec((None, NSA_KV_HEADS, None, NSA_DH, KT), lambda b, i: (b, 0, i, 0, 0))
    v4spec = pl.BlockSpec((None, NSA_KV_HEADS, None, NSA_DH, KT), lambda b, i: (b, 0, i // CHUNK, 0, i % CHUNK))
    rspec = pl.BlockSpec((None, None, 2 * NSA_KV, KT), lambda b, i: (layer, b, 0, tile(i)))
    extra, extra_specs, alias = (), [], {}
    if bufs is not None:
        extra, extra_specs, alias = tuple(bufs), [pl.BlockSpec(memory_space=pl.ANY)] * 3, {2: 5, 3: 6, 4: 7}
    return pl.pallas_call(
        functools.partial(_kvpack_kernel, nqt=nqt),
        out_shape=(kshape, vshape, v4shape, kshape, vshape, rshape, rshape, rshape),
        grid=(n, ntile),
        in_specs=[pl.BlockSpec((KT, 1024), lambda b, i: (b * nqt + tile(i), C_KV // 1024)),
                  pl.BlockSpec((KT, LANES), lambda b, i: (tile(i), 0))] + extra_specs,
        out_specs=(kspec, vspec, v4spec, kspec, vspec, rspec, rspec, rspec),
        input_output_aliases=alias,
        compiler_params=_cparams(("parallel", "arbitrary")),
        name="kv_pack",
    )(z, onehot, *extra)


def _qk_chunk(k_ref, t0, ntile, qpt_ref):
    k = k_ref[pl.ds(t0, ntile)].reshape(ntile * KT, LANES + NSA_DH)
    return _dot(k, qpt_ref[...])


def _online_softmax(st, pv, m, l, acc):
    m_new = jnp.maximum(m, jnp.max(st, axis=0, keepdims=True))
    alpha = jnp.exp2(m - m_new)
    p = jnp.exp2(st - m_new)
    l_new = alpha * l + jnp.sum(p, axis=0, keepdims=True)
    return m_new, l_new, alpha * acc + pv(p.astype(BF16))


def _pv_tiles(v_ref, t0, ntile):
    def pv(pb):
        out = _dot(v_ref[t0], pb[0:KT])
        for j in range(1, ntile):
            out = out + _dot(v_ref[t0 + j], pb[j * KT:(j + 1) * KT])
        return out
    return pv


def _selwin_kernel(q_ref, mn_ref, ks_ref, vs_ref, vs4_ref, kw_ref, vw_ref, asel_ref, awin_ref, os_ref, ow_ref,
                   qpt_scr, sta_scr, stb_scr, m_scr, l_scr, acc_scr):
    qt = pl.program_id(1)
    heads = range(NSA_KV_HEADS)
    gq = NSA_GROUP * KT
    for g in heads:
        mn = mn_ref[g]
        for k in range(NSA_GROUP // 2):
            c0 = (g * NSA_GROUP + 2 * k) * NSA_DH
            qt2 = (q_ref[:, c0:c0 + 2 * NSA_DH] * (NSA_DH ** -0.5 * LOG2E)).T.astype(BF16)
            for u in range(2):
                r = 2 * k + u
                qpt_scr[g, 0:LANES, r * KT:(r + 1) * KT] = mn
                qpt_scr[g, LANES:LANES + NSA_DH, r * KT:(r + 1) * KT] = qt2[u * NSA_DH:(u + 1) * NSA_DH]
        m_scr[g] = jnp.full((1, gq), M_INIT, F32)
        l_scr[g] = jnp.zeros((1, gq), F32)
        acc_scr[g] = jnp.zeros((NSA_DH, gq), F32)

    nplain = jnp.maximum(qt - 1, 0) // CHUNK

    def scores(st_scr, c):
        for g in heads:
            st_scr[g] = _qk_chunk(ks_ref.at[g], KPAD + CHUNK * c, CHUNK, qpt_scr.at[g])

    def update(st_scr, c):
        for g in heads:
            m, l, acc = _online_softmax(st_scr[g], lambda pb, g=g: _dot(vs4_ref[g, KPAD // CHUNK + c], pb),
                                        m_scr[g], l_scr[g], acc_scr[g])
            m_scr[g] = m
            l_scr[g] = l
            acc_scr[g] = acc

    @pl.when(nplain > 0)
    def _():
        scores(sta_scr, 0)

    def pair(i, carry):
        c0 = 2 * i
        scores(stb_scr, c0 + 1)
        update(sta_scr, c0)
        scores(sta_scr, jnp.minimum(c0 + 2, nplain - 1))
        update(stb_scr, c0 + 1)
        return carry

    lax.fori_loop(0, nplain // 2, pair, 0)

    @pl.when(nplain % 2 == 1)
    def _():
        update(sta_scr, nplain - 1)

    e0 = CHUNK * nplain
    off = pl.multiple_of((EDGE - (qt - e0)) * KT, KT)
    for g in heads:
        st = _qk_chunk(ks_ref.at[g], KPAD + e0, EDGE, qpt_scr.at[g]) + asel_ref[g, pl.ds(off, EDGE * KT), :]
        _, l, acc = _online_softmax(st, _pv_tiles(vs_ref.at[g], KPAD + e0, EDGE), m_scr[g], l_scr[g], acc_scr[g])
        _attn_store(os_ref, g, acc / l)

    for g in heads:
        st = _qk_chunk(kw_ref.at[g], qt, EDGE, qpt_scr.at[g]) + awin_ref[g]
        rows = []
        for j in range(EDGE):
            blk = st[j * KT:(j + 1) * KT]
            if j < EDGE - 1:
                blk = blk + jnp.where(qt >= EDGE - 1 - j, 0.0, NEG)
            rows.append(blk)
        st = jnp.concatenate(rows, axis=0)
        _, l, acc = _online_softmax(st, _pv_tiles(vw_ref.at[g], qt, EDGE), jnp.full((1, gq), M_INIT, F32),
                                    jnp.zeros((1, gq), F32), jnp.zeros((NSA_DH, gq), F32))
        _attn_store(ow_ref, g, acc / l)


def _selwin(z, mnt, ks, vs, vs4, kw, vw, asel, awin, *, n, t):
    nqt = t // KT
    ntile = nqt + 2 * KPAD
    gq = NSA_GROUP * KT
    hk = NSA_KV_HEADS
    once = pl.Buffered(1)
    kspec = pl.BlockSpec((None, hk, ntile, KT, LANES + NSA_DH), lambda b, i: (b, 0, 0, 0, 0), pipeline_mode=once)
    vspec = pl.BlockSpec((None, hk, ntile, NSA_DH, KT), lambda b, i: (b, 0, 0, 0, 0), pipeline_mode=once)
    v4spec = pl.BlockSpec((None, hk, ntile // CHUNK, NSA_DH, CHUNK * KT), lambda b, i: (b, 0, 0, 0, 0), pipeline_mode=once)
    ospec = pl.BlockSpec((KT, NSA_WIDTH), lambda b, i: (b * nqt + i, 0))
    return pl.pallas_call(
        _selwin_kernel,
        out_shape=(jax.ShapeDtypeStruct((n * t, NSA_WIDTH), F32), jax.ShapeDtypeStruct((n * t, NSA_WIDTH), F32)),
        grid=(n, nqt),
        in_specs=[pl.BlockSpec((KT, NSA_WIDTH), lambda b, i: (b * nqt + i, C_NQ // NSA_WIDTH)),
                  pl.BlockSpec((None, hk, LANES, KT), lambda b, i: (b, 0, 0, i)),
                  kspec, vspec, v4spec, kspec, vspec,
                  pl.BlockSpec((hk, 2 * EDGE * KT, gq), lambda b, i: (0, 0, 0), pipeline_mode=once),
                  pl.BlockSpec((hk, EDGE * KT, gq), lambda b, i: (0, 0, 0), pipeline_mode=once)],
        out_specs=(ospec, ospec),
        scratch_shapes=[pltpu.VMEM((hk, LANES + NSA_DH, gq), BF16),
                        pltpu.VMEM((hk, CHUNK * KT, gq), F32),
                        pltpu.VMEM((hk, CHUNK * KT, gq), F32),
                        pltpu.VMEM((hk, 1, gq), F32),
                        pltpu.VMEM((hk, 1, gq), F32),
                        pltpu.VMEM((hk, NSA_DH, gq), F32)],
        compiler_params=_cparams(("parallel", "arbitrary")),
        name="sel_win",
    )(z, mnt, ks, vs, vs4, kw, vw, asel, awin)


def _nsa_prompt(z, pe2, bd_ck, bd_cv, tab, *, n, t, layer, depth, row_bufs):
    assert t % KT == 0 and WINDOW == (EDGE - 1) * KT and KT % L_SEL == 0 and EDGE <= KPAD + 1 and EDGE <= CHUNK + 1
    nblk = t // L_CMP
    n_sel = -(-t // L_SEL)
    assert nblk <= 2 * LANES and n_sel <= LANES and nblk % 2 == 0
    kc, vc = _cmpkv(z, pe2, bd_ck, bd_cv, n=n, t=t)
    half = nblk // 2

    def perm(a):
        z0 = jnp.zeros((n, LANES - half, NSA_KV), F32)
        return jnp.concatenate([a[:, 0::2], z0, a[:, 1::2], z0], axis=1)

    slot = np.arange(LANES)
    big = 1 << 20
    cpos = np.concatenate([np.where(slot < half, (2 * slot) * L_CMP + L_CMP - 1, big),
                           np.where(slot < half, (2 * slot + 1) * L_CMP + L_CMP - 1, big)])
    cpos = np.broadcast_to(cpos[:, None], (2 * LANES, NSA_GROUP * KT))
    sblk = np.broadcast_to(np.where(slot < n_sel, slot, big)[:, None], (LANES, NSA_KV_HEADS * KT))
    rowb = _cmp_rowbias(tab, np.arange(KT), None)[:, :5 * NSA_HEADS].reshape(KT, 5, NSA_KV_HEADS, NSA_GROUP)
    rowbt = jnp.pad(rowb.transpose(2, 1, 3, 0).reshape(NSA_KV_HEADS, 5, NSA_GROUP * KT), ((0, 0), (0, 3), (0, 0)))
    o_cmp, mnt = _cmpsel(z, perm(kc), perm(vc).transpose(0, 2, 1), rowbt, jnp.asarray(cpos, I32), jnp.asarray(sblk, I32),
                         n=n, t=t, k_top=min(N_SEL, n_sel))

    onehot = jnp.asarray(np.arange(t)[:, None] // L_SEL == np.arange(LANES)[None, :], BF16)
    ks, vs, vs4, kw, vw, *row_bufs = _kvpack(z, onehot, n=n, t=t, layer=layer, depth=depth, bufs=row_bufs)
    a0 = _toeplitz_T(tab, 0, KT) * LOG2E
    a1 = _toeplitz_T(tab, KT, KT) * LOG2E
    aw = _toeplitz_T(tab, WINDOW, KT, window=WINDOW) * LOG2E
    zeros = jnp.zeros_like(a0)
    asel = jnp.concatenate([zeros] * (EDGE - 1) + [a1, a0] + [jnp.full_like(a0, NEG)] * (EDGE - 1), axis=1)
    awin = jnp.concatenate([aw] + [zeros] * (EDGE - 3) + [a1, a0], axis=1)
    o_sel, o_win = _selwin(z, mnt, ks, vs, vs4, kw, vw, asel, awin, n=n, t=t)
    return o_cmp, o_sel, o_win, row_bufs


def _nsa_sample_kernel(pt_ref, *refs, npg, t, past, wbuf, k_top, aliased):
    del pt_ref
    cpages = refs[0:npg]
    spages = refs[npg:2 * npg]
    rest = refs[2 * npg:]
    (q_ref, kvn_ref, cw_ref, pe_ref, wk_ref, wv_ref, rb_ref, cpos_ref, sblk_ref, oht_ref, asel_ref, awin_ref) = rest[:12]
    (oc_ref, os_ref, ow_ref, wout_ref, kselt, vselt, kwint, vwint, cm_scr, kcv_scr, qw_scr, qp_scr) = rest[12 + aliased:]
    rows = NSA_HEADS * t
    half = NSA_GROUP * t
    kvn = kvn_ref[...]

    @pl.when(pl.program_id(0) == 0)
    def _():
        kselt[0:LANES, :] = oht_ref[...]
        kcv_scr[...] = jnp.zeros(kcv_scr.shape, F32)
        qw_scr[...] = jnp.zeros(qw_scr.shape, F32)

    kvpad = jnp.concatenate([kvn[:, 0:6 * NSA_KV], jnp.zeros((KT - t, 6 * NSA_KV), F32)], axis=0)
    new_t = [kvpad[:, j * NSA_KV:(j + 1) * NSA_KV].T for j in range(2, 6)]

    bpp = PAGE_SIZE // L_CMP
    for p in range(npg):
        kselt[LANES:2 * LANES, p * PAGE_SIZE:(p + 1) * PAGE_SIZE] = spages[p][0:NSA_KV, :].astype(BF16)
        vselt[:, p * PAGE_SIZE:(p + 1) * PAGE_SIZE] = spages[p][NSA_KV:2 * NSA_KV, :].astype(BF16)
        for c in range(2):
            blk = cpages[p][c * NSA_KV:(c + 1) * NSA_KV, :].T
            cm_scr[p * bpp:(p + 1) * bpp, c * NSA_KV:(c + 1) * NSA_KV] = blk.reshape(bpp, L_CMP, NSA_KV).sum(axis=1) * (1.0 / L_CMP)
    kselt[LANES:2 * LANES, past:past + KT] = new_t[0].astype(BF16)
    vselt[:, past:past + KT] = new_t[1].astype(BF16)
    cw = cw_ref[...]
    kwint[:, 0:wbuf] = cw[0:NSA_KV, :].astype(BF16)
    vwint[:, 0:wbuf] = cw[NSA_KV:2 * NSA_KV, :].astype(BF16)
    kwint[:, wbuf:wbuf + KT] = new_t[2].astype(BF16)
    vwint[:, wbuf:wbuf + KT] = new_t[3].astype(BF16)
    keep = wout_ref.shape[1]
    for c in range(2):
        wout_ref[c * NSA_KV:(c + 1) * NSA_KV, :] = jnp.concatenate(
            [cw[c * NSA_KV:(c + 1) * NSA_KV, wbuf + t - keep:wbuf], new_t[2 + c][:, 0:t]], axis=1)

    for h in range(NSA_HEADS):
        g = h // NSA_GROUP
        qh = q_ref[:, h * NSA_DH:(h + 1) * NSA_DH] * (NSA_DH ** -0.5)
        qw_scr[h * t:(h + 1) * t, g * NSA_DH:(g + 1) * NSA_DH] = qh
    qw = qw_scr[...].astype(BF16)

    def pick(x):
        row = lax.broadcasted_iota(I32, (rows, NSA_DH), 0)
        return jnp.where(row < half, x[:, :NSA_DH], x[:, NSA_DH:])

    def store(o_ref, x):
        for h in range(NSA_HEADS):
            o_ref[:, h * NSA_DH:(h + 1) * NSA_DH] = x[h * t:(h + 1) * t, :]

    nblk = npg * bpp
    cm = cm_scr[...] + jnp.mean(pe_ref[...], axis=0, keepdims=True)
    kcv_scr[0:nblk, 0:NSA_KV] = _dot(cm[:, :NSA_KV].astype(BF16), wk_ref[...])
    kcv_scr[0:nblk, NSA_KV:] = _dot(cm[:, NSA_KV:].astype(BF16), wv_ref[...])
    kc = kcv_scr[:, 0:NSA_KV].astype(BF16)
    vc = kcv_scr[:, NSA_KV:].astype(BF16)
    qpos = lax.broadcasted_iota(I32, (rows, 1), 0) % t + past
    dist = qpos - cpos_ref[...]
    valid = dist >= 0
    rb = rb_ref[...]
    s = _nt(qw, kc) + _cmp_bias(dist, lambda m: rb[:, m + 1:m + 2])
    s = jnp.where(valid, s, M_INIT)
    mx = jnp.max(s, axis=-1, keepdims=True)
    e = jnp.where(valid, jnp.exp(s - mx), 0.0)
    p = e / jnp.maximum(jnp.sum(e, axis=-1, keepdims=True), 1e-30)
    store(oc_ref, pick(_dot(p.astype(BF16), vc)))

    imps = []
    for g in range(NSA_KV_HEADS):
        imp = p[g * half:g * half + t]
        for r in range(1, NSA_GROUP):
            imp = imp + p[g * half + r * t:g * half + (r + 1) * t]
        imps.append(imp + pltpu.roll(imp, LANES - 1, 1))
    imps.append(jnp.zeros((LANES - NSA_KV_HEADS * t, LANES), F32))
    imp_t = jnp.concatenate(imps, axis=0).T
    qpos_l = lax.broadcasted_iota(I32, (1, LANES), 1) % t + past
    mn = _select_mask(imp_t, sblk_ref[...], qpos_l, k_top, 0).T.astype(BF16)
    for g in range(NSA_KV_HEADS):
        for r in range(NSA_GROUP):
            qp_scr[g * half + r * t:g * half + (r + 1) * t, 0:LANES] = mn[g * t:(g + 1) * t]
    qp_scr[:, LANES:2 * LANES] = qw

    def attend(qmat, kt_all, vt_all, add):
        sc = _dot(qmat, kt_all) + add
        mm = jnp.max(sc, axis=-1, keepdims=True)
        ee = jnp.exp(sc - mm)
        ll = jnp.sum(ee, axis=-1, keepdims=True)
        return pick(_nt(ee.astype(BF16), vt_all)) / ll

    store(os_ref, attend(qp_scr[...], kselt[...], vselt[...], asel_ref[...]))
    store(ow_ref, attend(qw, kwint[...], vwint[...], awin_ref[...]))


def _nsa_sample(z, cache_cmp_t, cache_sel_t, cache_win_t, page_table, pe2, bd_ck, bd_cv, tab, *, n, t, past, layer, win_buf):
    npg = page_table.shape[1]
    depth = cache_win_t.shape[0]
    assert past == npg * PAGE_SIZE and past % L_SEL == 0 and past % L_CMP == 0 and PAGE_SIZE == KT
    wbuf = cache_win_t.shape[3]
    keep = min(WINDOW, wbuf + t)
    nblk = past // L_CMP
    assert t < L_CMP and nblk <= LANES and wbuf % KT == 0 and keep > t
    length = past + t
    n_sel = -(-length // L_SEL)
    assert 2 * (n_sel - 1) < LANES and NSA_KV_HEADS * t <= LANES
    rows = NSA_HEADS * t
    lsel = past + KT
    lwin = wbuf + KT
    big = 1 << 20
    slot = np.arange(LANES)
    cpos = np.where(slot < nblk, slot * L_CMP + L_CMP - 1, big)[None, :]
    sblk = np.broadcast_to(np.where((slot % 2 == 0) & (slot // 2 < n_sel), slot // 2, big)[:, None], (LANES, LANES))
    head_of_row = np.repeat(np.arange(NSA_HEADS), t)
    tok_of_row = np.tile(np.arange(t), NSA_HEADS)
    rowb = _cmp_rowbias(tab, past + tok_of_row, head_of_row)
    kpos_sel = np.arange(lsel)
    onehot_t = jnp.asarray((kpos_sel[None, :] < length) & (2 * (kpos_sel[None, :] // L_SEL) == slot[:, None]), BF16)

    def additive(kpos, exists, window):
        d = (past + np.arange(t))[:, None] - kpos[None, :]
        valid = (d >= 0) & exists[None, :]
        if window is not None:
            valid &= d < window
        near = valid & (d < 2 * KT)
        j0 = int(np.nonzero(near.any(axis=0))[0].min()) // KT * KT
        vals = tab[jnp.asarray(np.clip(d[:, j0:], 0, 2 * KT - 1))]
        vals = jnp.where(jnp.asarray(near[:, j0:])[:, :, None], vals, 0.0)
        vals = jnp.pad(vals, ((0, 0), (j0, 0), (0, 0)))
        vals = jnp.where(jnp.asarray(valid)[:, :, None], vals, NEG)
        return vals.transpose(2, 0, 1).reshape(NSA_HEADS * t, kpos.shape[0])

    asel = additive(kpos_sel, kpos_sel < length, None)
    kidx = np.arange(lwin)
    awin = additive(past - wbuf + kidx, kidx < wbuf + t, WINDOW)

    page_spec = lambda p: pl.BlockSpec((None, None, 2 * NSA_KV, PAGE_SIZE), lambda i, pt, p=p: (layer, pt[i, p], 0, 0))
    const = lambda shape: pl.BlockSpec(shape, lambda i, pt: tuple(0 for _ in shape))
    ospec = pl.BlockSpec((t, NSA_WIDTH), lambda i, pt: (i, 0))
    aliased = win_buf is not None
    extra, extra_specs, alias = (), [], {}
    if aliased:
        extra, extra_specs, alias = (win_buf,), [pl.BlockSpec(memory_space=pl.ANY)], {1 + 2 * npg + 12: 3}
    kern = functools.partial(_nsa_sample_kernel, npg=npg, t=t, past=past, wbuf=wbuf, k_top=min(N_SEL, n_sel), aliased=int(aliased))
    grid_spec = pltpu.PrefetchScalarGridSpec(
        num_scalar_prefetch=1,
        grid=(n,),
        in_specs=[page_spec(p) for p in range(npg)] + [page_spec(p) for p in range(npg)] + [
            pl.BlockSpec((t, NSA_WIDTH), lambda i, pt: (i, C_NQ // NSA_WIDTH)),
            pl.BlockSpec((t, 1024), lambda i, pt: (i, C_KV // 1024)),
            pl.BlockSpec((None, None, 2 * NSA_KV, wbuf), lambda i, pt: (layer, i, 0, 0)),
            const((L_CMP, 2 * NSA_KV)), const((NSA_KV, NSA_KV)), const((NSA_KV, NSA_KV)),
            const((rows, LANES)), const((1, LANES)), const((LANES, LANES)),
            const((LANES, lsel)), const((rows, lsel)), const((rows, lwin))] + extra_specs,
        out_specs=(ospec, ospec, ospec, pl.BlockSpec((None, None, 2 * NSA_KV, keep), lambda i, pt: (layer, i, 0, 0))),
        scratch_shapes=[pltpu.VMEM((2 * LANES, lsel), BF16), pltpu.VMEM((LANES, lsel), BF16),
                        pltpu.VMEM((LANES, lwin), BF16), pltpu.VMEM((LANES, lwin), BF16),
                        pltpu.VMEM((nblk, 2 * NSA_KV), F32), pltpu.VMEM((LANES, 2 * NSA_KV), F32),
                        pltpu.VMEM((rows, NSA_KV), F32),
                        pltpu.VMEM((rows, 2 * LANES), BF16)])
    o_cmp, o_sel, o_win, win_new = pl.pallas_call(
        kern,
        grid_spec=grid_spec,
        out_shape=(jax.ShapeDtypeStruct((n * t, NSA_WIDTH), F32),) * 3
        + (jax.ShapeDtypeStruct((depth, n, 2 * NSA_KV, keep), F32),),
        input_output_aliases=alias,
        compiler_params=_cparams(("arbitrary",)),
        name="nsa_sample",
    )(page_table, *([cache_cmp_t] * npg), *([cache_sel_t] * npg), z, z, cache_win_t, pe2, bd_ck, bd_cv, rowb,
      jnp.asarray(cpos, I32), jnp.asarray(sblk, I32), onehot_t, asel, awin, *extra)
    return o_cmp, o_sel, o_win, win_new


def _merge_kernel(x_ref, mg_ref, ga_ref, gb_ref, oc_ref, os_ref, ow_ref, nbg_ref, ng_ref,
                  wa_ref, wb_ref, wc_ref, wo_ref, fg_ref, y_ref, *, final):
    tm = x_ref.shape[0]
    gates = jax.nn.sigmoid(nbg_ref[...])
    lane = lax.broadcasted_iota(I32, (tm, LANES), 1)
    low = lane < NSA_DH
    parts = []
    for pi in range(NSA_HEADS // 2):
        cols = slice(pi * LANES, (pi + 1) * LANES)
        acc = jnp.zeros((tm, LANES), F32)
        for b, o_ref in enumerate((oc_ref, os_ref, ow_ref)):
            c0 = b * NSA_HEADS + 2 * pi
            gt = jnp.where(low, gates[:, c0:c0 + 1], gates[:, c0 + 1:c0 + 2])
            acc = acc + gt * o_ref[:, cols]
        parts.append((_silu(ng_ref[:, cols]) * acc).astype(BF16))
    gc = jnp.concatenate(parts, axis=1)
    b_a = _dot(ga_ref[...], wa_ref[...])
    b_b = _dot(gb_ref[...], wb_ref[...])
    b_c = _dot(gc, wc_ref[...])
    mg = mg_ref[...]
    merged = (jax.nn.sigmoid(mg[:, 0:D_MODEL]) * b_a + jax.nn.sigmoid(mg[:, D_MODEL:2 * D_MODEL]) * b_b
              + jax.nn.sigmoid(mg[:, 2 * D_MODEL:]) * b_c)
    y = x_ref[...] + _dot(merged.astype(BF16), wo_ref[...])
    if final:
        y = y * lax.rsqrt(jnp.mean(y * y, axis=-1, keepdims=True) + EPS) * fg_ref[...]
    y_ref[...] = y


def _merge(x2, z, ga, gb, o_cmp, o_sel, o_win, wa, wb, wc, wo, fgain, *, final):
    m = x2.shape[0]
    tm = min(512, m)
    row = lambda w, c=0: pl.BlockSpec((tm, w), lambda i, c=c: (i, c))
    const = lambda a: pl.BlockSpec(a.shape, lambda i: (0, 0), pipeline_mode=pl.Buffered(1))
    fg = fgain.reshape(1, D_MODEL)
    return pl.pallas_call(
        functools.partial(_merge_kernel, final=final),
        out_shape=jax.ShapeDtypeStruct((m, D_MODEL), F32),
        grid=(m // tm,),
        in_specs=[row(D_MODEL), row(3 * D_MODEL, C_MG // (3 * D_MODEL)), row(RET_V), row(POOL_WIDTH),
                  row(NSA_WIDTH), row(NSA_WIDTH), row(NSA_WIDTH),
                  row(LANES, C_NBG // LANES), row(NSA_WIDTH, C_NG // NSA_WIDTH),
                  const(wa), const(wb), const(wc), const(wo), const(fg)],
        out_specs=row(D_MODEL),
        compiler_params=_cparams(("parallel",)),
        name="merge",
    )(x2, z, ga, gb, o_cmp, o_sel, o_win, z, z, wa, wb, wc, wo, fg)


def _kv_rows(z, n, t):
    kv = z[:, C_KV:C_KV + 6 * NSA_KV].reshape(n, t, 3, 2, NSA_KV_HEADS, NSA_DH)
    return kv[:, :, 0], kv[:, :, 1], kv[:, :, 2]


def _slots_minor(cache):
    d, x, slots = cache.shape[:3]
    return cache.transpose(0, 1, 3, 4, 5, 2).reshape(d, x, 2 * NSA_KV, slots)


def _slots_major(rows_t):
    d, n, _, slots = rows_t.shape
    return rows_t.reshape(d, n, 2, NSA_KV_HEADS, NSA_DH, slots).transpose(0, 1, 5, 2, 3, 4)


def kernel(x_prompt, x_sample, state_ret, state_pool, cache_win, cache_cmp, cache_sel, page_table, norm_gain, w_in, w_pool, pool_scale, pe_cmp, w_ck, w_cv, w_br_a, w_br_b, w_br_c, w_out, rel_bias, final_gain):
    b, seq, _ = x_prompt.shape
    nb, dseq, _ = x_sample.shape
    depth = w_in.shape[0]
    past = page_table.shape[1] * PAGE_SIZE
    pt = page_table.astype(I32)

    w_in_p = _pack_w_in(w_in)
    w_pool_b = w_pool.astype(BF16)
    wa, wb, wc, wo = (w.astype(BF16) for w in (w_br_a, w_br_b, w_br_c, w_out))
    eye = jnp.eye(NSA_KV_HEADS, dtype=F32)
    bd_ck = jax.vmap(lambda w: jnp.kron(eye, w))(w_ck).astype(BF16)
    bd_cv = jax.vmap(lambda w: jnp.kron(eye, w))(w_cv).astype(BF16)
    pe2 = pe_cmp.reshape(depth, L_CMP, 2 * NSA_KV)
    tab = _bias_table(rel_bias)
    cmp_t, sel_t, win_t = _slots_minor(cache_cmp), _slots_minor(cache_sel), _slots_minor(cache_win)

    ret_chunk = 256 if seq % 256 == 0 else seq
    pool_tt = min(1024, seq)
    samp_nseq = 8 if nb % 8 == 0 else 1
    samp_nb = 16 if nb % 16 == 0 else 1

    hp = x_prompt.reshape(b * seq, D_MODEL)
    hs = x_sample.reshape(nb * dseq, D_MODEL)
    zero_state = jnp.zeros((b, RET_HEADS, RET_DK, RET_DV), F32)
    ret_p, pool_p, pool_s, cmp_s, sel_s = [], [], [], [], []
    row_bufs = ret_s = win_s = None
    for l in range(depth):
        final = l == depth - 1
        z = _inproj(hp, norm_gain[l], w_in_p[l])
        ga, s_new = _retention(z, zero_state, n=b, t=seq, pos0=0, chunk=ret_chunk, nseq=1)
        u = z[:, C_PU:C_PU + POOL_WIDTH].reshape(b, seq, POOL_WIDTH)
        nbt = seq // pool_tt
        halo = u.reshape(b, nbt, pool_tt, POOL_WIDTH)[:, :, pool_tt - 16:, :]
        halo = jnp.concatenate([jnp.zeros((b, 1, 16, POOL_WIDTH), F32), halo[:, :nbt - 1]], axis=1)
        gb = _pool(z, halo.reshape(b * nbt, 16, POOL_WIDTH), w_pool_b[l], pool_scale[l], n=b, t=seq, pos0=0, tt=pool_tt, nb=1)
        o_cmp, o_sel, o_win, row_bufs = _nsa_prompt(z, pe2[l], bd_ck[l], bd_cv[l], tab, n=b, t=seq, layer=l, depth=depth,
                                                    row_bufs=row_bufs)
        hp = _merge(hp, z, ga, gb, o_cmp, o_sel, o_win, wa[l], wb[l], wc[l], wo[l], final_gain, final=final)
        ret_p.append(s_new)
        pool_p.append(u[:, seq - POOL_BUF:])

        z = _inproj(hs, norm_gain[l], w_in_p[l])
        ga, ret_s = _retention(z, state_ret, n=nb, t=dseq, pos0=past, chunk=dseq, nseq=samp_nseq, layer=l, s_buf=ret_s)
        u = z[:, C_PU:C_PU + POOL_WIDTH].reshape(nb, dseq, POOL_WIDTH)
        ext = jnp.concatenate([state_pool[l], u], axis=1)
        halo = jnp.pad(state_pool[l], ((0, 0), (16 - POOL_BUF, 0), (0, 0)))
        gb = _pool(z, halo, w_pool_b[l], pool_scale[l], n=nb, t=dseq, pos0=past, tt=dseq, nb=samp_nb)
        o_cmp, o_sel, o_win, win_s = _nsa_sample(z, cmp_t, sel_t, win_t, pt, pe2[l], bd_ck[l], bd_cv[l], tab,
                                                 n=nb, t=dseq, past=past, layer=l, win_buf=win_s)
        cmp_new, sel_new, _ = _kv_rows(z, nb, dseq)
        hs = _merge(hs, z, ga, gb, o_cmp, o_sel, o_win, wa[l], wb[l], wc[l], wo[l], final_gain, final=final)
        pool_s.append(ext[:, ext.shape[1] - POOL_BUF:])
        cmp_s.append(cmp_new)
        sel_s.append(sel_new)

    y_prompt = hp.reshape(b, seq, D_MODEL)
    y_sample = hs.reshape(nb, dseq, D_MODEL)
    cmp_p, sel_p, win_p = row_bufs
    keep_p = min(WINDOW, seq)
    st = lambda xs: jnp.stack(xs)
    return (y_prompt, y_sample, st(ret_p), ret_s, st(pool_p), st(pool_s),
            _slots_major(win_p[:, :, :, seq - keep_p:]), _slots_major(win_s),
            _slots_major(cmp_p), st(cmp_s), _slots_major(sel_p), st(sel_s))
```

```python
import functools
import math

import numpy as np
import jax
import jax.numpy as jnp
from jax import lax
from jax.experimental import pallas as pl
from jax.experimental.pallas import tpu as pltpu

F32 = jnp.float32
BF16 = jnp.bfloat16
I32 = jnp.int32

D_MODEL = 1024
PAGE_SIZE = 128
RET_HEADS, RET_DK, RET_DV = 4, 128, 256
RET_QK, RET_V = RET_HEADS * RET_DK, RET_HEADS * RET_DV
ROPE_BASE = 10000.0
POOL_WINDOWS = (2, 4, 8, 16)
POOL_GROUPS, POOL_GDIM = 4, 128
POOL_WIDTH = POOL_GROUPS * POOL_GDIM
POOL_BUF = 15
NSA_HEADS, NSA_KV_HEADS, NSA_DH = 8, 2, 64
NSA_GROUP = NSA_HEADS // NSA_KV_HEADS
NSA_WIDTH, NSA_KV = NSA_HEADS * NSA_DH, NSA_KV_HEADS * NSA_DH
L_CMP, L_SEL, N_SEL, WINDOW = 32, 64, 16, 512
FORCE_SCORE = 1e4
N_BUCKETS, MAX_DIST = 32, 128
EPS = 1e-6
IN_SIZES = (RET_QK, RET_QK, RET_V, RET_V, POOL_WIDTH, POOL_WIDTH, NSA_WIDTH,
            NSA_KV, NSA_KV, NSA_KV, NSA_KV, NSA_KV, NSA_KV, 3 * NSA_HEADS, NSA_WIDTH, 3 * D_MODEL)

LANES = 128
VMEM_LIMIT = 56 * 1024 * 1024

C_MG, C_RV, C_RG, C_RQ, C_RK = 0, 3072, 4096, 5120, 5632
C_PU, C_PG, C_NQ, C_NG, C_KV, C_NBG = 6144, 6656, 7168, 7680, 8192, 8960
NP = 9216
TM_INPROJ, TN_INPROJ = 256, 1024
KT = 128
KPAD = 4
CHUNK = 4
EDGE = 5
NEAR = 16
NEG = -float(2.0 ** 100)
M_INIT = -1e30
LOG2E = 1.4426950408889634


def _cparams(sem):
    return pltpu.CompilerParams(dimension_semantics=sem, vmem_limit_bytes=VMEM_LIMIT)


def _nt(a, b):
    return lax.dot_general(a, b, (((1,), (1,)), ((), ())), preferred_element_type=F32)


def _tn(a, b):
    return lax.dot_general(a, b, (((0,), (0,)), ((), ())), preferred_element_type=F32)


def _dot(a, b):
    return jnp.dot(a, b, preferred_element_type=F32)


def _silu(x):
    return x * jax.nn.sigmoid(x)


def _inproj_kernel(x_ref, g_ref, w_ref, z_ref):
    x = x_ref[...]
    ms = jnp.mean(x * x, axis=-1, keepdims=True)
    xn = (x * lax.rsqrt(ms + EPS) * g_ref[...]).astype(BF16)
    for j in range(NP // TN_INPROJ):
        cols = slice(j * TN_INPROJ, (j + 1) * TN_INPROJ)
        z_ref[:, cols] = _dot(xn, w_ref[:, cols])


def _inproj(x2, gain, w_p):
    m = x2.shape[0]
    tm = min(TM_INPROJ, m)
    return pl.pallas_call(
        _inproj_kernel,
        out_shape=jax.ShapeDtypeStruct((m, NP), F32),
        grid=(m // tm,),
        in_specs=[pl.BlockSpec((tm, D_MODEL), lambda i: (i, 0)),
                  pl.BlockSpec((1, D_MODEL), lambda i: (0, 0)),
                  pl.BlockSpec((D_MODEL, NP), lambda i: (0, 0), pipeline_mode=pl.Buffered(1))],
        out_specs=pl.BlockSpec((tm, NP), lambda i: (i, 0)),
        compiler_params=_cparams(("parallel",)),
        name="inproj",
    )(x2, gain.reshape(1, D_MODEL), w_p)


def _pack_w_in(w_in):
    offs = np.cumsum((0,) + IN_SIZES)
    f = [w_in[..., offs[i]:offs[i + 1]] for i in range(len(IN_SIZES))]
    (rq, rk, rv, rg, pu, pg, nq, ck, cv, sk, sv, wk, wv, nbg, ng, mg) = f
    pad = lambda n: jnp.zeros(w_in.shape[:-1] + (n,), w_in.dtype)
    cols = [mg, rv, rg, rq, rk, pu, pg, nq, ng, ck, cv, sk, sv, wk, wv, nbg, pad(LANES - 3 * NSA_HEADS), pad(NP - C_NBG - LANES)]
    return jnp.concatenate(cols, axis=-1).astype(BF16)


def _retention_kernel(q_ref, k_ref, v_ref, g_ref, cos_ref, sin_ref, dm_ref, cross_ref, tail_ref, s0_ref,
                      ga_ref, s_ref, qr_scr, kt_scr, o_scr, *, nseq, chunk, decay):
    rows = nseq * chunk

    @pl.when(pl.program_id(1) == 0)
    def _():
        s_ref[...] = s0_ref[...]

    cos = cos_ref[...]
    sin = sin_ref[...]
    lane = lax.broadcasted_iota(I32, (rows, RET_DK), 1)
    even = (lane % 2) == 0

    def rot(x):
        sw = jnp.where(even, pltpu.roll(x, RET_DK - 1, 1), pltpu.roll(x, 1, 1))
        return x * cos + sw * sin

    for h in range(RET_HEADS):
        qr = rot(q_ref[:, h * RET_DK:(h + 1) * RET_DK])
        kr = rot(k_ref[:, h * RET_DK:(h + 1) * RET_DK]) * (RET_DK ** -0.5)
        att = _nt(qr.astype(BF16), kr.astype(BF16)) * dm_ref[h]
        o_scr[:, h * RET_DV:(h + 1) * RET_DV] = _dot(att.astype(BF16), v_ref[:, h * RET_DV:(h + 1) * RET_DV].astype(BF16))
        qr_scr[h] = qr
        kt_scr[h] = kr * tail_ref[h]

    def per_seq(j, carry):
        r0 = pl.multiple_of(j * chunk, chunk)
        for h in range(RET_HEADS):
            s = s_ref[j, h]
            cols = slice(h * RET_DV, (h + 1) * RET_DV)
            inter = _dot(qr_scr[h, pl.ds(r0, chunk), :].astype(BF16), s.astype(BF16)) * cross_ref[h]
            o_scr[pl.ds(r0, chunk), cols] = o_scr[pl.ds(r0, chunk), cols] + inter
            upd = _tn(kt_scr[h, pl.ds(r0, chunk), :].astype(BF16), v_ref[pl.ds(r0, chunk), cols].astype(BF16))
            s_ref[j, h] = s * decay[h] + upd
        return carry

    lax.fori_loop(0, nseq, per_seq, 0)

    for h in range(RET_HEADS):
        cols = slice(h * RET_DV, (h + 1) * RET_DV)
        o = o_scr[:, cols]
        o = o * lax.rsqrt(jnp.mean(o * o, axis=-1, keepdims=True) + EPS)
        ga_ref[:, cols] = (_silu(g_ref[:, cols]) * o).astype(BF16)


def _retention(z, s0, *, n, t, pos0, chunk, nseq, layer=None, s_buf=None):
    nc = t // chunk
    rows = nseq * chunk
    assert t % chunk == 0 and n % nseq == 0 and (nseq == 1 or nc == 1)
    h = np.arange(RET_HEADS, dtype=np.float64)
    log_g = np.log1p(-np.exp2(-5.0 - h))
    idx = np.arange(chunk, dtype=np.float64)
    rel = idx[:, None] - idx[None, :]
    dmask = np.where(rel[None] >= 0, np.exp(np.maximum(rel, 0.0)[None] * log_g[:, None, None]), 0.0)
    dmask = np.stack([np.kron(np.eye(nseq), dmask[i]) for i in range(RET_HEADS)])
    cross = np.exp((idx + 1.0)[None, :] * log_g[:, None])
    tail = np.exp((chunk - 1.0 - idx)[None, :] * log_g[:, None])
    decay = tuple(float(x) for x in np.exp(chunk * log_g))
    cross_b = np.broadcast_to(cross[:, :, None], (RET_HEADS, chunk, RET_DV))
    tail_b = np.broadcast_to(np.tile(tail, (1, nseq))[:, :, None], (RET_HEADS, rows, RET_DK))
    pos = pos0 + np.arange(t, dtype=np.float64)
    inv = ROPE_BASE ** (-np.arange(0, RET_DK, 2, dtype=np.float64) / RET_DK)
    ang = pos[:, None] * inv[None, :]
    cos = np.repeat(np.cos(ang), 2, axis=1)
    sgn = np.tile(np.array([-1.0, 1.0]), RET_DK // 2)
    sin = np.repeat(np.sin(ang), 2, axis=1) * sgn[None, :]
    if nseq > 1:
        cos, sin = np.tile(cos, (nseq, 1)), np.tile(sin, (nseq, 1))
    c = lambda a: jnp.asarray(a, F32)
    kern = functools.partial(_retention_kernel, nseq=nseq, chunk=chunk, decay=decay)
    row_blk = lambda i, j: i * nc + j
    if layer is None:
        s_spec = pl.BlockSpec((nseq, RET_HEADS, RET_DK, RET_DV), lambda i, j: (i, 0, 0, 0))
    else:
        s_spec = pl.BlockSpec((None, nseq, RET_HEADS, RET_DK, RET_DV), lambda i, j: (layer, i, 0, 0, 0))
    extra, extra_specs, alias = (), [], {}
    if s_buf is not None:
        extra, extra_specs, alias = (s_buf,), [pl.BlockSpec(memory_space=pl.ANY)], {10: 1}
        kern = lambda *refs, _k=kern: _k(*refs[:10], *refs[11:])
    ga, s_new = pl.pallas_call(
        kern,
        out_shape=(jax.ShapeDtypeStruct((n * t, RET_V), BF16), jax.ShapeDtypeStruct(s0.shape, F32)),
        grid=(n // nseq, nc),
        input_output_aliases=alias,
        in_specs=[pl.BlockSpec((rows, RET_QK), lambda i, j: (row_blk(i, j), C_RQ // RET_QK)),
                  pl.BlockSpec((rows, RET_QK), lambda i, j: (row_blk(i, j), C_RK // RET_QK)),
                  pl.BlockSpec((rows, RET_V), lambda i, j: (row_blk(i, j), C_RV // RET_V)),
                  pl.BlockSpec((rows, RET_V), lambda i, j: (row_blk(i, j), C_RG // RET_V)),
                  pl.BlockSpec((rows, RET_DK), lambda i, j: (j, 0)),
                  pl.BlockSpec((rows, RET_DK), lambda i, j: (j, 0)),
                  pl.BlockSpec((RET_HEADS, rows, rows), lambda i, j: (0, 0, 0)),
                  pl.BlockSpec((RET_HEADS, chunk, RET_DV), lambda i, j: (0, 0, 0)),
                  pl.BlockSpec((RET_HEADS, rows, RET_DK), lambda i, j: (0, 0, 0)),
                  s_spec] + extra_specs,
        out_specs=(pl.BlockSpec((rows, RET_V), lambda i, j: (row_blk(i, j), 0)), s_spec),
        scratch_shapes=[pltpu.VMEM((RET_HEADS, rows, RET_DK), F32),
                        pltpu.VMEM((RET_HEADS, rows, RET_DK), F32),
                        pltpu.VMEM((rows, RET_V), F32)],
        compiler_params=_cparams(("parallel", "arbitrary")),
        name="retention",
    )(z, z, z, z, c(cos), c(sin), c(dmask), c(cross_b), c(tail_b), s0, *extra)
    return ga, s_new


def _pool_kernel(u_ref, pg_ref, halo_ref, w_ref, sc_ref, gb_ref, ext_scr, *, nb, tt, pos0, nbt):
    hal = 16
    ext_scr[:, 0:hal, :] = halo_ref[...]
    ext_scr[:, hal:hal + tt, :] = u_ref[...]
    pos_base = pos0 + (pl.program_id(0) * nb % nbt) * tt
    p1 = lax.broadcasted_iota(I32, (nb, tt, POOL_GDIM), 1) + pos_base + 1
    for g, w in enumerate(POOL_WINDOWS):
        cols = slice(g * POOL_GDIM, (g + 1) * POOL_GDIM)
        acc = ext_scr[:, hal:hal + tt, cols]
        for j in range(1, w):
            acc = acc + ext_scr[:, hal - j:hal - j + tt, cols]
        cnt = jnp.minimum(p1, w).astype(F32)
        d = acc / cnt - u_ref[:, :, cols]
        y = _dot(d.reshape(nb * tt, POOL_GDIM).astype(BF16), w_ref[g])
        y = y * sc_ref[:, cols]
        gate = _silu(pg_ref[:, :, cols]).reshape(nb * tt, POOL_GDIM)
        gb_ref[:, cols] = (gate * y).astype(BF16)


def _pool(z, halo, w_pool_b, scale, *, n, t, pos0, tt, nb):
    nbt = t // tt
    g_tot = n * nbt
    assert g_tot % nb == 0 and (nb == 1 or nbt == 1)
    z3 = z.reshape(g_tot, tt, NP)
    kern = functools.partial(_pool_kernel, nb=nb, tt=tt, pos0=pos0, nbt=nbt)
    return pl.pallas_call(
        kern,
        out_shape=jax.ShapeDtypeStruct((n * t, POOL_WIDTH), BF16),
        grid=(g_tot // nb,),
        in_specs=[pl.BlockSpec((nb, tt, POOL_WIDTH), lambda i: (i, 0, C_PU // POOL_WIDTH)),
                  pl.BlockSpec((nb, tt, POOL_WIDTH), lambda i: (i, 0, C_PG // POOL_WIDTH)),
                  pl.BlockSpec((nb, 16, POOL_WIDTH), lambda i: (i, 0, 0)),
                  pl.BlockSpec((POOL_GROUPS, POOL_GDIM, POOL_GDIM), lambda i: (0, 0, 0)),
                  pl.BlockSpec((1, POOL_WIDTH), lambda i: (0, 0))],
        out_specs=pl.BlockSpec((nb * tt, POOL_WIDTH), lambda i: (i, 0)),
        scratch_shapes=[pltpu.VMEM((nb, 16 + tt, POOL_WIDTH), F32)],
        compiler_params=_cparams(("parallel",)),
        name="pool",
    )(z3, z3, halo, w_pool_b, scale.reshape(1, POOL_WIDTH))


def _t5_bucket(dist):
    n = jnp.maximum(dist, 0)
    exact = N_BUCKETS // 2
    nf = jnp.maximum(n, 1).astype(F32)
    large = exact + (jnp.log(nf / exact) / math.log(MAX_DIST / exact) * (N_BUCKETS - exact)).astype(I32)
    large = jnp.minimum(large, N_BUCKETS - 1)
    return jnp.where(n < exact, n, large)


def _bias_table(rel_bias):
    rb = rel_bias.astype(F32)
    tab = rb[_t5_bucket(jnp.arange(2 * KT, dtype=I32))]
    return tab - rb[N_BUCKETS - 1][None, :]


def _toeplitz_T(tab, delta, rows_q, window=None):
    assert rows_q == KT
    d = delta + np.arange(-(KT - 1), KT)
    valid = d >= 0
    if window is not None:
        valid &= d < window
    near = valid & (d < 2 * KT)
    lo = int(np.clip(d[0], 0, 2 * KT))
    hi = int(np.clip(d[-1] + 1, 0, 2 * KT))
    seg = tab[lo:hi]
    r = jnp.zeros((2 * KT - 1, NSA_HEADS), F32)
    if hi > lo:
        r = lax.dynamic_update_slice(r, seg, (int(np.nonzero(d == lo)[0][0]), 0))
    r = jnp.where(jnp.asarray(near)[:, None], r, 0.0)
    r = jnp.where(jnp.asarray(valid)[:, None], r, NEG)
    rp = jnp.concatenate([r, jnp.zeros((1, NSA_HEADS), F32)], axis=0).T
    skew = jnp.broadcast_to(rp[:, None, :], (NSA_HEADS, KT, 2 * KT)).reshape(NSA_HEADS, KT * 2 * KT)
    skew = skew[:, :KT * (2 * KT - 1)].reshape(NSA_HEADS, KT, 2 * KT - 1)
    vals = skew[:, :, KT - 1:]
    return vals.reshape(NSA_KV_HEADS, NSA_GROUP, KT, rows_q).transpose(0, 2, 1, 3).reshape(NSA_KV_HEADS, KT, NSA_GROUP * rows_q)


def _cmp_rowbias(tab, pos, per_row_head):
    b = np.asarray(pos) % L_CMP
    cols = []
    for m in range(-1, 4):
        if m < 0:
            v = jnp.where(jnp.asarray(b == L_CMP - 1)[:, None], tab[0][None, :], NEG)
        else:
            v = tab[jnp.asarray(L_CMP * m + b + 1)]
        cols.append(v)
    r = jnp.stack(cols, axis=1)
    if per_row_head is None:
        r = r.reshape(r.shape[0], 5 * NSA_HEADS)
        return jnp.pad(r, ((0, 0), (0, LANES - 5 * NSA_HEADS)))
    r = r[jnp.arange(r.shape[0]), :, jnp.asarray(per_row_head)]
    return jnp.pad(r, ((0, 0), (0, LANES - 5)))


def _cmp_bias(dist, col):
    b = jnp.where(dist <= 4 * L_CMP, col(3), 0.0)
    b = jnp.where(dist <= 3 * L_CMP, col(2), b)
    b = jnp.where(dist <= 2 * L_CMP, col(1), b)
    b = jnp.where(dist <= L_CMP, col(0), b)
    return jnp.where(dist <= 0, col(-1), b)


def _select_mask(imp, sblk, qpos, k_top, axis):
    forced = (sblk == 0) | (sblk == qpos // L_SEL)
    avail = sblk * L_SEL <= qpos
    w = jnp.where(forced, FORCE_SCORE, imp)
    w = jnp.where(avail, w, -jnp.inf)
    idx = lax.broadcasted_iota(I32, w.shape, axis).astype(F32)
    for _ in range(k_top):
        mx = jnp.max(w, axis=axis, keepdims=True)
        first = jnp.min(jnp.where(w == mx, idx, float(4 * LANES)), axis=axis, keepdims=True)
        w = jnp.where(idx == first, -jnp.inf, w)
    return jnp.where(avail & (w == -jnp.inf), 0.0, NEG)


def _cmpkv_kernel(kv_ref, pe_ref, wk_ref, wv_ref, kc_ref, vc_ref):
    x = kv_ref[...]
    nblk = x.shape[0] // L_CMP
    cm = x.reshape(nblk, L_CMP, 2 * NSA_KV).sum(axis=1) * (1.0 / L_CMP)
    cm = cm + jnp.mean(pe_ref[...], axis=0, keepdims=True)
    kc_ref[...] = _dot(cm[:, :NSA_KV].astype(BF16), wk_ref[...])
    vc_ref[...] = _dot(cm[:, NSA_KV:].astype(BF16), wv_ref[...])


def _cmpkv(z, pe2, bd_ck, bd_cv, *, n, t):
    nblk = t // L_CMP
    z3 = z.reshape(n, t, NP)
    return pl.pallas_call(
        _cmpkv_kernel,
        out_shape=(jax.ShapeDtypeStruct((n, nblk, NSA_KV), F32), jax.ShapeDtypeStruct((n, nblk, NSA_KV), F32)),
        grid=(n,),
        in_specs=[pl.BlockSpec((None, nblk * L_CMP, 2 * NSA_KV), lambda i: (i, 0, C_KV // (2 * NSA_KV))),
                  pl.BlockSpec((L_CMP, 2 * NSA_KV), lambda i: (0, 0)),
                  pl.BlockSpec((NSA_KV, NSA_KV), lambda i: (0, 0)),
                  pl.BlockSpec((NSA_KV, NSA_KV), lambda i: (0, 0))],
        out_specs=(pl.BlockSpec((None, nblk, NSA_KV), lambda i: (i, 0, 0)),
                   pl.BlockSpec((None, nblk, NSA_KV), lambda i: (i, 0, 0))),
        compiler_params=_cparams(("parallel",)),
        name="cmp_kv",
    )(z3, pe2, bd_ck, bd_cv)


def _attn_store(o_ref, g, ot):
    for k in range(NSA_GROUP // 2):
        c0 = (g * NSA_GROUP + 2 * k) * NSA_DH
        o_ref[:, c0:c0 + 2 * NSA_DH] = jnp.concatenate(
            [ot[:, 2 * k * KT:(2 * k + 1) * KT], ot[:, (2 * k + 1) * KT:(2 * k + 2) * KT]], axis=0).T


def _cmpsel_kernel(q_ref, kc_ref, vct_ref, rbt_ref, cpos_ref, sblk_ref, oc_ref, mn_ref, s_scr, *, tq, k_top):
    gq = NSA_GROUP * tq
    qs = pl.program_id(1) * tq
    qpos = lax.broadcasted_iota(I32, (1, gq), 1) % tq + qs
    valid = qpos - cpos_ref[...] >= 0
    kc = kc_ref[...].astype(BF16)
    vct = vct_ref[...].astype(BF16)
    w0 = jnp.clip((2 * pl.program_id(1) - 2) // 8 * 8, 0, LANES - NEAR)
    imps = []
    for g in range(NSA_KV_HEADS):
        qg = q_ref[:, g * NSA_GROUP * NSA_DH:(g + 1) * NSA_GROUP * NSA_DH] * (NSA_DH ** -0.5)
        qt = jnp.concatenate([qg[:, k * LANES:(k + 1) * LANES].T for k in range(NSA_GROUP // 2)], axis=0)
        qt = jnp.concatenate([qt[r * NSA_DH:(r + 1) * NSA_DH] for r in range(NSA_GROUP)], axis=1).astype(BF16)
        s_scr[...] = _dot(kc[:, g * NSA_DH:(g + 1) * NSA_DH], qt)
        rbt = rbt_ref[g]
        for half in range(2):
            rows_w = pl.ds(pl.multiple_of(half * LANES + w0, 8), NEAR)
            dw = qpos - cpos_ref[rows_w, :]
            s_scr[rows_w, :] = s_scr[rows_w, :] + _cmp_bias(dw, lambda m: rbt[m + 1:m + 2, :])
        s = jnp.where(valid, s_scr[...], M_INIT)
        mx = jnp.max(s, axis=0, keepdims=True)
        e = jnp.where(valid, jnp.exp(s - mx), 0.0)
        p = e / jnp.maximum(jnp.sum(e, axis=0, keepdims=True), 1e-30)
        _attn_store(oc_ref, g, _dot(vct[g * NSA_DH:(g + 1) * NSA_DH, :], p.astype(BF16)))
        imp = p[:, 0:tq]
        for r in range(1, NSA_GROUP):
            imp = imp + p[:, r * tq:(r + 1) * tq]
        imps.append(imp[:LANES] + imp[LANES:])
    mn = _select_mask(jnp.concatenate(imps, axis=1), sblk_ref[...], qpos[:, 0:NSA_KV_HEADS * tq], k_top, 0).astype(BF16)
    for g in range(NSA_KV_HEADS):
        mn_ref[g] = mn[:, g * tq:(g + 1) * tq]


def _cmpsel(z, kcp, vctp, rowbt, cpos, sblk, *, n, t, k_top):
    tq = KT
    assert tq == 4 * L_CMP and 4 * L_CMP == KT
    nqt = t // tq
    gq = NSA_GROUP * tq
    kern = functools.partial(_cmpsel_kernel, tq=tq, k_top=k_top)
    return pl.pallas_call(
        kern,
        out_shape=(jax.ShapeDtypeStruct((n * t, NSA_WIDTH), F32),
                   jax.ShapeDtypeStruct((n, NSA_KV_HEADS, LANES, t), BF16)),
        grid=(n, nqt),
        in_specs=[pl.BlockSpec((tq, NSA_WIDTH), lambda b, i: (b * nqt + i, C_NQ // NSA_WIDTH)),
                  pl.BlockSpec((None, 2 * LANES, NSA_KV), lambda b, i: (b, 0, 0)),
                  pl.BlockSpec((None, NSA_KV, 2 * LANES), lambda b, i: (b, 0, 0)),
                  pl.BlockSpec((NSA_KV_HEADS, 8, gq), lambda b, i: (0, 0, 0)),
                  pl.BlockSpec((2 * LANES, gq), lambda b, i: (0, 0)),
                  pl.BlockSpec((LANES, NSA_KV_HEADS * tq), lambda b, i: (0, 0))],
        out_specs=(pl.BlockSpec((tq, NSA_WIDTH), lambda b, i: (b * nqt + i, 0)),
                   pl.BlockSpec((None, NSA_KV_HEADS, LANES, tq), lambda b, i: (b, 0, 0, i))),
        scratch_shapes=[pltpu.VMEM((2 * LANES, gq), F32)],
        compiler_params=_cparams(("parallel", "parallel")),
        name="cmp_select",
    )(z, kcp, vctp, rowbt, cpos, sblk)


def _kvpack_kernel(z_ref, oh_ref, *refs, nqt):
    ks_ref, vs_ref, vs4_ref, kw_ref, vw_ref, ct_ref, st_ref, wt_ref = refs[-8:]
    i = pl.program_id(1)
    real = (i >= KPAD) & (i < KPAD + nqt)
    kv = z_ref[...]
    kvt = [kv[:, j * NSA_KV:(j + 1) * NSA_KV].T for j in range(6)]
    for o_ref, j in ((ct_ref, 0), (st_ref, 2), (wt_ref, 4)):
        o_ref[0:NSA_KV, :] = kvt[j]
        o_ref[NSA_KV:2 * NSA_KV, :] = kvt[j + 1]

    @pl.when(real)
    def _():
        for k_ref, v_refs, j, lead in ((ks_ref, (vs_ref, vs4_ref), 2, oh_ref[...]),
                                       (kw_ref, (vw_ref,), 4, jnp.zeros((KT, LANES), BF16))):
            vt = kvt[j + 1].astype(BF16)
            for g in range(NSA_KV_HEADS):
                k_ref[g, :, 0:LANES] = lead
                k_ref[g, :, LANES:LANES + NSA_DH] = kv[:, j * NSA_KV + g * NSA_DH:j * NSA_KV + (g + 1) * NSA_DH].astype(BF16)
                for v_ref in v_refs:
                    v_ref[g] = vt[g * NSA_DH:(g + 1) * NSA_DH, :]

    @pl.when(jnp.logical_not(real))
    def _():
        for ref in (ks_ref, vs_ref, vs4_ref, kw_ref, vw_ref):
            ref[...] = jnp.zeros(ref.shape, BF16)


def _kvpack(z, onehot, *, n, t, layer, depth, bufs):
    nqt = t // KT
    ntile = nqt + 2 * KPAD
    assert ntile % CHUNK == 0
    tile = lambda i: jnp.clip(i - KPAD, 0, nqt - 1)
    kshape = jax.ShapeDtypeStruct((n, NSA_KV_HEADS, ntile, KT, LANES + NSA_DH), BF16)
    vshape = jax.ShapeDtypeStruct((n, NSA_KV_HEADS, ntile, NSA_DH, KT), BF16)
    v4shape = jax.ShapeDtypeStruct((n, NSA_KV_HEADS, ntile // CHUNK, NSA_DH, CHUNK * KT), BF16)
    rshape = jax.ShapeDtypeStruct((depth, n, 2 * NSA_KV, t), F32)
    kspec = pl.BlockSpec((None, NSA_KV_HEADS, None, KT, LANES + NSA_DH), lambda b, i: (b, 0, i, 0, 0))
    vspec = pl.BlockSpec((None, NSA_KV_HEADS, None, NSA_DH, KT), lambda b, i: (b, 0, i, 0, 0))
    v4spec = pl.BlockSpec((None, NSA_KV_HEADS, None, NSA_DH, KT), lambda b, i: (b, 0, i // CHUNK, 0, i % CHUNK))
    rspec = pl.BlockSpec((None, None, 2 * NSA_KV, KT), lambda b, i: (layer, b, 0, tile(i)))
    extra, extra_specs, alias = (), [], {}
    if bufs is not None:
        extra, extra_specs, alias = tuple(bufs), [pl.BlockSpec(memory_space=pl.ANY)] * 3, {2: 5, 3: 6, 4: 7}
    return pl.pallas_call(
        functools.partial(_kvpack_kernel, nqt=nqt),
        out_shape=(kshape, vshape, v4shape, kshape, vshape, rshape, rshape, rshape),
        grid=(n, ntile),
        in_specs=[pl.BlockSpec((KT, 1024), lambda b, i: (b * nqt + tile(i), C_KV // 1024)),
                  pl.BlockSpec((KT, LANES), lambda b, i: (tile(i), 0))] + extra_specs,
        out_specs=(kspec, vspec, v4spec, kspec, vspec, rspec, rspec, rspec),
        input_output_aliases=alias,
        compiler_params=_cparams(("parallel", "arbitrary")),
        name="kv_pack",
    )(z, onehot, *extra)


def _qk_chunk(k_ref, t0, ntile, qpt_ref):
    k = k_ref[pl.ds(t0, ntile)].reshape(ntile * KT, LANES + NSA_DH)
    return _dot(k, qpt_ref[...])


def _online_softmax(st, pv, m, l, acc):
    m_new = jnp.maximum(m, jnp.max(st, axis=0, keepdims=True))
    alpha = jnp.exp2(m - m_new)
    p = jnp.exp2(st - m_new)
    l_new = alpha * l + jnp.sum(p, axis=0, keepdims=True)
    return m_new, l_new, alpha * acc + pv(p.astype(BF16))


def _pv_tiles(v_ref, t0, ntile):
    def pv(pb):
        out = _dot(v_ref[t0], pb[0:KT])
        for j in range(1, ntile):
            out = out + _dot(v_ref[t0 + j], pb[j * KT:(j + 1) * KT])
        return out
    return pv


def _selwin_kernel(q_ref, mn_ref, ks_ref, vs_ref, vs4_ref, kw_ref, vw_ref, asel_ref, awin_ref, os_ref, ow_ref,
                   qpt_scr, sta_scr, stb_scr, m_scr, l_scr, acc_scr):
    qt = pl.program_id(1)
    heads = range(NSA_KV_HEADS)
    gq = NSA_GROUP * KT
    for g in heads:
        mn = mn_ref[g]
        for k in range(NSA_GROUP // 2):
            c0 = (g * NSA_GROUP + 2 * k) * NSA_DH
            qt2 = (q_ref[:, c0:c0 + 2 * NSA_DH] * (NSA_DH ** -0.5 * LOG2E)).T.astype(BF16)
            for u in range(2):
                r = 2 * k + u
                qpt_scr[g, 0:LANES, r * KT:(r + 1) * KT] = mn
                qpt_scr[g, LANES:LANES + NSA_DH, r * KT:(r + 1) * KT] = qt2[u * NSA_DH:(u + 1) * NSA_DH]
        m_scr[g] = jnp.full((1, gq), M_INIT, F32)
        l_scr[g] = jnp.zeros((1, gq), F32)
        acc_scr[g] = jnp.zeros((NSA_DH, gq), F32)

    nplain = jnp.maximum(qt - 1, 0) // CHUNK

    def scores(st_scr, c):
        for g in heads:
            st_scr[g] = _qk_chunk(ks_ref.at[g], KPAD + CHUNK * c, CHUNK, qpt_scr.at[g])

    def update(st_scr, c):
        for g in heads:
            m, l, acc = _online_softmax(st_scr[g], lambda pb, g=g: _dot(vs4_ref[g, KPAD // CHUNK + c], pb),
                                        m_scr[g], l_scr[g], acc_scr[g])
            m_scr[g] = m
            l_scr[g] = l
            acc_scr[g] = acc

    @pl.when(nplain > 0)
    def _():
        scores(sta_scr, 0)

    def pair(i, carry):
        c0 = 2 * i
        scores(stb_scr, c0 + 1)
        update(sta_scr, c0)
        scores(sta_scr, jnp.minimum(c0 + 2, nplain - 1))
        update(stb_scr, c0 + 1)
        return carry

    lax.fori_loop(0, nplain // 2, pair, 0)

    @pl.when(nplain % 2 == 1)
    def _():
        update(sta_scr, nplain - 1)

    e0 = CHUNK * nplain
    off = pl.multiple_of((EDGE - (qt - e0)) * KT, KT)
    for g in heads:
        st = _qk_chunk(ks_ref.at[g], KPAD + e0, EDGE, qpt_scr.at[g]) + asel_ref[g, pl.ds(off, EDGE * KT), :]
        _, l, acc = _online_softmax(st, _pv_tiles(vs_ref.at[g], KPAD + e0, EDGE), m_scr[g], l_scr[g], acc_scr[g])
        _attn_store(os_ref, g, acc / l)

    for g in heads:
        st = _qk_chunk(kw_ref.at[g], qt, EDGE, qpt_scr.at[g]) + awin_ref[g]
        rows = []
        for j in range(EDGE):
            blk = st[j * KT:(j + 1) * KT]
            if j < EDGE - 1:
                blk = blk + jnp.where(qt >= EDGE - 1 - j, 0.0, NEG)
            rows.append(blk)
        st = jnp.concatenate(rows, axis=0)
        _, l, acc = _online_softmax(st, _pv_tiles(vw_ref.at[g], qt, EDGE), jnp.full((1, gq), M_INIT, F32),
                                    jnp.zeros((1, gq), F32), jnp.zeros((NSA_DH, gq), F32))
        _attn_store(ow_ref, g, acc / l)


def _selwin(z, mnt, ks, vs, vs4, kw, vw, asel, awin, *, n, t):
    nqt = t // KT
    ntile = nqt + 2 * KPAD
    gq = NSA_GROUP * KT
    hk = NSA_KV_HEADS
    once = pl.Buffered(1)
    kspec = pl.BlockSpec((None, hk, ntile, KT, LANES + NSA_DH), lambda b, i: (b, 0, 0, 0, 0), pipeline_mode=once)
    vspec = pl.BlockSpec((None, hk, ntile, NSA_DH, KT), lambda b, i: (b, 0, 0, 0, 0), pipeline_mode=once)
    v4spec = pl.BlockSpec((None, hk, ntile // CHUNK, NSA_DH, CHUNK * KT), lambda b, i: (b, 0, 0, 0, 0), pipeline_mode=once)
    ospec = pl.BlockSpec((KT, NSA_WIDTH), lambda b, i: (b * nqt + i, 0))
    return pl.pallas_call(
        _selwin_kernel,
        out_shape=(jax.ShapeDtypeStruct((n * t, NSA_WIDTH), F32), jax.ShapeDtypeStruct((n * t, NSA_WIDTH), F32)),
        grid=(n, nqt),
        in_specs=[pl.BlockSpec((KT, NSA_WIDTH), lambda b, i: (b * nqt + i, C_NQ // NSA_WIDTH)),
                  pl.BlockSpec((None, hk, LANES, KT), lambda b, i: (b, 0, 0, i)),
                  kspec, vspec, v4spec, kspec, vspec,
                  pl.BlockSpec((hk, 2 * EDGE * KT, gq), lambda b, i: (0, 0, 0), pipeline_mode=once),
                  pl.BlockSpec((hk, EDGE * KT, gq), lambda b, i: (0, 0, 0), pipeline_mode=once)],
        out_specs=(ospec, ospec),
        scratch_shapes=[pltpu.VMEM((hk, LANES + NSA_DH, gq), BF16),
                        pltpu.VMEM((hk, CHUNK * KT, gq), F32),
                        pltpu.VMEM((hk, CHUNK * KT, gq), F32),
                        pltpu.VMEM((hk, 1, gq), F32),
                        pltpu.VMEM((hk, 1, gq), F32),
                        pltpu.VMEM((hk, NSA_DH, gq), F32)],
        compiler_params=_cparams(("parallel", "arbitrary")),
        name="sel_win",
    )(z, mnt, ks, vs, vs4, kw, vw, asel, awin)


def _nsa_prompt(z, pe2, bd_ck, bd_cv, tab, *, n, t, layer, depth, row_bufs):
    assert t % KT == 0 and WINDOW == (EDGE - 1) * KT and KT % L_SEL == 0 and EDGE <= KPAD + 1 and EDGE <= CHUNK + 1
    nblk = t // L_CMP
    n_sel = -(-t // L_SEL)
    assert nblk <= 2 * LANES and n_sel <= LANES and nblk % 2 == 0
    kc, vc = _cmpkv(z, pe2, bd_ck, bd_cv, n=n, t=t)
    half = nblk // 2

    def perm(a):
        z0 = jnp.zeros((n, LANES - half, NSA_KV), F32)
        return jnp.concatenate([a[:, 0::2], z0, a[:, 1::2], z0], axis=1)

    slot = np.arange(LANES)
    big = 1 << 20
    cpos = np.concatenate([np.where(slot < half, (2 * slot) * L_CMP + L_CMP - 1, big),
                           np.where(slot < half, (2 * slot + 1) * L_CMP + L_CMP - 1, big)])
    cpos = np.broadcast_to(cpos[:, None], (2 * LANES, NSA_GROUP * KT))
    sblk = np.broadcast_to(np.where(slot < n_sel, slot, big)[:, None], (LANES, NSA_KV_HEADS * KT))
    rowb = _cmp_rowbias(tab, np.arange(KT), None)[:, :5 * NSA_HEADS].reshape(KT, 5, NSA_KV_HEADS, NSA_GROUP)
    rowbt = jnp.pad(rowb.transpose(2, 1, 3, 0).reshape(NSA_KV_HEADS, 5, NSA_GROUP * KT), ((0, 0), (0, 3), (0, 0)))
    o_cmp, mnt = _cmpsel(z, perm(kc), perm(vc).transpose(0, 2, 1), rowbt, jnp.asarray(cpos, I32), jnp.asarray(sblk, I32),
                         n=n, t=t, k_top=min(N_SEL, n_sel))

    onehot = jnp.asarray(np.arange(t)[:, None] // L_SEL == np.arange(LANES)[None, :], BF16)
    ks, vs, vs4, kw, vw, *row_bufs = _kvpack(z, onehot, n=n, t=t, layer=layer, depth=depth, bufs=row_bufs)
    a0 = _toeplitz_T(tab, 0, KT) * LOG2E
    a1 = _toeplitz_T(tab, KT, KT) * LOG2E
    aw = _toeplitz_T(tab, WINDOW, KT, window=WINDOW) * LOG2E
    zeros = jnp.zeros_like(a0)
    asel = jnp.concatenate([zeros] * (EDGE - 1) + [a1, a0] + [jnp.full_like(a0, NEG)] * (EDGE - 1), axis=1)
    awin = jnp.concatenate([aw] + [zeros] * (EDGE - 3) + [a1, a0], axis=1)
    o_sel, o_win = _selwin(z, mnt, ks, vs, vs4, kw, vw, asel, awin, n=n, t=t)
    return o_cmp, o_sel, o_win, row_bufs


def _nsa_sample_kernel(pt_ref, *refs, npg, t, past, wbuf, k_top, aliased, ns):
    del pt_ref
    cpages = refs[0:ns * npg]
    spages = refs[ns * npg:2 * ns * npg]
    rest = refs[2 * ns * npg:]
    (q_ref, kvn_ref, cw_ref, pe_ref, wk_ref, wv_ref, rb_ref, cpos_ref, sblk_ref, oht_ref, asel_ref, awin_ref) = rest[:12]
    (oc_ref, os_ref, ow_ref, wout_ref, kselt, vselt, kwint, vwint, cm_scr, kcv_scr, qw_scr, qp_scr) = rest[12 + aliased:]
    rows = NSA_HEADS * t
    half = NSA_GROUP * t
    bpp = PAGE_SIZE // L_CMP
    nblk = npg * bpp
    keep = wout_ref.shape[2]

    @pl.when(pl.program_id(0) == 0)
    def _():
        for s in range(ns):
            kselt[s, 0:LANES, :] = oht_ref[...]
        kcv_scr[...] = jnp.zeros(kcv_scr.shape, F32)
        qw_scr[...] = jnp.zeros(qw_scr.shape, F32)

    def pick(x):
        row = lax.broadcasted_iota(I32, (rows, NSA_DH), 0)
        return jnp.where(row < half, x[:, :NSA_DH], x[:, NSA_DH:])

    def store(o_ref, s, x):
        for h in range(NSA_HEADS):
            o_ref[s * t:(s + 1) * t, h * NSA_DH:(h + 1) * NSA_DH] = x[h * t:(h + 1) * t, :]

    qpos = lax.broadcasted_iota(I32, (rows, 1), 0) % t + past
    dist = qpos - cpos_ref[...]
    valid = dist >= 0
    rb = rb_ref[...]
    qws, imps = [], []
    for s in range(ns):
        kvn = kvn_ref[s * t:(s + 1) * t, :]
        kvpad = jnp.concatenate([kvn[:, 0:6 * NSA_KV], jnp.zeros((KT - t, 6 * NSA_KV), F32)], axis=0)
        new_t = [kvpad[:, j * NSA_KV:(j + 1) * NSA_KV].T for j in range(2, 6)]

        for p in range(npg):
            sp, cp = spages[s * npg + p], cpages[s * npg + p]
            kselt[s, LANES:2 * LANES, p * PAGE_SIZE:(p + 1) * PAGE_SIZE] = sp[0:NSA_KV, :].astype(BF16)
            vselt[s, :, p * PAGE_SIZE:(p + 1) * PAGE_SIZE] = sp[NSA_KV:2 * NSA_KV, :].astype(BF16)
            for c in range(2):
                blk = cp[c * NSA_KV:(c + 1) * NSA_KV, :].T
                cm_scr[s, p * bpp:(p + 1) * bpp, c * NSA_KV:(c + 1) * NSA_KV] = blk.reshape(bpp, L_CMP, NSA_KV).sum(axis=1) * (1.0 / L_CMP)
        kselt[s, LANES:2 * LANES, past:past + KT] = new_t[0].astype(BF16)
        vselt[s, :, past:past + KT] = new_t[1].astype(BF16)
        cw = cw_ref[s]
        kwint[s, :, 0:wbuf] = cw[0:NSA_KV, :].astype(BF16)
        vwint[s, :, 0:wbuf] = cw[NSA_KV:2 * NSA_KV, :].astype(BF16)
        kwint[s, :, wbuf:wbuf + KT] = new_t[2].astype(BF16)
        vwint[s, :, wbuf:wbuf + KT] = new_t[3].astype(BF16)
        for c in range(2):
            wout_ref[s, c * NSA_KV:(c + 1) * NSA_KV, :] = jnp.concatenate(
                [cw[c * NSA_KV:(c + 1) * NSA_KV, wbuf + t - keep:wbuf], new_t[2 + c][:, 0:t]], axis=1)

        for h in range(NSA_HEADS):
            g = h // NSA_GROUP
            qh = q_ref[s * t:(s + 1) * t, h * NSA_DH:(h + 1) * NSA_DH] * (NSA_DH ** -0.5)
            qw_scr[s, h * t:(h + 1) * t, g * NSA_DH:(g + 1) * NSA_DH] = qh
        qw = qw_scr[s].astype(BF16)
        qws.append(qw)

        cm = cm_scr[s] + jnp.mean(pe_ref[...], axis=0, keepdims=True)
        kcv_scr[s, 0:nblk, 0:NSA_KV] = _dot(cm[:, :NSA_KV].astype(BF16), wk_ref[...])
        kcv_scr[s, 0:nblk, NSA_KV:] = _dot(cm[:, NSA_KV:].astype(BF16), wv_ref[...])
        kc = kcv_scr[s, :, 0:NSA_KV].astype(BF16)
        vc = kcv_scr[s, :, NSA_KV:].astype(BF16)
        sc = _nt(qw, kc) + _cmp_bias(dist, lambda m: rb[:, m + 1:m + 2])
        sc = jnp.where(valid, sc, M_INIT)
        mx = jnp.max(sc, axis=-1, keepdims=True)
        e = jnp.where(valid, jnp.exp(sc - mx), 0.0)
        p = e / jnp.maximum(jnp.sum(e, axis=-1, keepdims=True), 1e-30)
        store(oc_ref, s, pick(_dot(p.astype(BF16), vc)))

        for g in range(NSA_KV_HEADS):
            imp = p[g * half:g * half + t]
            for r in range(1, NSA_GROUP):
                imp = imp + p[g * half + r * t:g * half + (r + 1) * t]
            imps.append(imp + pltpu.roll(imp, LANES - 1, 1))

    imps.append(jnp.zeros((LANES - ns * NSA_KV_HEADS * t, LANES), F32))
    imp_t = jnp.concatenate(imps, axis=0).T
    qpos_l = lax.broadcasted_iota(I32, (1, LANES), 1) % t + past
    mn = _select_mask(imp_t, sblk_ref[...], qpos_l, k_top, 0).T.astype(BF16)

    def attend(qmat, kt_all, vt_all, add):
        sc = _dot(qmat, kt_all) + add
        mm = jnp.max(sc, axis=-1, keepdims=True)
        ee = jnp.exp(sc - mm)
        ll = jnp.sum(ee, axis=-1, keepdims=True)
        return pick(_nt(ee.astype(BF16), vt_all)) / ll

    for s in range(ns):
        for g in range(NSA_KV_HEADS):
            row0 = (s * NSA_KV_HEADS + g) * t
            for r in range(NSA_GROUP):
                qp_scr[s, g * half + r * t:g * half + (r + 1) * t, 0:LANES] = mn[row0:row0 + t]
        qp_scr[s, :, LANES:2 * LANES] = qws[s]
        store(os_ref, s, attend(qp_scr[s], kselt[s], vselt[s], asel_ref[...]))
        store(ow_ref, s, attend(qws[s], kwint[s], vwint[s], awin_ref[...]))


def _nsa_sample(z, cache_cmp_t, cache_sel_t, cache_win_t, page_table, pe2, bd_ck, bd_cv, tab, *, n, t, past, layer, win_buf):
    npg = page_table.shape[1]
    depth = cache_win_t.shape[0]
    assert past == npg * PAGE_SIZE and past % L_SEL == 0 and past % L_CMP == 0 and PAGE_SIZE == KT
    wbuf = cache_win_t.shape[3]
    keep = min(WINDOW, wbuf + t)
    nblk = past // L_CMP
    assert t < L_CMP and nblk <= LANES and wbuf % KT == 0 and keep > t
    length = past + t
    n_sel = -(-length // L_SEL)
    ns = 2 if n % 2 == 0 else 1
    assert 2 * (n_sel - 1) < LANES and ns * NSA_KV_HEADS * t <= LANES
    rows = NSA_HEADS * t
    lsel = past + KT
    lwin = wbuf + KT
    big = 1 << 20
    slot = np.arange(LANES)
    cpos = np.where(slot < nblk, slot * L_CMP + L_CMP - 1, big)[None, :]
    sblk = np.broadcast_to(np.where((slot % 2 == 0) & (slot // 2 < n_sel), slot // 2, big)[:, None], (LANES, LANES))
    head_of_row = np.repeat(np.arange(NSA_HEADS), t)
    tok_of_row = np.tile(np.arange(t), NSA_HEADS)
    rowb = _cmp_rowbias(tab, past + tok_of_row, head_of_row)
    kpos_sel = np.arange(lsel)
    onehot_t = jnp.asarray((kpos_sel[None, :] < length) & (2 * (kpos_sel[None, :] // L_SEL) == slot[:, None]), BF16)

    def additive(kpos, exists, window):
        d = (past + np.arange(t))[:, None] - kpos[None, :]
        valid = (d >= 0) & exists[None, :]
        if window is not None:
            valid &= d < window
        near = valid & (d < 2 * KT)
        j0 = int(np.nonzero(near.any(axis=0))[0].min()) // KT * KT
        vals = tab[jnp.asarray(np.clip(d[:, j0:], 0, 2 * KT - 1))]
        vals = jnp.where(jnp.asarray(near[:, j0:])[:, :, None], vals, 0.0)
        vals = jnp.pad(vals, ((0, 0), (j0, 0), (0, 0)))
        vals = jnp.where(jnp.asarray(valid)[:, :, None], vals, NEG)
        return vals.transpose(2, 0, 1).reshape(NSA_HEADS * t, kpos.shape[0])

    asel = additive(kpos_sel, kpos_sel < length, None)
    kidx = np.arange(lwin)
    awin = additive(past - wbuf + kidx, kidx < wbuf + t, WINDOW)

    pages = [(s, p) for s in range(ns) for p in range(npg)]
    page_spec = lambda s, p: pl.BlockSpec((None, None, 2 * NSA_KV, PAGE_SIZE),
                                          lambda i, pt, s=s, p=p: (layer, pt[ns * i + s, p], 0, 0))
    const = lambda shape: pl.BlockSpec(shape, lambda i, pt: tuple(0 for _ in shape))
    ospec = pl.BlockSpec((ns * t, NSA_WIDTH), lambda i, pt: (i, 0))
    aliased = win_buf is not None
    extra, extra_specs, alias = (), [], {}
    if aliased:
        extra, extra_specs, alias = (win_buf,), [pl.BlockSpec(memory_space=pl.ANY)], {1 + 2 * ns * npg + 12: 3}
    kern = functools.partial(_nsa_sample_kernel, npg=npg, t=t, past=past, wbuf=wbuf, k_top=min(N_SEL, n_sel),
                             aliased=int(aliased), ns=ns)
    grid_spec = pltpu.PrefetchScalarGridSpec(
        num_scalar_prefetch=1,
        grid=(n // ns,),
        in_specs=[page_spec(s, p) for s, p in pages] + [page_spec(s, p) for s, p in pages] + [
            pl.BlockSpec((ns * t, NSA_WIDTH), lambda i, pt: (i, C_NQ // NSA_WIDTH)),
            pl.BlockSpec((ns * t, 1024), lambda i, pt: (i, C_KV // 1024)),
            pl.BlockSpec((None, ns, 2 * NSA_KV, wbuf), lambda i, pt: (layer, i, 0, 0)),
            const((L_CMP, 2 * NSA_KV)), const((NSA_KV, NSA_KV)), const((NSA_KV, NSA_KV)),
            const((rows, LANES)), const((1, LANES)), const((LANES, LANES)),
            const((LANES, lsel)), const((rows, lsel)), const((rows, lwin))] + extra_specs,
        out_specs=(ospec, ospec, ospec, pl.BlockSpec((None, ns, 2 * NSA_KV, keep), lambda i, pt: (layer, i, 0, 0))),
        scratch_shapes=[pltpu.VMEM((ns, 2 * LANES, lsel), BF16), pltpu.VMEM((ns, LANES, lsel), BF16),
                        pltpu.VMEM((ns, LANES, lwin), BF16), pltpu.VMEM((ns, LANES, lwin), BF16),
                        pltpu.VMEM((ns, nblk, 2 * NSA_KV), F32), pltpu.VMEM((ns, LANES, 2 * NSA_KV), F32),
                        pltpu.VMEM((ns, rows, NSA_KV), F32),
                        pltpu.VMEM((ns, rows, 2 * LANES), BF16)])
    o_cmp, o_sel, o_win, win_new = pl.pallas_call(
        kern,
        grid_spec=grid_spec,
        out_shape=(jax.ShapeDtypeStruct((n * t, NSA_WIDTH), F32),) * 3
        + (jax.ShapeDtypeStruct((depth, n, 2 * NSA_KV, keep), F32),),
        input_output_aliases=alias,
        compiler_params=_cparams(("arbitrary",)),
        name="nsa_sample",
    )(page_table, *([cache_cmp_t] * (ns * npg)), *([cache_sel_t] * (ns * npg)), z, z, cache_win_t, pe2, bd_ck, bd_cv, rowb,
      jnp.asarray(cpos, I32), jnp.asarray(sblk, I32), onehot_t, asel, awin, *extra)
    return o_cmp, o_sel, o_win, win_new


def _merge_kernel(x_ref, mg_ref, ga_ref, gb_ref, oc_ref, os_ref, ow_ref, nbg_ref, ng_ref,
                  wa_ref, wb_ref, wc_ref, wo_ref, fg_ref, y_ref, *, final):
    tm = x_ref.shape[0]
    gates = jax.nn.sigmoid(nbg_ref[...])
    lane = lax.broadcasted_iota(I32, (tm, LANES), 1)
    low = lane < NSA_DH
    parts = []
    for pi in range(NSA_HEADS // 2):
        cols = slice(pi * LANES, (pi + 1) * LANES)
        acc = jnp.zeros((tm, LANES), F32)
        for b, o_ref in enumerate((oc_ref, os_ref, ow_ref)):
            c0 = b * NSA_HEADS + 2 * pi
            gt = jnp.where(low, gates[:, c0:c0 + 1], gates[:, c0 + 1:c0 + 2])
            acc = acc + gt * o_ref[:, cols]
        parts.append((_silu(ng_ref[:, cols]) * acc).astype(BF16))
    gc = jnp.concatenate(parts, axis=1)
    b_a = _dot(ga_ref[...], wa_ref[...])
    b_b = _dot(gb_ref[...], wb_ref[...])
    b_c = _dot(gc, wc_ref[...])
    mg = mg_ref[...]
    merged = (jax.nn.sigmoid(mg[:, 0:D_MODEL]) * b_a + jax.nn.sigmoid(mg[:, D_MODEL:2 * D_MODEL]) * b_b
              + jax.nn.sigmoid(mg[:, 2 * D_MODEL:]) * b_c)
    y = x_ref[...] + _dot(merged.astype(BF16), wo_ref[...])
    if final:
        y = y * lax.rsqrt(jnp.mean(y * y, axis=-1, keepdims=True) + EPS) * fg_ref[...]
    y_ref[...] = y


def _merge(x2, z, ga, gb, o_cmp, o_sel, o_win, wa, wb, wc, wo, fgain, *, final):
    m = x2.shape[0]
    tm = min(512, m)
    row = lambda w, c=0: pl.BlockSpec((tm, w), lambda i, c=c: (i, c))
    const = lambda a: pl.BlockSpec(a.shape, lambda i: (0, 0), pipeline_mode=pl.Buffered(1))
    fg = fgain.reshape(1, D_MODEL)
    return pl.pallas_call(
        functools.partial(_merge_kernel, final=final),
        out_shape=jax.ShapeDtypeStruct((m, D_MODEL), F32),
        grid=(m // tm,),
        in_specs=[row(D_MODEL), row(3 * D_MODEL, C_MG // (3 * D_MODEL)), row(RET_V), row(POOL_WIDTH),
                  row(NSA_WIDTH), row(NSA_WIDTH), row(NSA_WIDTH),
                  row(LANES, C_NBG // LANES), row(NSA_WIDTH, C_NG // NSA_WIDTH),
                  const(wa), const(wb), const(wc), const(wo), const(fg)],
        out_specs=row(D_MODEL),
        compiler_params=_cparams(("parallel",)),
        name="merge",
    )(x2, z, ga, gb, o_cmp, o_sel, o_win, z, z, wa, wb, wc, wo, fg)


def _kv_rows(z, n, t):
    kv = z[:, C_KV:C_KV + 6 * NSA_KV].reshape(n, t, 3, 2, NSA_KV_HEADS, NSA_DH)
    return kv[:, :, 0], kv[:, :, 1], kv[:, :, 2]


def _slots_minor(cache):
    d, x, slots = cache.shape[:3]
    return cache.transpose(0, 1, 3, 4, 5, 2).reshape(d, x, 2 * NSA_KV, slots)


def _slots_major(rows_t):
    d, n, _, slots = rows_t.shape
    return rows_t.reshape(d, n, 2, NSA_KV_HEADS, NSA_DH, slots).transpose(0, 1, 5, 2, 3, 4)


def kernel(x_prompt, x_sample, state_ret, state_pool, cache_win, cache_cmp, cache_sel, page_table, norm_gain, w_in, w_pool, pool_scale, pe_cmp, w_ck, w_cv, w_br_a, w_br_b, w_br_c, w_out, rel_bias, final_gain):
    b, seq, _ = x_prompt.shape
    nb, dseq, _ = x_sample.shape
    depth = w_in.shape[0]
    past = page_table.shape[1] * PAGE_SIZE
    pt = page_table.astype(I32)

    w_in_p = _pack_w_in(w_in)
    w_pool_b = w_pool.astype(BF16)
    wa, wb, wc, wo = (w.astype(BF16) for w in (w_br_a, w_br_b, w_br_c, w_out))
    eye = jnp.eye(NSA_KV_HEADS, dtype=F32)
    bd_ck = jax.vmap(lambda w: jnp.kron(eye, w))(w_ck).astype(BF16)
    bd_cv = jax.vmap(lambda w: jnp.kron(eye, w))(w_cv).astype(BF16)
    pe2 = pe_cmp.reshape(depth, L_CMP, 2 * NSA_KV)
    tab = _bias_table(rel_bias)
    cmp_t, sel_t, win_t = _slots_minor(cache_cmp), _slots_minor(cache_sel), _slots_minor(cache_win)

    ret_chunk = 256 if seq % 256 == 0 else seq
    pool_tt = min(1024, seq)
    samp_nseq = 8 if nb % 8 == 0 else 1
    samp_nb = 16 if nb % 16 == 0 else 1

    hp = x_prompt.reshape(b * seq, D_MODEL)
    hs = x_sample.reshape(nb * dseq, D_MODEL)
    zero_state = jnp.zeros((b, RET_HEADS, RET_DK, RET_DV), F32)
    ret_p, pool_p, pool_s, cmp_s, sel_s = [], [], [], [], []
    row_bufs = ret_s = win_s = None
    for l in range(depth):
        final = l == depth - 1
        z = _inproj(hp, norm_gain[l], w_in_p[l])
        ga, s_new = _retention(z, zero_state, n=b, t=seq, pos0=0, chunk=ret_chunk, nseq=1)
        u = z[:, C_PU:C_PU + POOL_WIDTH].reshape(b, seq, POOL_WIDTH)
        nbt = seq // pool_tt
        halo = u.reshape(b, nbt, pool_tt, POOL_WIDTH)[:, :, pool_tt - 16:, :]
        halo = jnp.concatenate([jnp.zeros((b, 1, 16, POOL_WIDTH), F32), halo[:, :nbt - 1]], axis=1)
        gb = _pool(z, halo.reshape(b * nbt, 16, POOL_WIDTH), w_pool_b[l], pool_scale[l], n=b, t=seq, pos0=0, tt=pool_tt, nb=1)
        o_cmp, o_sel, o_win, row_bufs = _nsa_prompt(z, pe2[l], bd_ck[l], bd_cv[l], tab, n=b, t=seq, layer=l, depth=depth,
                                                    row_bufs=row_bufs)
        hp = _merge(hp, z, ga, gb, o_cmp, o_sel, o_win, wa[l], wb[l], wc[l], wo[l], final_gain, final=final)
        ret_p.append(s_new)
        pool_p.append(u[:, seq - POOL_BUF:])

        z = _inproj(hs, norm_gain[l], w_in_p[l])
        ga, ret_s = _retention(z, state_ret, n=nb, t=dseq, pos0=past, chunk=dseq, nseq=samp_nseq, layer=l, s_buf=ret_s)
        u = z[:, C_PU:C_PU + POOL_WIDTH].reshape(nb, dseq, POOL_WIDTH)
        ext = jnp.concatenate([state_pool[l], u], axis=1)
        halo = jnp.pad(state_pool[l], ((0, 0), (16 - POOL_BUF, 0), (0, 0)))
        gb = _pool(z, halo, w_pool_b[l], pool_scale[l], n=nb, t=dseq, pos0=past, tt=dseq, nb=samp_nb)
        o_cmp, o_sel, o_win, win_s = _nsa_sample(z, cmp_t, sel_t, win_t, pt, pe2[l], bd_ck[l], bd_cv[l], tab,
                                                 n=nb, t=dseq, past=past, layer=l, win_buf=win_s)
        cmp_new, sel_new, _ = _kv_rows(z, nb, dseq)
        hs = _merge(hs, z, ga, gb, o_cmp, o_sel, o_win, wa[l], wb[l], wc[l], wo[l], final_gain, final=final)
        pool_s.append(ext[:, ext.shape[1] - POOL_BUF:])
        cmp_s.append(cmp_new)
        sel_s.append(sel_new)

    y_prompt = hp.reshape(b, seq, D_MODEL)
    y_sample = hs.reshape(nb, dseq, D_MODEL)
    cmp_p, sel_p, win_p = row_bufs
    keep_p = min(WINDOW, seq)
    st = lambda xs: jnp.stack(xs)
    return (y_prompt, y_sample, st(ret_p), ret_s, st(pool_p), st(pool_s),
            _slots_major(win_p[:, :, :, seq - keep_p:]), _slots_major(win_s),
            _slots_major(cmp_p), st(cmp_s), _slots_major(sel_p), st(sel_s))
```

```python
import functools
import math

import numpy as np
import jax
import jax.numpy as jnp
from jax import lax
from jax.experimental import pallas as pl
from jax.experimental.pallas import tpu as pltpu

F32 = jnp.float32
BF16 = jnp.bfloat16
I32 = jnp.int32

D_MODEL = 1024
PAGE_SIZE = 128
RET_HEADS, RET_DK, RET_DV = 4, 128, 256
RET_QK, RET_V = RET_HEADS * RET_DK, RET_HEADS * RET_DV
ROPE_BASE = 10000.0
POOL_WINDOWS = (2, 4, 8, 16)
POOL_GROUPS, POOL_GDIM = 4, 128
POOL_WIDTH = POOL_GROUPS * POOL_GDIM
POOL_BUF = 15
NSA_HEADS, NSA_KV_HEADS, NSA_DH = 8, 2, 64
NSA_GROUP = NSA_HEADS // NSA_KV_HEADS
NSA_WIDTH, NSA_KV = NSA_HEADS * NSA_DH, NSA_KV_HEADS * NSA_DH
L_CMP, L_SEL, N_SEL, WINDOW = 32, 64, 16, 512
FORCE_SCORE = 1e4
N_BUCKETS, MAX_DIST = 32, 128
EPS = 1e-6
IN_SIZES = (RET_QK, RET_QK, RET_V, RET_V, POOL_WIDTH, POOL_WIDTH, NSA_WIDTH,
            NSA_KV, NSA_KV, NSA_KV, NSA_KV, NSA_KV, NSA_KV, 3 * NSA_HEADS, NSA_WIDTH, 3 * D_MODEL)

LANES = 128
VMEM_LIMIT = 56 * 1024 * 1024

C_MG, C_RV, C_RG, C_RQ, C_RK = 0, 3072, 4096, 5120, 5632
C_PU, C_PG, C_NQ, C_NG, C_KV, C_NBG = 6144, 6656, 7168, 7680, 8192, 8960
NP = 9216
TM_INPROJ, TN_INPROJ = 256, 1024
KT = 128
KPAD = 4
CHUNK = 4
TPS = 2
EDGE = 5
NEAR = 16
NEG = -float(2.0 ** 100)
M_INIT = -1e30
LOG2E = 1.4426950408889634


def _cparams(sem):
    return pltpu.CompilerParams(dimension_semantics=sem, vmem_limit_bytes=VMEM_LIMIT)


def _nt(a, b):
    return lax.dot_general(a, b, (((1,), (1,)), ((), ())), preferred_element_type=F32)


def _tn(a, b):
    return lax.dot_general(a, b, (((0,), (0,)), ((), ())), preferred_element_type=F32)


def _dot(a, b):
    return jnp.dot(a, b, preferred_element_type=F32)


def _silu(x):
    return x * jax.nn.sigmoid(x)


def _inproj_kernel(x_ref, g_ref, w_ref, z_ref):
    x = x_ref[...]
    ms = jnp.mean(x * x, axis=-1, keepdims=True)
    xn = (x * lax.rsqrt(ms + EPS) * g_ref[...]).astype(BF16)
    for j in range(NP // TN_INPROJ):
        cols = slice(j * TN_INPROJ, (j + 1) * TN_INPROJ)
        z_ref[:, cols] = _dot(xn, w_ref[:, cols])


def _inproj(x2, gain, w_p):
    m = x2.shape[0]
    tm = min(TM_INPROJ, m)
    return pl.pallas_call(
        _inproj_kernel,
        out_shape=jax.ShapeDtypeStruct((m, NP), F32),
        grid=(m // tm,),
        in_specs=[pl.BlockSpec((tm, D_MODEL), lambda i: (i, 0)),
                  pl.BlockSpec((1, D_MODEL), lambda i: (0, 0)),
                  pl.BlockSpec((D_MODEL, NP), lambda i: (0, 0), pipeline_mode=pl.Buffered(1))],
        out_specs=pl.BlockSpec((tm, NP), lambda i: (i, 0)),
        compiler_params=_cparams(("parallel",)),
        name="inproj",
    )(x2, gain.reshape(1, D_MODEL), w_p)


def _pack_w_in(w_in):
    offs = np.cumsum((0,) + IN_SIZES)
    f = [w_in[..., offs[i]:offs[i + 1]] for i in range(len(IN_SIZES))]
    (rq, rk, rv, rg, pu, pg, nq, ck, cv, sk, sv, wk, wv, nbg, ng, mg) = f
    pad = lambda n: jnp.zeros(w_in.shape[:-1] + (n,), w_in.dtype)
    cols = [mg, rv, rg, rq, rk, pu, pg, nq, ng, ck, cv, sk, sv, wk, wv, nbg, pad(LANES - 3 * NSA_HEADS), pad(NP - C_NBG - LANES)]
    return jnp.concatenate(cols, axis=-1).astype(BF16)


def _retention_kernel(q_ref, k_ref, v_ref, g_ref, cos_ref, sin_ref, dm_ref, cross_ref, tail_ref, s0_ref,
                      *rest, nseq, chunk, decay):
    ga_ref, s_ref, qr_scr, kt_scr, o_scr = rest[-5:]
    rows = nseq * chunk

    @pl.when(pl.program_id(1) == 0)
    def _():
        s_ref[...] = s0_ref[...]

    cos = cos_ref[...]
    sin = sin_ref[...]
    lane = lax.broadcasted_iota(I32, (rows, RET_DK), 1)
    even = (lane % 2) == 0

    def rot(x):
        sw = jnp.where(even, pltpu.roll(x, RET_DK - 1, 1), pltpu.roll(x, 1, 1))
        return x * cos + sw * sin

    for h in range(RET_HEADS):
        qr = rot(q_ref[:, h * RET_DK:(h + 1) * RET_DK])
        kr = rot(k_ref[:, h * RET_DK:(h + 1) * RET_DK]) * (RET_DK ** -0.5)
        att = _nt(qr.astype(BF16), kr.astype(BF16)) * dm_ref[h]
        o_scr[:, h * RET_DV:(h + 1) * RET_DV] = _dot(att.astype(BF16), v_ref[:, h * RET_DV:(h + 1) * RET_DV].astype(BF16))
        qr_scr[h] = qr
        kt_scr[h] = kr * tail_ref[h]

    def per_seq(j, carry):
        r0 = pl.multiple_of(j * chunk, chunk)
        for h in range(RET_HEADS):
            s = s_ref[j, h]
            cols = slice(h * RET_DV, (h + 1) * RET_DV)
            inter = _dot(qr_scr[h, pl.ds(r0, chunk), :].astype(BF16), s.astype(BF16)) * cross_ref[h]
            o_scr[pl.ds(r0, chunk), cols] = o_scr[pl.ds(r0, chunk), cols] + inter
            upd = _tn(kt_scr[h, pl.ds(r0, chunk), :].astype(BF16), v_ref[pl.ds(r0, chunk), cols].astype(BF16))
            s_ref[j, h] = s * decay[h] + upd
        return carry

    lax.fori_loop(0, nseq, per_seq, 0)

    for h in range(RET_HEADS):
        cols = slice(h * RET_DV, (h + 1) * RET_DV)
        o = o_scr[:, cols]
        o = o * lax.rsqrt(jnp.mean(o * o, axis=-1, keepdims=True) + EPS)
        ga_ref[:, cols] = (_silu(g_ref[:, cols]) * o).astype(BF16)


def _retention(z, s0, *, n, t, pos0, chunk, nseq, layer=None, s_buf=None):
    nc = t // chunk
    rows = nseq * chunk
    assert t % chunk == 0 and n % nseq == 0 and (nseq == 1 or nc == 1)
    h = np.arange(RET_HEADS, dtype=np.float64)
    log_g = np.log1p(-np.exp2(-5.0 - h))
    idx = np.arange(chunk, dtype=np.float64)
    rel = idx[:, None] - idx[None, :]
    dmask = np.where(rel[None] >= 0, np.exp(np.maximum(rel, 0.0)[None] * log_g[:, None, None]), 0.0)
    dmask = np.stack([np.kron(np.eye(nseq), dmask[i]) for i in range(RET_HEADS)])
    cross = np.exp((idx + 1.0)[None, :] * log_g[:, None])
    tail = np.exp((chunk - 1.0 - idx)[None, :] * log_g[:, None])
    decay = tuple(float(x) for x in np.exp(chunk * log_g))
    cross_b = np.broadcast_to(cross[:, :, None], (RET_HEADS, chunk, RET_DV))
    tail_b = np.broadcast_to(np.tile(tail, (1, nseq))[:, :, None], (RET_HEADS, rows, RET_DK))
    pos = pos0 + np.arange(t, dtype=np.float64)
    inv = ROPE_BASE ** (-np.arange(0, RET_DK, 2, dtype=np.float64) / RET_DK)
    ang = pos[:, None] * inv[None, :]
    cos = np.repeat(np.cos(ang), 2, axis=1)
    sgn = np.tile(np.array([-1.0, 1.0]), RET_DK // 2)
    sin = np.repeat(np.sin(ang), 2, axis=1) * sgn[None, :]
    if nseq > 1:
        cos, sin = np.tile(cos, (nseq, 1)), np.tile(sin, (nseq, 1))
    c = lambda a: jnp.asarray(a, F32)
    kern = functools.partial(_retention_kernel, nseq=nseq, chunk=chunk, decay=decay)
    row_blk = lambda i, j: i * nc + j
    if layer is None:
        s_spec = pl.BlockSpec((nseq, RET_HEADS, RET_DK, RET_DV), lambda i, j: (i, 0, 0, 0))
    else:
        s_spec = pl.BlockSpec((None, nseq, RET_HEADS, RET_DK, RET_DV), lambda i, j: (layer, i, 0, 0, 0))
    extra, extra_specs, alias = (), [], {}
    if s_buf is not None:
        extra, extra_specs, alias = (s_buf,), [pl.BlockSpec(memory_space=pl.ANY)], {10: 1}
    ga, s_new = pl.pallas_call(
        kern,
        out_shape=(jax.ShapeDtypeStruct((n * t, RET_V), BF16), jax.ShapeDtypeStruct(s0.shape, F32)),
        grid=(n // nseq, nc),
        input_output_aliases=alias,
        in_specs=[pl.BlockSpec((rows, RET_QK), lambda i, j: (row_blk(i, j), C_RQ // RET_QK)),
                  pl.BlockSpec((rows, RET_QK), lambda i, j: (row_blk(i, j), C_RK // RET_QK)),
                  pl.BlockSpec((rows, RET_V), lambda i, j: (row_blk(i, j), C_RV // RET_V)),
                  pl.BlockSpec((rows, RET_V), lambda i, j: (row_blk(i, j), C_RG // RET_V)),
                  pl.BlockSpec((rows, RET_DK), lambda i, j: (j, 0)),
                  pl.BlockSpec((rows, RET_DK), lambda i, j: (j, 0)),
                  pl.BlockSpec((RET_HEADS, rows, rows), lambda i, j: (0, 0, 0)),
                  pl.BlockSpec((RET_HEADS, chunk, RET_DV), lambda i, j: (0, 0, 0)),
                  pl.BlockSpec((RET_HEADS, rows, RET_DK), lambda i, j: (0, 0, 0)),
                  s_spec] + extra_specs,
        out_specs=(pl.BlockSpec((rows, RET_V), lambda i, j: (row_blk(i, j), 0)), s_spec),
        scratch_shapes=[pltpu.VMEM((RET_HEADS, rows, RET_DK), F32),
                        pltpu.VMEM((RET_HEADS, rows, RET_DK), F32),
                        pltpu.VMEM((rows, RET_V), F32)],
        compiler_params=_cparams(("parallel", "arbitrary")),
        name="retention",
    )(z, z, z, z, c(cos), c(sin), c(dmask), c(cross_b), c(tail_b), s0, *extra)
    return ga, s_new


def _pool_kernel(u_ref, pg_ref, halo_ref, w_ref, sc_ref, gb_ref, ext_scr, *, nb, tt, pos0, nbt):
    hal = 16
    ext_scr[:, 0:hal, :] = halo_ref[...]
    ext_scr[:, hal:hal + tt, :] = u_ref[...]
    pos_base = pos0 + (pl.program_id(0) * nb % nbt) * tt
    p1 = lax.broadcasted_iota(I32, (nb, tt, POOL_GDIM), 1) + pos_base + 1
    for g, w in enumerate(POOL_WINDOWS):
        cols = slice(g * POOL_GDIM, (g + 1) * POOL_GDIM)
        acc = ext_scr[:, hal:hal + tt, cols]
        for j in range(1, w):
            acc = acc + ext_scr[:, hal - j:hal - j + tt, cols]
        cnt = jnp.minimum(p1, w).astype(F32)
        d = acc / cnt - u_ref[:, :, cols]
        y = _dot(d.reshape(nb * tt, POOL_GDIM).astype(BF16), w_ref[g])
        y = y * sc_ref[:, cols]
        gate = _silu(pg_ref[:, :, cols]).reshape(nb * tt, POOL_GDIM)
        gb_ref[:, cols] = (gate * y).astype(BF16)


def _pool(z, halo, w_pool_b, scale, *, n, t, pos0, tt, nb):
    nbt = t // tt
    g_tot = n * nbt
    assert g_tot % nb == 0 and (nb == 1 or nbt == 1)
    z3 = z.reshape(g_tot, tt, NP)
    kern = functools.partial(_pool_kernel, nb=nb, tt=tt, pos0=pos0, nbt=nbt)
    return pl.pallas_call(
        kern,
        out_shape=jax.ShapeDtypeStruct((n * t, POOL_WIDTH), BF16),
        grid=(g_tot // nb,),
        in_specs=[pl.BlockSpec((nb, tt, POOL_WIDTH), lambda i: (i, 0, C_PU // POOL_WIDTH)),
                  pl.BlockSpec((nb, tt, POOL_WIDTH), lambda i: (i, 0, C_PG // POOL_WIDTH)),
                  pl.BlockSpec((nb, 16, POOL_WIDTH), lambda i: (i, 0, 0)),
                  pl.BlockSpec((POOL_GROUPS, POOL_GDIM, POOL_GDIM), lambda i: (0, 0, 0)),
                  pl.BlockSpec((1, POOL_WIDTH), lambda i: (0, 0))],
        out_specs=pl.BlockSpec((nb * tt, POOL_WIDTH), lambda i: (i, 0)),
        scratch_shapes=[pltpu.VMEM((nb, 16 + tt, POOL_WIDTH), F32)],
        compiler_params=_cparams(("parallel",)),
        name="pool",
    )(z3, z3, halo, w_pool_b, scale.reshape(1, POOL_WIDTH))


def _t5_bucket(dist):
    n = jnp.maximum(dist, 0)
    exact = N_BUCKETS // 2
    nf = jnp.maximum(n, 1).astype(F32)
    large = exact + (jnp.log(nf / exact) / math.log(MAX_DIST / exact) * (N_BUCKETS - exact)).astype(I32)
    large = jnp.minimum(large, N_BUCKETS - 1)
    return jnp.where(n < exact, n, large)


def _bias_table(rel_bias):
    rb = rel_bias.astype(F32)
    tab = rb[_t5_bucket(jnp.arange(2 * KT, dtype=I32))]
    return tab - rb[N_BUCKETS - 1][None, :]


def _toeplitz_T(tab, delta, rows_q, window=None):
    assert rows_q == KT
    d = delta + np.arange(-(KT - 1), KT)
    valid = d >= 0
    if window is not None:
        valid &= d < window
    near = valid & (d < 2 * KT)
    lo = int(np.clip(d[0], 0, 2 * KT))
    hi = int(np.clip(d[-1] + 1, 0, 2 * KT))
    seg = tab[lo:hi]
    r = jnp.zeros((2 * KT - 1, NSA_HEADS), F32)
    if hi > lo:
        r = lax.dynamic_update_slice(r, seg, (int(np.nonzero(d == lo)[0][0]), 0))
    r = jnp.where(jnp.asarray(near)[:, None], r, 0.0)
    r = jnp.where(jnp.asarray(valid)[:, None], r, NEG)
    rp = jnp.concatenate([r, jnp.zeros((1, NSA_HEADS), F32)], axis=0).T
    skew = jnp.broadcast_to(rp[:, None, :], (NSA_HEADS, KT, 2 * KT)).reshape(NSA_HEADS, KT * 2 * KT)
    skew = skew[:, :KT * (2 * KT - 1)].reshape(NSA_HEADS, KT, 2 * KT - 1)
    vals = skew[:, :, KT - 1:]
    return vals.reshape(NSA_KV_HEADS, NSA_GROUP, KT, rows_q).transpose(0, 2, 1, 3).reshape(NSA_KV_HEADS, KT, NSA_GROUP * rows_q)


def _cmp_rowbias(tab, pos, per_row_head):
    b = np.asarray(pos) % L_CMP
    cols = []
    for m in range(-1, 4):
        if m < 0:
            v = jnp.where(jnp.asarray(b == L_CMP - 1)[:, None], tab[0][None, :], NEG)
        else:
            v = tab[jnp.asarray(L_CMP * m + b + 1)]
        cols.append(v)
    r = jnp.stack(cols, axis=1)
    if per_row_head is None:
        r = r.reshape(r.shape[0], 5 * NSA_HEADS)
        return jnp.pad(r, ((0, 0), (0, LANES - 5 * NSA_HEADS)))
    r = r[jnp.arange(r.shape[0]), :, jnp.asarray(per_row_head)]
    return jnp.pad(r, ((0, 0), (0, LANES - 5)))


def _cmp_bias(dist, col):
    b = jnp.where(dist <= 4 * L_CMP, col(3), 0.0)
    b = jnp.where(dist <= 3 * L_CMP, col(2), b)
    b = jnp.where(dist <= 2 * L_CMP, col(1), b)
    b = jnp.where(dist <= L_CMP, col(0), b)
    return jnp.where(dist <= 0, col(-1), b)


def _select_mask(imp, sblk, qpos, k_top, axis):
    forced = (sblk == 0) | (sblk == qpos // L_SEL)
    avail = sblk * L_SEL <= qpos
    w = jnp.where(forced, FORCE_SCORE, imp)
    w = jnp.where(avail, w, -jnp.inf)
    idx = lax.broadcasted_iota(I32, w.shape, axis).astype(F32)
    for _ in range(k_top):
        mx = jnp.max(w, axis=axis, keepdims=True)
        first = jnp.min(jnp.where(w == mx, idx, float(4 * LANES)), axis=axis, keepdims=True)
        w = jnp.where(idx == first, -jnp.inf, w)
    return jnp.where(avail & (w == -jnp.inf), 0.0, NEG)


def _cmpkv_kernel(kv_ref, pe_ref, wk_ref, wv_ref, kc_ref, vc_ref):
    x = kv_ref[...]
    nblk = x.shape[0] // L_CMP
    cm = x.reshape(nblk, L_CMP, 2 * NSA_KV).sum(axis=1) * (1.0 / L_CMP)
    cm = cm + jnp.mean(pe_ref[...], axis=0, keepdims=True)
    kc_ref[...] = _dot(cm[:, :NSA_KV].astype(BF16), wk_ref[...])
    vc_ref[...] = _dot(cm[:, NSA_KV:].astype(BF16), wv_ref[...])


def _cmpkv(z, pe2, bd_ck, bd_cv, *, n, t):
    nblk = t // L_CMP
    z3 = z.reshape(n, t, NP)
    return pl.pallas_call(
        _cmpkv_kernel,
        out_shape=(jax.ShapeDtypeStruct((n, nblk, NSA_KV), F32), jax.ShapeDtypeStruct((n, nblk, NSA_KV), F32)),
        grid=(n,),
        in_specs=[pl.BlockSpec((None, nblk * L_CMP, 2 * NSA_KV), lambda i: (i, 0, C_KV // (2 * NSA_KV))),
                  pl.BlockSpec((L_CMP, 2 * NSA_KV), lambda i: (0, 0)),
                  pl.BlockSpec((NSA_KV, NSA_KV), lambda i: (0, 0)),
                  pl.BlockSpec((NSA_KV, NSA_KV), lambda i: (0, 0))],
        out_specs=(pl.BlockSpec((None, nblk, NSA_KV), lambda i: (i, 0, 0)),
                   pl.BlockSpec((None, nblk, NSA_KV), lambda i: (i, 0, 0))),
        compiler_params=_cparams(("parallel",)),
        name="cmp_kv",
    )(z3, pe2, bd_ck, bd_cv)


def _attn_store(o_ref, g, ot):
    for k in range(NSA_GROUP // 2):
        c0 = (g * NSA_GROUP + 2 * k) * NSA_DH
        o_ref[:, c0:c0 + 2 * NSA_DH] = jnp.concatenate(
            [ot[:, 2 * k * KT:(2 * k + 1) * KT], ot[:, (2 * k + 1) * KT:(2 * k + 2) * KT]], axis=0).T


def _cmpsel_kernel(q_ref, kc_ref, vct_ref, rbt_ref, cpos_ref, sblk_ref, oc_ref, mn_ref, s_scr, *, tq, k_top):
    gq = NSA_GROUP * tq
    qs = pl.program_id(1) * tq
    qpos = lax.broadcasted_iota(I32, (1, gq), 1) % tq + qs
    valid = qpos - cpos_ref[...] >= 0
    kc = kc_ref[...].astype(BF16)
    vct = vct_ref[...].astype(BF16)
    w0 = jnp.clip((2 * pl.program_id(1) - 2) // 8 * 8, 0, LANES - NEAR)
    imps = []
    for g in range(NSA_KV_HEADS):
        qg = q_ref[:, g * NSA_GROUP * NSA_DH:(g + 1) * NSA_GROUP * NSA_DH] * (NSA_DH ** -0.5)
        qt = jnp.concatenate([qg[:, k * LANES:(k + 1) * LANES].T for k in range(NSA_GROUP // 2)], axis=0)
        qt = jnp.concatenate([qt[r * NSA_DH:(r + 1) * NSA_DH] for r in range(NSA_GROUP)], axis=1).astype(BF16)
        s_scr[...] = _dot(kc[:, g * NSA_DH:(g + 1) * NSA_DH], qt)
        rbt = rbt_ref[g]
        for half in range(2):
            rows_w = pl.ds(pl.multiple_of(half * LANES + w0, 8), NEAR)
            dw = qpos - cpos_ref[rows_w, :]
            s_scr[rows_w, :] = s_scr[rows_w, :] + _cmp_bias(dw, lambda m: rbt[m + 1:m + 2, :])
        s = jnp.where(valid, s_scr[...], M_INIT)
        mx = jnp.max(s, axis=0, keepdims=True)
        e = jnp.where(valid, jnp.exp(s - mx), 0.0)
        p = e / jnp.maximum(jnp.sum(e, axis=0, keepdims=True), 1e-30)
        _attn_store(oc_ref, g, _dot(vct[g * NSA_DH:(g + 1) * NSA_DH, :], p.astype(BF16)))
        imp = p[:, 0:tq]
        for r in range(1, NSA_GROUP):
            imp = imp + p[:, r * tq:(r + 1) * tq]
        imps.append(imp[:LANES] + imp[LANES:])
    mn = _select_mask(jnp.concatenate(imps, axis=1), sblk_ref[...], qpos[:, 0:NSA_KV_HEADS * tq], k_top, 0).astype(BF16)
    for g in range(NSA_KV_HEADS):
        mn_ref[g] = mn[:, g * tq:(g + 1) * tq]


def _cmpsel(z, kcp, vctp, rowbt, cpos, sblk, *, n, t, k_top):
    tq = KT
    assert tq == 4 * L_CMP and 4 * L_CMP == KT
    nqt = t // tq
    gq = NSA_GROUP * tq
    kern = functools.partial(_cmpsel_kernel, tq=tq, k_top=k_top)
    return pl.pallas_call(
        kern,
        out_shape=(jax.ShapeDtypeStruct((n * t, NSA_WIDTH), F32),
                   jax.ShapeDtypeStruct((n, NSA_KV_HEADS, LANES, t), BF16)),
        grid=(n, nqt),
        in_specs=[pl.BlockSpec((tq, NSA_WIDTH), lambda b, i: (b * nqt + i, C_NQ // NSA_WIDTH)),
                  pl.BlockSpec((None, 2 * LANES, NSA_KV), lambda b, i: (b, 0, 0)),
                  pl.BlockSpec((None, NSA_KV, 2 * LANES), lambda b, i: (b, 0, 0)),
                  pl.BlockSpec((NSA_KV_HEADS, 8, gq), lambda b, i: (0, 0, 0)),
                  pl.BlockSpec((2 * LANES, gq), lambda b, i: (0, 0)),
                  pl.BlockSpec((LANES, NSA_KV_HEADS * tq), lambda b, i: (0, 0))],
        out_specs=(pl.BlockSpec((tq, NSA_WIDTH), lambda b, i: (b * nqt + i, 0)),
                   pl.BlockSpec((None, NSA_KV_HEADS, LANES, tq), lambda b, i: (b, 0, 0, i))),
        scratch_shapes=[pltpu.VMEM((2 * LANES, gq), F32)],
        compiler_params=_cparams(("parallel", "parallel")),
        name="cmp_select",
    )(z, kcp, vctp, rowbt, cpos, sblk)


def _kvpack_kernel(z_ref, oh_ref, *refs, nqt):
    ks_ref, vs_ref, vs4_ref, kw_ref, vw_ref, ct_ref, st_ref, wt_ref = refs[-8:]
    i = pl.program_id(1) * TPS
    real = (i >= KPAD) & (i < KPAD + nqt)

    for u in range(TPS):
        rows = slice(u * KT, (u + 1) * KT)
        kv = z_ref[rows, :]
        kvt = [kv[:, j * NSA_KV:(j + 1) * NSA_KV].T for j in range(6)]
        for o_ref, j in ((ct_ref, 0), (st_ref, 2), (wt_ref, 4)):
            o_ref[0:NSA_KV, rows] = kvt[j]
            o_ref[NSA_KV:2 * NSA_KV, rows] = kvt[j + 1]

        @pl.when(real)
        def _(u=u, rows=rows, kv=kv, kvt=kvt):
            for k_ref, v_refs, j, lead in ((ks_ref, (vs_ref, vs4_ref.at[:, :, rows]), 2, oh_ref[rows, :]),
                                           (kw_ref, (vw_ref,), 4, jnp.zeros((KT, LANES), BF16))):
                vt = kvt[j + 1].astype(BF16)
                for g in range(NSA_KV_HEADS):
                    k_ref[g, u, :, 0:LANES] = lead
                    k_ref[g, u, :, LANES:LANES + NSA_DH] = kv[:, j * NSA_KV + g * NSA_DH:j * NSA_KV + (g + 1) * NSA_DH].astype(BF16)
                    vg = vt[g * NSA_DH:(g + 1) * NSA_DH, :]
                    v_refs[0][g, u] = vg
                    if len(v_refs) > 1:
                        v_refs[1][g] = vg

    @pl.when(jnp.logical_not(real))
    def _():
        for ref in (ks_ref, vs_ref, vs4_ref, kw_ref, vw_ref):
            ref[...] = jnp.zeros(ref.shape, BF16)


def _kvpack(z, onehot, *, n, t, layer, depth, bufs):
    nqt = t // KT
    ntile = nqt + 2 * KPAD
    assert ntile % CHUNK == 0 and KPAD % TPS == 0 and nqt % TPS == 0 and CHUNK % TPS == 0
    nstep, cps = ntile // TPS, CHUNK // TPS
    tile = lambda i: jnp.clip(i - KPAD // TPS, 0, nqt // TPS - 1)
    kshape = jax.ShapeDtypeStruct((n, NSA_KV_HEADS, ntile, KT, LANES + NSA_DH), BF16)
    vshape = jax.ShapeDtypeStruct((n, NSA_KV_HEADS, ntile, NSA_DH, KT), BF16)
    v4shape = jax.ShapeDtypeStruct((n, NSA_KV_HEADS, ntile // CHUNK, NSA_DH, CHUNK * KT), BF16)
    rshape = jax.ShapeDtypeStruct((depth, n, 2 * NSA_KV, t), F32)
    kspec = pl.BlockSpec((None, NSA_KV_HEADS, TPS, KT, LANES + NSA_DH), lambda b, i: (b, 0, i, 0, 0))
    vspec = pl.BlockSpec((None, NSA_KV_HEADS, TPS, NSA_DH, KT), lambda b, i: (b, 0, i, 0, 0))
    v4spec = pl.BlockSpec((None, NSA_KV_HEADS, None, NSA_DH, TPS * KT), lambda b, i: (b, 0, i // cps, 0, i % cps))
    rspec = pl.BlockSpec((None, None, 2 * NSA_KV, TPS * KT), lambda b, i: (layer, b, 0, tile(i)))
    extra, extra_specs, alias = (), [], {}
    if bufs is not None:
        extra, extra_specs, alias = tuple(bufs), [pl.BlockSpec(memory_space=pl.ANY)] * 3, {2: 5, 3: 6, 4: 7}
    return pl.pallas_call(
        functools.partial(_kvpack_kernel, nqt=nqt),
        out_shape=(kshape, vshape, v4shape, kshape, vshape, rshape, rshape, rshape),
        grid=(n, nstep),
        in_specs=[pl.BlockSpec((TPS * KT, 1024), lambda b, i: (b * (nqt // TPS) + tile(i), C_KV // 1024)),
                  pl.BlockSpec((TPS * KT, LANES), lambda b, i: (tile(i), 0))] + extra_specs,
        out_specs=(kspec, vspec, v4spec, kspec, vspec, rspec, rspec, rspec),
        input_output_aliases=alias,
        compiler_params=_cparams(("parallel", "arbitrary")),
        name="kv_pack",
    )(z, onehot, *extra)


def _qk_chunk(k_ref, t0, ntile, qpt_ref):
    k = k_ref[pl.ds(t0, ntile)].reshape(ntile * KT, LANES + NSA_DH)
    return _dot(k, qpt_ref[...])


def _online_softmax(st, pv, m, l, acc):
    m_new = jnp.maximum(m, jnp.max(st, axis=0, keepdims=True))
    alpha = jnp.exp2(m - m_new)
    p = jnp.exp2(st - m_new)
    l_new = alpha * l + jnp.sum(p, axis=0, keepdims=True)
    return m_new, l_new, alpha * acc + pv(p.astype(BF16))


def _pv_tiles(v_ref, t0, ntile):
    def pv(pb):
        out = _dot(v_ref[t0], pb[0:KT])
        for j in range(1, ntile):
            out = out + _dot(v_ref[t0 + j], pb[j * KT:(j + 1) * KT])
        return out
    return pv


def _selwin_kernel(q_ref, mn_ref, ks_ref, vs_ref, vs4_ref, kw_ref, vw_ref, asel_ref, awin_ref, os_ref, ow_ref,
                   qpt_scr, sta_scr, stb_scr, m_scr, l_scr, acc_scr):
    qt = pl.program_id(1)
    heads = range(NSA_KV_HEADS)
    gq = NSA_GROUP * KT
    for g in heads:
        mn = mn_ref[g]
        for k in range(NSA_GROUP // 2):
            c0 = (g * NSA_GROUP + 2 * k) * NSA_DH
            qt2 = (q_ref[:, c0:c0 + 2 * NSA_DH] * (NSA_DH ** -0.5 * LOG2E)).T.astype(BF16)
            for u in range(2):
                r = 2 * k + u
                qpt_scr[g, 0:LANES, r * KT:(r + 1) * KT] = mn
                qpt_scr[g, LANES:LANES + NSA_DH, r * KT:(r + 1) * KT] = qt2[u * NSA_DH:(u + 1) * NSA_DH]
        m_scr[g] = jnp.full((1, gq), M_INIT, F32)
        l_scr[g] = jnp.zeros((1, gq), F32)
        acc_scr[g] = jnp.zeros((NSA_DH, gq), F32)

    nplain = jnp.maximum(qt - 1, 0) // CHUNK

    def scores(st_scr, c):
        for g in heads:
            st_scr[g] = _qk_chunk(ks_ref.at[g], KPAD + CHUNK * c, CHUNK, qpt_scr.at[g])

    def update(st_scr, c):
        for g in heads:
            m, l, acc = _online_softmax(st_scr[g], lambda pb, g=g: _dot(vs4_ref[g, KPAD // CHUNK + c], pb),
                                        m_scr[g], l_scr[g], acc_scr[g])
            m_scr[g] = m
            l_scr[g] = l
            acc_scr[g] = acc

    @pl.when(nplain > 0)
    def _():
        scores(sta_scr, 0)

    def pair(i, carry):
        c0 = 2 * i
        scores(stb_scr, c0 + 1)
        update(sta_scr, c0)
        scores(sta_scr, jnp.minimum(c0 + 2, nplain - 1))
        update(stb_scr, c0 + 1)
        return carry

    lax.fori_loop(0, nplain // 2, pair, 0)

    @pl.when(nplain % 2 == 1)
    def _():
        update(sta_scr, nplain - 1)

    e0 = CHUNK * nplain
    off = pl.multiple_of((EDGE - (qt - e0)) * KT, KT)
    for g in heads:
        st = _qk_chunk(ks_ref.at[g], KPAD + e0, EDGE, qpt_scr.at[g]) + asel_ref[g, pl.ds(off, EDGE * KT), :]
        _, l, acc = _online_softmax(st, _pv_tiles(vs_ref.at[g], KPAD + e0, EDGE), m_scr[g], l_scr[g], acc_scr[g])
        _attn_store(os_ref, g, acc / l)

    for g in heads:
        st = _qk_chunk(kw_ref.at[g], qt, EDGE, qpt_scr.at[g]) + awin_ref[g]
        rows = []
        for j in range(EDGE):
            blk = st[j * KT:(j + 1) * KT]
            if j < EDGE - 1:
                blk = blk + jnp.where(qt >= EDGE - 1 - j, 0.0, NEG)
            rows.append(blk)
        st = jnp.concatenate(rows, axis=0)
        _, l, acc = _online_softmax(st, _pv_tiles(vw_ref.at[g], qt, EDGE), jnp.full((1, gq), M_INIT, F32),
                                    jnp.zeros((1, gq), F32), jnp.zeros((NSA_DH, gq), F32))
        _attn_store(ow_ref, g, acc / l)


def _selwin(z, mnt, ks, vs, vs4, kw, vw, asel, awin, *, n, t):
    nqt = t // KT
    ntile = nqt + 2 * KPAD
    gq = NSA_GROUP * KT
    hk = NSA_KV_HEADS
    once = pl.Buffered(1)
    kspec = pl.BlockSpec((None, hk, ntile, KT, LANES + NSA_DH), lambda b, i: (b, 0, 0, 0, 0), pipeline_mode=once)
    vspec = pl.BlockSpec((None, hk, ntile, NSA_DH, KT), lambda b, i: (b, 0, 0, 0, 0), pipeline_mode=once)
    v4spec = pl.BlockSpec((None, hk, ntile // CHUNK, NSA_DH, CHUNK * KT), lambda b, i: (b, 0, 0, 0, 0), pipeline_mode=once)
    ospec = pl.BlockSpec((KT, NSA_WIDTH), lambda b, i: (b * nqt + i, 0))
    return pl.pallas_call(
        _selwin_kernel,
        out_shape=(jax.ShapeDtypeStruct((n * t, NSA_WIDTH), F32), jax.ShapeDtypeStruct((n * t, NSA_WIDTH), F32)),
        grid=(n, nqt),
        in_specs=[pl.BlockSpec((KT, NSA_WIDTH), lambda b, i: (b * nqt + i, C_NQ // NSA_WIDTH)),
                  pl.BlockSpec((None, hk, LANES, KT), lambda b, i: (b, 0, 0, i)),
                  kspec, vspec, v4spec, kspec, vspec,
                  pl.BlockSpec((hk, 2 * EDGE * KT, gq), lambda b, i: (0, 0, 0), pipeline_mode=once),
                  pl.BlockSpec((hk, EDGE * KT, gq), lambda b, i: (0, 0, 0), pipeline_mode=once)],
        out_specs=(ospec, ospec),
        scratch_shapes=[pltpu.VMEM((hk, LANES + NSA_DH, gq), BF16),
                        pltpu.VMEM((hk, CHUNK * KT, gq), F32),
                        pltpu.VMEM((hk, CHUNK * KT, gq), F32),
                        pltpu.VMEM((hk, 1, gq), F32),
                        pltpu.VMEM((hk, 1, gq), F32),
                        pltpu.VMEM((hk, NSA_DH, gq), F32)],
        compiler_params=_cparams(("parallel", "arbitrary")),
        name="sel_win",
    )(z, mnt, ks, vs, vs4, kw, vw, asel, awin)


def _nsa_prompt(z, pe2, bd_ck, bd_cv, tab, *, n, t, layer, depth, row_bufs):
    assert t % KT == 0 and WINDOW == (EDGE - 1) * KT and KT % L_SEL == 0 and EDGE <= KPAD + 1 and EDGE <= CHUNK + 1
    nblk = t // L_CMP
    n_sel = -(-t // L_SEL)
    assert nblk <= 2 * LANES and n_sel <= LANES and nblk % 2 == 0
    kc, vc = _cmpkv(z, pe2, bd_ck, bd_cv, n=n, t=t)
    half = nblk // 2

    def perm(a):
        z0 = jnp.zeros((n, LANES - half, NSA_KV), F32)
        return jnp.concatenate([a[:, 0::2], z0, a[:, 1::2], z0], axis=1)

    slot = np.arange(LANES)
    big = 1 << 20
    cpos = np.concatenate([np.where(slot < half, (2 * slot) * L_CMP + L_CMP - 1, big),
                           np.where(slot < half, (2 * slot + 1) * L_CMP + L_CMP - 1, big)])
    cpos = np.broadcast_to(cpos[:, None], (2 * LANES, NSA_GROUP * KT))
    sblk = np.broadcast_to(np.where(slot < n_sel, slot, big)[:, None], (LANES, NSA_KV_HEADS * KT))
    rowb = _cmp_rowbias(tab, np.arange(KT), None)[:, :5 * NSA_HEADS].reshape(KT, 5, NSA_KV_HEADS, NSA_GROUP)
    rowbt = jnp.pad(rowb.transpose(2, 1, 3, 0).reshape(NSA_KV_HEADS, 5, NSA_GROUP * KT), ((0, 0), (0, 3), (0, 0)))
    o_cmp, mnt = _cmpsel(z, perm(kc), perm(vc).transpose(0, 2, 1), rowbt, jnp.asarray(cpos, I32), jnp.asarray(sblk, I32),
                         n=n, t=t, k_top=min(N_SEL, n_sel))

    onehot = jnp.asarray(np.arange(t)[:, None] // L_SEL == np.arange(LANES)[None, :], BF16)
    ks, vs, vs4, kw, vw, *row_bufs = _kvpack(z, onehot, n=n, t=t, layer=layer, depth=depth, bufs=row_bufs)
    a0 = _toeplitz_T(tab, 0, KT) * LOG2E
    a1 = _toeplitz_T(tab, KT, KT) * LOG2E
    aw = _toeplitz_T(tab, WINDOW, KT, window=WINDOW) * LOG2E
    zeros = jnp.zeros_like(a0)
    asel = jnp.concatenate([zeros] * (EDGE - 1) + [a1, a0] + [jnp.full_like(a0, NEG)] * (EDGE - 1), axis=1)
    awin = jnp.concatenate([aw] + [zeros] * (EDGE - 3) + [a1, a0], axis=1)
    o_sel, o_win = _selwin(z, mnt, ks, vs, vs4, kw, vw, asel, awin, n=n, t=t)
    return o_cmp, o_sel, o_win, row_bufs


def _nsa_sample_kernel(pt_ref, *refs, npg, t, past, wbuf, k_top, aliased, ns):
    del pt_ref
    cpages = refs[0:ns * npg]
    spages = refs[ns * npg:2 * ns * npg]
    rest = refs[2 * ns * npg:]
    (q_ref, kvn_ref, cw_ref, pe_ref, wk_ref, wv_ref, rb_ref, cpos_ref, sblk_ref, oht_ref, asel_ref, awin_ref) = rest[:12]
    (oc_ref, os_ref, ow_ref, wout_ref, kselt, vselt, kwint, vwint, cm_scr, kcv_scr, qw_scr, qp_scr) = rest[12 + aliased:]
    rows = NSA_HEADS * t
    half = NSA_GROUP * t
    bpp = PAGE_SIZE // L_CMP
    nblk = npg * bpp
    keep = wout_ref.shape[2]

    @pl.when(pl.program_id(0) == 0)
    def _():
        for s in range(ns):
            kselt[s, 0:LANES, :] = oht_ref[...]
        kcv_scr[...] = jnp.zeros(kcv_scr.shape, F32)
        qw_scr[...] = jnp.zeros(qw_scr.shape, F32)

    def pick(x):
        row = lax.broadcasted_iota(I32, (rows, NSA_DH), 0)
        return jnp.where(row < half, x[:, :NSA_DH], x[:, NSA_DH:])

    def store(o_ref, s, x):
        for h in range(NSA_HEADS):
            o_ref[s * t:(s + 1) * t, h * NSA_DH:(h + 1) * NSA_DH] = x[h * t:(h + 1) * t, :]

    qpos = lax.broadcasted_iota(I32, (rows, 1), 0) % t + past
    dist = qpos - cpos_ref[...]
    valid = dist >= 0
    rb = rb_ref[...]
    qws, imps = [], []
    for s in range(ns):
        kvn = kvn_ref[s * t:(s + 1) * t, :]
        kvpad = jnp.concatenate([kvn[:, 0:6 * NSA_KV], jnp.zeros((KT - t, 6 * NSA_KV), F32)], axis=0)
        new_t = [kvpad[:, j * NSA_KV:(j + 1) * NSA_KV].T for j in range(2, 6)]

        for p in range(npg):
            sp, cp = spages[s * npg + p], cpages[s * npg + p]
            kselt[s, LANES:2 * LANES, p * PAGE_SIZE:(p + 1) * PAGE_SIZE] = sp[0:NSA_KV, :].astype(BF16)
            vselt[s, :, p * PAGE_SIZE:(p + 1) * PAGE_SIZE] = sp[NSA_KV:2 * NSA_KV, :].astype(BF16)
            for c in range(2):
                blk = cp[c * NSA_KV:(c + 1) * NSA_KV, :].T
                cm_scr[s, p * bpp:(p + 1) * bpp, c * NSA_KV:(c + 1) * NSA_KV] = blk.reshape(bpp, L_CMP, NSA_KV).sum(axis=1) * (1.0 / L_CMP)
        kselt[s, LANES:2 * LANES, past:past + KT] = new_t[0].astype(BF16)
        vselt[s, :, past:past + KT] = new_t[1].astype(BF16)
        cw = cw_ref[s]
        kwint[s, :, 0:wbuf] = cw[0:NSA_KV, :].astype(BF16)
        vwint[s, :, 0:wbuf] = cw[NSA_KV:2 * NSA_KV, :].astype(BF16)
        kwint[s, :, wbuf:wbuf + KT] = new_t[2].astype(BF16)
        vwint[s, :, wbuf:wbuf + KT] = new_t[3].astype(BF16)
        for c in range(2):
            wout_ref[s, c * NSA_KV:(c + 1) * NSA_KV, :] = jnp.concatenate(
                [cw[c * NSA_KV:(c + 1) * NSA_KV, wbuf + t - keep:wbuf], new_t[2 + c][:, 0:t]], axis=1)

        for h in range(NSA_HEADS):
            g = h // NSA_GROUP
            qh = q_ref[s * t:(s + 1) * t, h * NSA_DH:(h + 1) * NSA_DH] * (NSA_DH ** -0.5)
            qw_scr[s, h * t:(h + 1) * t, g * NSA_DH:(g + 1) * NSA_DH] = qh
        qw = qw_scr[s].astype(BF16)
        qws.append(qw)

        cm = cm_scr[s] + jnp.mean(pe_ref[...], axis=0, keepdims=True)
        kcv_scr[s, 0:nblk, 0:NSA_KV] = _dot(cm[:, :NSA_KV].astype(BF16), wk_ref[...])
        kcv_scr[s, 0:nblk, NSA_KV:] = _dot(cm[:, NSA_KV:].astype(BF16), wv_ref[...])
        kc = kcv_scr[s, :, 0:NSA_KV].astype(BF16)
        vc = kcv_scr[s, :, NSA_KV:].astype(BF16)
        sc = _nt(qw, kc) + _cmp_bias(dist, lambda m: rb[:, m + 1:m + 2])
        sc = jnp.where(valid, sc, M_INIT)
        mx = jnp.max(sc, axis=-1, keepdims=True)
        e = jnp.where(valid, jnp.exp(sc - mx), 0.0)
        p = e / jnp.maximum(jnp.sum(e, axis=-1, keepdims=True), 1e-30)
        store(oc_ref, s, pick(_dot(p.astype(BF16), vc)))

        for g in range(NSA_KV_HEADS):
            imp = p[g * half:g * half + t]
            for r in range(1, NSA_GROUP):
                imp = imp + p[g * half + r * t:g * half + (r + 1) * t]
            imps.append(imp + pltpu.roll(imp, LANES - 1, 1))

    imps.append(jnp.zeros((LANES - ns * NSA_KV_HEADS * t, LANES), F32))
    imp_t = jnp.concatenate(imps, axis=0).T
    qpos_l = lax.broadcasted_iota(I32, (1, LANES), 1) % t + past
    mn = _select_mask(imp_t, sblk_ref[...], qpos_l, k_top, 0).T.astype(BF16)

    def attend(qmat, kt_all, vt_all, add):
        sc = _dot(qmat, kt_all) + add
        mm = jnp.max(sc, axis=-1, keepdims=True)
        ee = jnp.exp(sc - mm)
        ll = jnp.sum(ee, axis=-1, keepdims=True)
        return pick(_nt(ee.astype(BF16), vt_all)) / ll

    for s in range(ns):
        for g in range(NSA_KV_HEADS):
            row0 = (s * NSA_KV_HEADS + g) * t
            for r in range(NSA_GROUP):
                qp_scr[s, g * half + r * t:g * half + (r + 1) * t, 0:LANES] = mn[row0:row0 + t]
        qp_scr[s, :, LANES:2 * LANES] = qws[s]
        store(os_ref, s, attend(qp_scr[s], kselt[s], vselt[s], asel_ref[...]))
        store(ow_ref, s, attend(qws[s], kwint[s], vwint[s], awin_ref[...]))


def _nsa_sample(z, cache_cmp_t, cache_sel_t, cache_win_t, page_table, pe2, bd_ck, bd_cv, tab, *, n, t, past, layer, win_buf):
    npg = page_table.shape[1]
    depth = cache_win_t.shape[0]
    assert past == npg * PAGE_SIZE and past % L_SEL == 0 and past % L_CMP == 0 and PAGE_SIZE == KT
    wbuf = cache_win_t.shape[3]
    keep = min(WINDOW, wbuf + t)
    nblk = past // L_CMP
    assert t < L_CMP and nblk <= LANES and wbuf % KT == 0 and keep > t
    length = past + t
    n_sel = -(-length // L_SEL)
    ns = 2 if n % 2 == 0 else 1
    assert 2 * (n_sel - 1) < LANES and ns * NSA_KV_HEADS * t <= LANES
    rows = NSA_HEADS * t
    lsel = past + KT
    lwin = wbuf + KT
    big = 1 << 20
    slot = np.arange(LANES)
    cpos = np.where(slot < nblk, slot * L_CMP + L_CMP - 1, big)[None, :]
    sblk = np.broadcast_to(np.where((slot % 2 == 0) & (slot // 2 < n_sel), slot // 2, big)[:, None], (LANES, LANES))
    head_of_row = np.repeat(np.arange(NSA_HEADS), t)
    tok_of_row = np.tile(np.arange(t), NSA_HEADS)
    rowb = _cmp_rowbias(tab, past + tok_of_row, head_of_row)
    kpos_sel = np.arange(lsel)
    onehot_t = jnp.asarray((kpos_sel[None, :] < length) & (2 * (kpos_sel[None, :] // L_SEL) == slot[:, None]), BF16)

    def additive(kpos, exists, window):
        d = (past + np.arange(t))[:, None] - kpos[None, :]
        valid = (d >= 0) & exists[None, :]
        if window is not None:
            valid &= d < window
        near = valid & (d < 2 * KT)
        j0 = int(np.nonzero(near.any(axis=0))[0].min()) // KT * KT
        vals = tab[jnp.asarray(np.clip(d[:, j0:], 0, 2 * KT - 1))]
        vals = jnp.where(jnp.asarray(near[:, j0:])[:, :, None], vals, 0.0)
        vals = jnp.pad(vals, ((0, 0), (j0, 0), (0, 0)))
        vals = jnp.where(jnp.asarray(valid)[:, :, None], vals, NEG)
        return vals.transpose(2, 0, 1).reshape(NSA_HEADS * t, kpos.shape[0])

    asel = additive(kpos_sel, kpos_sel < length, None)
    kidx = np.arange(lwin)
    awin = additive(past - wbuf + kidx, kidx < wbuf + t, WINDOW)

    pages = [(s, p) for s in range(ns) for p in range(npg)]
    page_spec = lambda s, p: pl.BlockSpec((None, None, 2 * NSA_KV, PAGE_SIZE),
                                          lambda i, pt, s=s, p=p: (layer, pt[ns * i + s, p], 0, 0))
    const = lambda shape: pl.BlockSpec(shape, lambda i, pt: tuple(0 for _ in shape))
    ospec = pl.BlockSpec((ns * t, NSA_WIDTH), lambda i, pt: (i, 0))
    aliased = win_buf is not None
    extra, extra_specs, alias = (), [], {}
    if aliased:
        extra, extra_specs, alias = (win_buf,), [pl.BlockSpec(memory_space=pl.ANY)], {1 + 2 * ns * npg + 12: 3}
    kern = functools.partial(_nsa_sample_kernel, npg=npg, t=t, past=past, wbuf=wbuf, k_top=min(N_SEL, n_sel),
                             aliased=int(aliased), ns=ns)
    grid_spec = pltpu.PrefetchScalarGridSpec(
        num_scalar_prefetch=1,
        grid=(n // ns,),
        in_specs=[page_spec(s, p) for s, p in pages] + [page_spec(s, p) for s, p in pages] + [
            pl.BlockSpec((ns * t, NSA_WIDTH), lambda i, pt: (i, C_NQ // NSA_WIDTH)),
            pl.BlockSpec((ns * t, 1024), lambda i, pt: (i, C_KV // 1024)),
            pl.BlockSpec((None, ns, 2 * NSA_KV, wbuf), lambda i, pt: (layer, i, 0, 0)),
            const((L_CMP, 2 * NSA_KV)), const((NSA_KV, NSA_KV)), const((NSA_KV, NSA_KV)),
            const((rows, LANES)), const((1, LANES)), const((LANES, LANES)),
            const((LANES, lsel)), const((rows, lsel)), const((rows, lwin))] + extra_specs,
        out_specs=(ospec, ospec, ospec, pl.BlockSpec((None, ns, 2 * NSA_KV, keep), lambda i, pt: (layer, i, 0, 0))),
        scratch_shapes=[pltpu.VMEM((ns, 2 * LANES, lsel), BF16), pltpu.VMEM((ns, LANES, lsel), BF16),
                        pltpu.VMEM((ns, LANES, lwin), BF16), pltpu.VMEM((ns, LANES, lwin), BF16),
                        pltpu.VMEM((ns, nblk, 2 * NSA_KV), F32), pltpu.VMEM((ns, LANES, 2 * NSA_KV), F32),
                        pltpu.VMEM((ns, rows, NSA_KV), F32),
                        pltpu.VMEM((ns, rows, 2 * LANES), BF16)])
    o_cmp, o_sel, o_win, win_new = pl.pallas_call(
        kern,
        grid_spec=grid_spec,
        out_shape=(jax.ShapeDtypeStruct((n * t, NSA_WIDTH), F32),) * 3
        + (jax.ShapeDtypeStruct((depth, n, 2 * NSA_KV, keep), F32),),
        input_output_aliases=alias,
        compiler_params=_cparams(("arbitrary",)),
        name="nsa_sample",
    )(page_table, *([cache_cmp_t] * (ns * npg)), *([cache_sel_t] * (ns * npg)), z, z, cache_win_t, pe2, bd_ck, bd_cv, rowb,
      jnp.asarray(cpos, I32), jnp.asarray(sblk, I32), onehot_t, asel, awin, *extra)
    return o_cmp, o_sel, o_win, win_new


def _merge_kernel(x_ref, mg_ref, ga_ref, gb_ref, oc_ref, os_ref, ow_ref, nbg_ref, ng_ref,
                  wa_ref, wb_ref, wc_ref, wo_ref, fg_ref, y_ref, *, final):
    tm = x_ref.shape[0]
    gates = jax.nn.sigmoid(nbg_ref[...])
    lane = lax.broadcasted_iota(I32, (tm, LANES), 1)
    low = lane < NSA_DH
    parts = []
    for pi in range(NSA_HEADS // 2):
        cols = slice(pi * LANES, (pi + 1) * LANES)
        acc = jnp.zeros((tm, LANES), F32)
        for b, o_ref in enumerate((oc_ref, os_ref, ow_ref)):
            c0 = b * NSA_HEADS + 2 * pi
            gt = jnp.where(low, gates[:, c0:c0 + 1], gates[:, c0 + 1:c0 + 2])
            acc = acc + gt * o_ref[:, cols]
        parts.append((_silu(ng_ref[:, cols]) * acc).astype(BF16))
    gc = jnp.concatenate(parts, axis=1)
    b_a = _dot(ga_ref[...], wa_ref[...])
    b_b = _dot(gb_ref[...], wb_ref[...])
    b_c = _dot(gc, wc_ref[...])
    mg = mg_ref[...]
    merged = (jax.nn.sigmoid(mg[:, 0:D_MODEL]) * b_a + jax.nn.sigmoid(mg[:, D_MODEL:2 * D_MODEL]) * b_b
              + jax.nn.sigmoid(mg[:, 2 * D_MODEL:]) * b_c)
    y = x_ref[...] + _dot(merged.astype(BF16), wo_ref[...])
    if final:
        y = y * lax.rsqrt(jnp.mean(y * y, axis=-1, keepdims=True) + EPS) * fg_ref[...]
    y_ref[...] = y


def _merge(x2, z, ga, gb, o_cmp, o_sel, o_win, wa, wb, wc, wo, fgain, *, final):
    m = x2.shape[0]
    tm = min(512, m)
    row = lambda w, c=0: pl.BlockSpec((tm, w), lambda i, c=c: (i, c))
    const = lambda a: pl.BlockSpec(a.shape, lambda i: (0, 0), pipeline_mode=pl.Buffered(1))
    fg = fgain.reshape(1, D_MODEL)
    return pl.pallas_call(
        functools.partial(_merge_kernel, final=final),
        out_shape=jax.ShapeDtypeStruct((m, D_MODEL), F32),
        grid=(m // tm,),
        in_specs=[row(D_MODEL), row(3 * D_MODEL, C_MG // (3 * D_MODEL)), row(RET_V), row(POOL_WIDTH),
                  row(NSA_WIDTH), row(NSA_WIDTH), row(NSA_WIDTH),
                  row(LANES, C_NBG // LANES), row(NSA_WIDTH, C_NG // NSA_WIDTH),
                  const(wa), const(wb), const(wc), const(wo), const(fg)],
        out_specs=row(D_MODEL),
        compiler_params=_cparams(("parallel",)),
        name="merge",
    )(x2, z, ga, gb, o_cmp, o_sel, o_win, z, z, wa, wb, wc, wo, fg)


def _kv_rows(z, n, t):
    kv = z[:, C_KV:C_KV + 6 * NSA_KV].reshape(n, t, 3, 2, NSA_KV_HEADS, NSA_DH)
    return kv[:, :, 0], kv[:, :, 1], kv[:, :, 2]


def _slots_minor(cache):
    d, x, slots = cache.shape[:3]
    return cache.transpose(0, 1, 3, 4, 5, 2).reshape(d, x, 2 * NSA_KV, slots)


def _slots_major(rows_t):
    d, n, _, slots = rows_t.shape
    return rows_t.reshape(d, n, 2, NSA_KV_HEADS, NSA_DH, slots).transpose(0, 1, 5, 2, 3, 4)


def kernel(x_prompt, x_sample, state_ret, state_pool, cache_win, cache_cmp, cache_sel, page_table, norm_gain, w_in, w_pool, pool_scale, pe_cmp, w_ck, w_cv, w_br_a, w_br_b, w_br_c, w_out, rel_bias, final_gain):
    b, seq, _ = x_prompt.shape
    nb, dseq, _ = x_sample.shape
    depth = w_in.shape[0]
    past = page_table.shape[1] * PAGE_SIZE
    pt = page_table.astype(I32)

    w_in_p = _pack_w_in(w_in)
    w_pool_b = w_pool.astype(BF16)
    wa, wb, wc, wo = (w.astype(BF16) for w in (w_br_a, w_br_b, w_br_c, w_out))
    eye = jnp.eye(NSA_KV_HEADS, dtype=F32)
    bd_ck = jax.vmap(lambda w: jnp.kron(eye, w))(w_ck).astype(BF16)
    bd_cv = jax.vmap(lambda w: jnp.kron(eye, w))(w_cv).astype(BF16)
    pe2 = pe_cmp.reshape(depth, L_CMP, 2 * NSA_KV)
    tab = _bias_table(rel_bias)
    cmp_t, sel_t, win_t = _slots_minor(cache_cmp), _slots_minor(cache_sel), _slots_minor(cache_win)

    ret_chunk = 256 if seq % 256 == 0 else seq
    pool_tt = min(1024, seq)
    samp_nseq = 8 if nb % 8 == 0 else 1
    samp_nb = 16 if nb % 16 == 0 else 1

    hp = x_prompt.reshape(b * seq, D_MODEL)
    hs = x_sample.reshape(nb * dseq, D_MODEL)
    zero_state = jnp.zeros((b, RET_HEADS, RET_DK, RET_DV), F32)
    ret_p, pool_p, pool_s, cmp_s, sel_s = [], [], [], [], []
    row_bufs = ret_s = win_s = None
    for l in range(depth):
        final = l == depth - 1
        z = _inproj(hp, norm_gain[l], w_in_p[l])
        ga, s_new = _retention(z, zero_state, n=b, t=seq, pos0=0, chunk=ret_chunk, nseq=1)
        u = z[:, C_PU:C_PU + POOL_WIDTH].reshape(b, seq, POOL_WIDTH)
        nbt = seq // pool_tt
        halo = u.reshape(b, nbt, pool_tt, POOL_WIDTH)[:, :, pool_tt - 16:, :]
        halo = jnp.concatenate([jnp.zeros((b, 1, 16, POOL_WIDTH), F32), halo[:, :nbt - 1]], axis=1)
        gb = _pool(z, halo.reshape(b * nbt, 16, POOL_WIDTH), w_pool_b[l], pool_scale[l], n=b, t=seq, pos0=0, tt=pool_tt, nb=1)
        o_cmp, o_sel, o_win, row_bufs = _nsa_prompt(z, pe2[l], bd_ck[l], bd_cv[l], tab, n=b, t=seq, layer=l, depth=depth,
                                                    row_bufs=row_bufs)
        hp = _merge(hp, z, ga, gb, o_cmp, o_sel, o_win, wa[l], wb[l], wc[l], wo[l], final_gain, final=final)
        ret_p.append(s_new)
        pool_p.append(u[:, seq - POOL_BUF:])

        z = _inproj(hs, norm_gain[l], w_in_p[l])
        ga, ret_s = _retention(z, state_ret, n=nb, t=dseq, pos0=past, chunk=dseq, nseq=samp_nseq, layer=l, s_buf=ret_s)
        u = z[:, C_PU:C_PU + POOL_WIDTH].reshape(nb, dseq, POOL_WIDTH)
        ext = jnp.concatenate([state_pool[l], u], axis=1)
        halo = jnp.pad(state_pool[l], ((0, 0), (16 - POOL_BUF, 0), (0, 0)))
        gb = _pool(z, halo, w_pool_b[l], pool_scale[l], n=nb, t=dseq, pos0=past, tt=dseq, nb=samp_nb)
        o_cmp, o_sel, o_win, win_s = _nsa_sample(z, cmp_t, sel_t, win_t, pt, pe2[l], bd_ck[l], bd_cv[l], tab,
                                                 n=nb, t=dseq, past=past, layer=l, win_buf=win_s)
        cmp_new, sel_new, _ = _kv_rows(z, nb, dseq)
        hs = _merge(hs, z, ga, gb, o_cmp, o_sel, o_win, wa[l], wb[l], wc[l], wo[l], final_gain, final=final)
        pool_s.append(ext[:, ext.shape[1] - POOL_BUF:])
        cmp_s.append(cmp_new)
        sel_s.append(sel_new)

    y_prompt = hp.reshape(b, seq, D_MODEL)
    y_sample = hs.reshape(nb, dseq, D_MODEL)
    cmp_p, sel_p, win_p = row_bufs
    keep_p = min(WINDOW, seq)
    st = lambda xs: jnp.stack(xs)
    return (y_prompt, y_sample, st(ret_p), ret_s, st(pool_p), st(pool_s),
            _slots_major(win_p[:, :, :, seq - keep_p:]), _slots_major(win_s),
            _slots_major(cmp_p), st(cmp_s), _slots_major(sel_p), st(sel_s))
```

```python
import functools
import math

import numpy as np
import jax
import jax.numpy as jnp
from jax import lax
from jax.experimental import pallas as pl
from jax.experimental.pallas import tpu as pltpu

F32 = jnp.float32
BF16 = jnp.bfloat16
I32 = jnp.int32

D_MODEL = 1024
PAGE_SIZE = 128
RET_HEADS, RET_DK, RET_DV = 4, 128, 256
RET_QK, RET_V = RET_HEADS * RET_DK, RET_HEADS * RET_DV
ROPE_BASE = 10000.0
POOL_WINDOWS = (2, 4, 8, 16)
POOL_GROUPS, POOL_GDIM = 4, 128
POOL_WIDTH = POOL_GROUPS * POOL_GDIM
POOL_BUF = 15
NSA_HEADS, NSA_KV_HEADS, NSA_DH = 8, 2, 64
NSA_GROUP = NSA_HEADS // NSA_KV_HEADS
NSA_WIDTH, NSA_KV = NSA_HEADS * NSA_DH, NSA_KV_HEADS * NSA_DH
L_CMP, L_SEL, N_SEL, WINDOW = 32, 64, 16, 512
FORCE_SCORE = 1e4
N_BUCKETS, MAX_DIST = 32, 128
EPS = 1e-6
IN_SIZES = (RET_QK, RET_QK, RET_V, RET_V, POOL_WIDTH, POOL_WIDTH, NSA_WIDTH,
            NSA_KV, NSA_KV, NSA_KV, NSA_KV, NSA_KV, NSA_KV, 3 * NSA_HEADS, NSA_WIDTH, 3 * D_MODEL)

LANES = 128
VMEM_LIMIT = 56 * 1024 * 1024

C_MG, C_RV, C_RG, C_RQ, C_RK = 0, 3072, 4096, 5120, 5632
C_PU, C_PG, C_NQ, C_NG, C_KV, C_NBG = 6144, 6656, 7168, 7680, 8192, 8960
NP = 9216
TM_INPROJ, TN_INPROJ = 256, 1024
KT = 128
KPAD = 4
CHUNK = 4
TPS = 4
EDGE = 5
NEAR = 16
NEG = -float(2.0 ** 100)
M_INIT = -1e30
LOG2E = 1.4426950408889634


def _cparams(sem):
    return pltpu.CompilerParams(dimension_semantics=sem, vmem_limit_bytes=VMEM_LIMIT)


def _nt(a, b):
    return lax.dot_general(a, b, (((1,), (1,)), ((), ())), preferred_element_type=F32)


def _tn(a, b):
    return lax.dot_general(a, b, (((0,), (0,)), ((), ())), preferred_element_type=F32)


def _dot(a, b):
    return jnp.dot(a, b, preferred_element_type=F32)


def _silu(x):
    return x * jax.nn.sigmoid(x)


def _inproj_kernel(x_ref, g_ref, w_ref, z_ref):
    x = x_ref[...]
    ms = jnp.mean(x * x, axis=-1, keepdims=True)
    xn = (x * lax.rsqrt(ms + EPS) * g_ref[...]).astype(BF16)
    for j in range(NP // TN_INPROJ):
        cols = slice(j * TN_INPROJ, (j + 1) * TN_INPROJ)
        z_ref[:, cols] = _dot(xn, w_ref[:, cols])


def _inproj(x2, gain, w_p):
    m = x2.shape[0]
    tm = min(TM_INPROJ, m)
    return pl.pallas_call(
        _inproj_kernel,
        out_shape=jax.ShapeDtypeStruct((m, NP), F32),
        grid=(m // tm,),
        in_specs=[pl.BlockSpec((tm, D_MODEL), lambda i: (i, 0)),
                  pl.BlockSpec((1, D_MODEL), lambda i: (0, 0)),
                  pl.BlockSpec((D_MODEL, NP), lambda i: (0, 0), pipeline_mode=pl.Buffered(1))],
        out_specs=pl.BlockSpec((tm, NP), lambda i: (i, 0)),
        compiler_params=_cparams(("parallel",)),
        name="inproj",
    )(x2, gain.reshape(1, D_MODEL), w_p)


def _pack_w_in(w_in):
    offs = np.cumsum((0,) + IN_SIZES)
    f = [w_in[..., offs[i]:offs[i + 1]] for i in range(len(IN_SIZES))]
    (rq, rk, rv, rg, pu, pg, nq, ck, cv, sk, sv, wk, wv, nbg, ng, mg) = f
    pad = lambda n: jnp.zeros(w_in.shape[:-1] + (n,), w_in.dtype)
    cols = [mg, rv, rg, rq, rk, pu, pg, nq, ng, ck, cv, sk, sv, wk, wv, nbg, pad(LANES - 3 * NSA_HEADS), pad(NP - C_NBG - LANES)]
    return jnp.concatenate(cols, axis=-1).astype(BF16)


def _retention_kernel(q_ref, k_ref, v_ref, g_ref, cos_ref, sin_ref, dm_ref, cross_ref, tail_ref, s0_ref,
                      *rest, nseq, chunk, decay):
    ga_ref, s_ref, qr_scr, kt_scr, o_scr = rest[-5:]
    rows = nseq * chunk

    @pl.when(pl.program_id(1) == 0)
    def _():
        s_ref[...] = s0_ref[...]

    cos = cos_ref[...]
    sin = sin_ref[...]
    lane = lax.broadcasted_iota(I32, (rows, RET_DK), 1)
    even = (lane % 2) == 0

    def rot(x):
        sw = jnp.where(even, pltpu.roll(x, RET_DK - 1, 1), pltpu.roll(x, 1, 1))
        return x * cos + sw * sin

    for h in range(RET_HEADS):
        qr = rot(q_ref[:, h * RET_DK:(h + 1) * RET_DK])
        kr = rot(k_ref[:, h * RET_DK:(h + 1) * RET_DK]) * (RET_DK ** -0.5)
        att = _nt(qr.astype(BF16), kr.astype(BF16)) * dm_ref[h]
        o_scr[:, h * RET_DV:(h + 1) * RET_DV] = _dot(att.astype(BF16), v_ref[:, h * RET_DV:(h + 1) * RET_DV].astype(BF16))
        qr_scr[h] = qr
        kt_scr[h] = kr * tail_ref[h]

    def per_seq(j, carry):
        r0 = pl.multiple_of(j * chunk, chunk)
        for h in range(RET_HEADS):
            s = s_ref[j, h]
            cols = slice(h * RET_DV, (h + 1) * RET_DV)
            inter = _dot(qr_scr[h, pl.ds(r0, chunk), :].astype(BF16), s.astype(BF16)) * cross_ref[h]
            o_scr[pl.ds(r0, chunk), cols] = o_scr[pl.ds(r0, chunk), cols] + inter
            upd = _tn(kt_scr[h, pl.ds(r0, chunk), :].astype(BF16), v_ref[pl.ds(r0, chunk), cols].astype(BF16))
            s_ref[j, h] = s * decay[h] + upd
        return carry

    lax.fori_loop(0, nseq, per_seq, 0)

    for h in range(RET_HEADS):
        cols = slice(h * RET_DV, (h + 1) * RET_DV)
        o = o_scr[:, cols]
        o = o * lax.rsqrt(jnp.mean(o * o, axis=-1, keepdims=True) + EPS)
        ga_ref[:, cols] = (_silu(g_ref[:, cols]) * o).astype(BF16)


def _retention(z, s0, *, n, t, pos0, chunk, nseq, layer=None, s_buf=None):
    nc = t // chunk
    rows = nseq * chunk
    assert t % chunk == 0 and n % nseq == 0 and (nseq == 1 or nc == 1)
    h = np.arange(RET_HEADS, dtype=np.float64)
    log_g = np.log1p(-np.exp2(-5.0 - h))
    idx = np.arange(chunk, dtype=np.float64)
    rel = idx[:, None] - idx[None, :]
    dmask = np.where(rel[None] >= 0, np.exp(np.maximum(rel, 0.0)[None] * log_g[:, None, None]), 0.0)
    dmask = np.stack([np.kron(np.eye(nseq), dmask[i]) for i in range(RET_HEADS)])
    cross = np.exp((idx + 1.0)[None, :] * log_g[:, None])
    tail = np.exp((chunk - 1.0 - idx)[None, :] * log_g[:, None])
    decay = tuple(float(x) for x in np.exp(chunk * log_g))
    cross_b = np.broadcast_to(cross[:, :, None], (RET_HEADS, chunk, RET_DV))
    tail_b = np.broadcast_to(np.tile(tail, (1, nseq))[:, :, None], (RET_HEADS, rows, RET_DK))
    pos = pos0 + np.arange(t, dtype=np.float64)
    inv = ROPE_BASE ** (-np.arange(0, RET_DK, 2, dtype=np.float64) / RET_DK)
    ang = pos[:, None] * inv[None, :]
    cos = np.repeat(np.cos(ang), 2, axis=1)
    sgn = np.tile(np.array([-1.0, 1.0]), RET_DK // 2)
    sin = np.repeat(np.sin(ang), 2, axis=1) * sgn[None, :]
    if nseq > 1:
        cos, sin = np.tile(cos, (nseq, 1)), np.tile(sin, (nseq, 1))
    c = lambda a: jnp.asarray(a, F32)
    kern = functools.partial(_retention_kernel, nseq=nseq, chunk=chunk, decay=decay)
    row_blk = lambda i, j: i * nc + j
    if layer is None:
        s_spec = pl.BlockSpec((nseq, RET_HEADS, RET_DK, RET_DV), lambda i, j: (i, 0, 0, 0))
    else:
        s_spec = pl.BlockSpec((None, nseq, RET_HEADS, RET_DK, RET_DV), lambda i, j: (layer, i, 0, 0, 0))
    extra, extra_specs, alias = (), [], {}
    if s_buf is not None:
        extra, extra_specs, alias = (s_buf,), [pl.BlockSpec(memory_space=pl.ANY)], {10: 1}
    ga, s_new = pl.pallas_call(
        kern,
        out_shape=(jax.ShapeDtypeStruct((n * t, RET_V), BF16), jax.ShapeDtypeStruct(s0.shape, F32)),
        grid=(n // nseq, nc),
        input_output_aliases=alias,
        in_specs=[pl.BlockSpec((rows, RET_QK), lambda i, j: (row_blk(i, j), C_RQ // RET_QK)),
                  pl.BlockSpec((rows, RET_QK), lambda i, j: (row_blk(i, j), C_RK // RET_QK)),
                  pl.BlockSpec((rows, RET_V), lambda i, j: (row_blk(i, j), C_RV // RET_V)),
                  pl.BlockSpec((rows, RET_V), lambda i, j: (row_blk(i, j), C_RG // RET_V)),
                  pl.BlockSpec((rows, RET_DK), lambda i, j: (j, 0)),
                  pl.BlockSpec((rows, RET_DK), lambda i, j: (j, 0)),
                  pl.BlockSpec((RET_HEADS, rows, rows), lambda i, j: (0, 0, 0)),
                  pl.BlockSpec((RET_HEADS, chunk, RET_DV), lambda i, j: (0, 0, 0)),
                  pl.BlockSpec((RET_HEADS, rows, RET_DK), lambda i, j: (0, 0, 0)),
                  s_spec] + extra_specs,
        out_specs=(pl.BlockSpec((rows, RET_V), lambda i, j: (row_blk(i, j), 0)), s_spec),
        scratch_shapes=[pltpu.VMEM((RET_HEADS, rows, RET_DK), F32),
                        pltpu.VMEM((RET_HEADS, rows, RET_DK), F32),
                        pltpu.VMEM((rows, RET_V), F32)],
        compiler_params=_cparams(("parallel", "arbitrary")),
        name="retention",
    )(z, z, z, z, c(cos), c(sin), c(dmask), c(cross_b), c(tail_b), s0, *extra)
    return ga, s_new


def _pool_kernel(u_ref, pg_ref, halo_ref, w_ref, sc_ref, gb_ref, ext_scr, *, nb, tt, pos0, nbt):
    hal = 16
    ext_scr[:, 0:hal, :] = halo_ref[...]
    ext_scr[:, hal:hal + tt, :] = u_ref[...]
    pos_base = pos0 + (pl.program_id(0) * nb % nbt) * tt
    p1 = lax.broadcasted_iota(I32, (nb, tt, POOL_GDIM), 1) + pos_base + 1
    for g, w in enumerate(POOL_WINDOWS):
        cols = slice(g * POOL_GDIM, (g + 1) * POOL_GDIM)
        acc = ext_scr[:, hal:hal + tt, cols]
        for j in range(1, w):
            acc = acc + ext_scr[:, hal - j:hal - j + tt, cols]
        cnt = jnp.minimum(p1, w).astype(F32)
        d = acc / cnt - u_ref[:, :, cols]
        y = _dot(d.reshape(nb * tt, POOL_GDIM).astype(BF16), w_ref[g])
        y = y * sc_ref[:, cols]
        gate = _silu(pg_ref[:, :, cols]).reshape(nb * tt, POOL_GDIM)
        gb_ref[:, cols] = (gate * y).astype(BF16)


def _pool(z, halo, w_pool_b, scale, *, n, t, pos0, tt, nb):
    nbt = t // tt
    g_tot = n * nbt
    assert g_tot % nb == 0 and (nb == 1 or nbt == 1)
    z3 = z.reshape(g_tot, tt, NP)
    kern = functools.partial(_pool_kernel, nb=nb, tt=tt, pos0=pos0, nbt=nbt)
    return pl.pallas_call(
        kern,
        out_shape=jax.ShapeDtypeStruct((n * t, POOL_WIDTH), BF16),
        grid=(g_tot // nb,),
        in_specs=[pl.BlockSpec((nb, tt, POOL_WIDTH), lambda i: (i, 0, C_PU // POOL_WIDTH)),
                  pl.BlockSpec((nb, tt, POOL_WIDTH), lambda i: (i, 0, C_PG // POOL_WIDTH)),
                  pl.BlockSpec((nb, 16, POOL_WIDTH), lambda i: (i, 0, 0)),
                  pl.BlockSpec((POOL_GROUPS, POOL_GDIM, POOL_GDIM), lambda i: (0, 0, 0)),
                  pl.BlockSpec((1, POOL_WIDTH), lambda i: (0, 0))],
        out_specs=pl.BlockSpec((nb * tt, POOL_WIDTH), lambda i: (i, 0)),
        scratch_shapes=[pltpu.VMEM((nb, 16 + tt, POOL_WIDTH), F32)],
        compiler_params=_cparams(("parallel",)),
        name="pool",
    )(z3, z3, halo, w_pool_b, scale.reshape(1, POOL_WIDTH))


def _t5_bucket(dist):
    n = jnp.maximum(dist, 0)
    exact = N_BUCKETS // 2
    nf = jnp.maximum(n, 1).astype(F32)
    large = exact + (jnp.log(nf / exact) / math.log(MAX_DIST / exact) * (N_BUCKETS - exact)).astype(I32)
    large = jnp.minimum(large, N_BUCKETS - 1)
    return jnp.where(n < exact, n, large)


def _bias_table(rel_bias):
    rb = rel_bias.astype(F32)
    tab = rb[_t5_bucket(jnp.arange(2 * KT, dtype=I32))]
    return tab - rb[N_BUCKETS - 1][None, :]


def _toeplitz_T(tab, delta, rows_q, window=None):
    assert rows_q == KT
    d = delta + np.arange(-(KT - 1), KT)
    valid = d >= 0
    if window is not None:
        valid &= d < window
    near = valid & (d < 2 * KT)
    lo = int(np.clip(d[0], 0, 2 * KT))
    hi = int(np.clip(d[-1] + 1, 0, 2 * KT))
    seg = tab[lo:hi]
    r = jnp.zeros((2 * KT - 1, NSA_HEADS), F32)
    if hi > lo:
        r = lax.dynamic_update_slice(r, seg, (int(np.nonzero(d == lo)[0][0]), 0))
    r = jnp.where(jnp.asarray(near)[:, None], r, 0.0)
    r = jnp.where(jnp.asarray(valid)[:, None], r, NEG)
    rp = jnp.concatenate([r, jnp.zeros((1, NSA_HEADS), F32)], axis=0).T
    skew = jnp.broadcast_to(rp[:, None, :], (NSA_HEADS, KT, 2 * KT)).reshape(NSA_HEADS, KT * 2 * KT)
    skew = skew[:, :KT * (2 * KT - 1)].reshape(NSA_HEADS, KT, 2 * KT - 1)
    vals = skew[:, :, KT - 1:]
    return vals.reshape(NSA_KV_HEADS, NSA_GROUP, KT, rows_q).transpose(0, 2, 1, 3).reshape(NSA_KV_HEADS, KT, NSA_GROUP * rows_q)


def _cmp_rowbias(tab, pos, per_row_head):
    b = np.asarray(pos) % L_CMP
    cols = []
    for m in range(-1, 4):
        if m < 0:
            v = jnp.where(jnp.asarray(b == L_CMP - 1)[:, None], tab[0][None, :], NEG)
        else:
            v = tab[jnp.asarray(L_CMP * m + b + 1)]
        cols.append(v)
    r = jnp.stack(cols, axis=1)
    if per_row_head is None:
        r = r.reshape(r.shape[0], 5 * NSA_HEADS)
        return jnp.pad(r, ((0, 0), (0, LANES - 5 * NSA_HEADS)))
    r = r[jnp.arange(r.shape[0]), :, jnp.asarray(per_row_head)]
    return jnp.pad(r, ((0, 0), (0, LANES - 5)))


def _cmp_bias(dist, col):
    b = jnp.where(dist <= 4 * L_CMP, col(3), 0.0)
    b = jnp.where(dist <= 3 * L_CMP, col(2), b)
    b = jnp.where(dist <= 2 * L_CMP, col(1), b)
    b = jnp.where(dist <= L_CMP, col(0), b)
    return jnp.where(dist <= 0, col(-1), b)


def _select_mask(imp, sblk, qpos, k_top, axis):
    forced = (sblk == 0) | (sblk == qpos // L_SEL)
    avail = sblk * L_SEL <= qpos
    w = jnp.where(forced, FORCE_SCORE, imp)
    w = jnp.where(avail, w, -jnp.inf)
    idx = lax.broadcasted_iota(I32, w.shape, axis).astype(F32)
    for _ in range(k_top):
        mx = jnp.max(w, axis=axis, keepdims=True)
        first = jnp.min(jnp.where(w == mx, idx, float(4 * LANES)), axis=axis, keepdims=True)
        w = jnp.where(idx == first, -jnp.inf, w)
    return jnp.where(avail & (w == -jnp.inf), 0.0, NEG)


def _cmpkv_kernel(kv_ref, pe_ref, wk_ref, wv_ref, kc_ref, vc_ref):
    x = kv_ref[...]
    nblk = x.shape[0] // L_CMP
    cm = x.reshape(nblk, L_CMP, 2 * NSA_KV).sum(axis=1) * (1.0 / L_CMP)
    cm = cm + jnp.mean(pe_ref[...], axis=0, keepdims=True)
    kc_ref[...] = _dot(cm[:, :NSA_KV].astype(BF16), wk_ref[...])
    vc_ref[...] = _dot(cm[:, NSA_KV:].astype(BF16), wv_ref[...])


def _cmpkv(z, pe2, bd_ck, bd_cv, *, n, t):
    nblk = t // L_CMP
    z3 = z.reshape(n, t, NP)
    return pl.pallas_call(
        _cmpkv_kernel,
        out_shape=(jax.ShapeDtypeStruct((n, nblk, NSA_KV), F32), jax.ShapeDtypeStruct((n, nblk, NSA_KV), F32)),
        grid=(n,),
        in_specs=[pl.BlockSpec((None, nblk * L_CMP, 2 * NSA_KV), lambda i: (i, 0, C_KV // (2 * NSA_KV))),
                  pl.BlockSpec((L_CMP, 2 * NSA_KV), lambda i: (0, 0)),
                  pl.BlockSpec((NSA_KV, NSA_KV), lambda i: (0, 0)),
                  pl.BlockSpec((NSA_KV, NSA_KV), lambda i: (0, 0))],
        out_specs=(pl.BlockSpec((None, nblk, NSA_KV), lambda i: (i, 0, 0)),
                   pl.BlockSpec((None, nblk, NSA_KV), lambda i: (i, 0, 0))),
        compiler_params=_cparams(("parallel",)),
        name="cmp_kv",
    )(z3, pe2, bd_ck, bd_cv)


def _attn_store(o_ref, g, ot):
    for k in range(NSA_GROUP // 2):
        c0 = (g * NSA_GROUP + 2 * k) * NSA_DH
        o_ref[:, c0:c0 + 2 * NSA_DH] = jnp.concatenate(
            [ot[:, 2 * k * KT:(2 * k + 1) * KT], ot[:, (2 * k + 1) * KT:(2 * k + 2) * KT]], axis=0).T


def _cmpsel_kernel(q_ref, kc_ref, vct_ref, rbt_ref, cpos_ref, sblk_ref, oc_ref, mn_ref, s_scr, *, tq, k_top):
    gq = NSA_GROUP * tq
    qs = pl.program_id(1) * tq
    qpos = lax.broadcasted_iota(I32, (1, gq), 1) % tq + qs
    valid = qpos - cpos_ref[...] >= 0
    kc = kc_ref[...].astype(BF16)
    vct = vct_ref[...].astype(BF16)
    w0 = jnp.clip((2 * pl.program_id(1) - 2) // 8 * 8, 0, LANES - NEAR)
    imps = []
    for g in range(NSA_KV_HEADS):
        qg = q_ref[:, g * NSA_GROUP * NSA_DH:(g + 1) * NSA_GROUP * NSA_DH] * (NSA_DH ** -0.5)
        qt = jnp.concatenate([qg[:, k * LANES:(k + 1) * LANES].T for k in range(NSA_GROUP // 2)], axis=0)
        qt = jnp.concatenate([qt[r * NSA_DH:(r + 1) * NSA_DH] for r in range(NSA_GROUP)], axis=1).astype(BF16)
        s_scr[...] = _dot(kc[:, g * NSA_DH:(g + 1) * NSA_DH], qt)
        rbt = rbt_ref[g]
        for half in range(2):
            rows_w = pl.ds(pl.multiple_of(half * LANES + w0, 8), NEAR)
            dw = qpos - cpos_ref[rows_w, :]
            s_scr[rows_w, :] = s_scr[rows_w, :] + _cmp_bias(dw, lambda m: rbt[m + 1:m + 2, :])
        s = jnp.where(valid, s_scr[...], M_INIT)
        mx = jnp.max(s, axis=0, keepdims=True)
        e = jnp.where(valid, jnp.exp(s - mx), 0.0)
        p = e / jnp.maximum(jnp.sum(e, axis=0, keepdims=True), 1e-30)
        _attn_store(oc_ref, g, _dot(vct[g * NSA_DH:(g + 1) * NSA_DH, :], p.astype(BF16)))
        imp = p[:, 0:tq]
        for r in range(1, NSA_GROUP):
            imp = imp + p[:, r * tq:(r + 1) * tq]
        imps.append(imp[:LANES] + imp[LANES:])
    mn = _select_mask(jnp.concatenate(imps, axis=1), sblk_ref[...], qpos[:, 0:NSA_KV_HEADS * tq], k_top, 0).astype(BF16)
    for g in range(NSA_KV_HEADS):
        mn_ref[g] = mn[:, g * tq:(g + 1) * tq]


def _cmpsel(z, kcp, vctp, rowbt, cpos, sblk, *, n, t, k_top):
    tq = KT
    assert tq == 4 * L_CMP and 4 * L_CMP == KT
    nqt = t // tq
    gq = NSA_GROUP * tq
    kern = functools.partial(_cmpsel_kernel, tq=tq, k_top=k_top)
    return pl.pallas_call(
        kern,
        out_shape=(jax.ShapeDtypeStruct((n * t, NSA_WIDTH), F32),
                   jax.ShapeDtypeStruct((n, NSA_KV_HEADS, LANES, t), BF16)),
        grid=(n, nqt),
        in_specs=[pl.BlockSpec((tq, NSA_WIDTH), lambda b, i: (b * nqt + i, C_NQ // NSA_WIDTH)),
                  pl.BlockSpec((None, 2 * LANES, NSA_KV), lambda b, i: (b, 0, 0)),
                  pl.BlockSpec((None, NSA_KV, 2 * LANES), lambda b, i: (b, 0, 0)),
                  pl.BlockSpec((NSA_KV_HEADS, 8, gq), lambda b, i: (0, 0, 0)),
                  pl.BlockSpec((2 * LANES, gq), lambda b, i: (0, 0)),
                  pl.BlockSpec((LANES, NSA_KV_HEADS * tq), lambda b, i: (0, 0))],
        out_specs=(pl.BlockSpec((tq, NSA_WIDTH), lambda b, i: (b * nqt + i, 0)),
                   pl.BlockSpec((None, NSA_KV_HEADS, LANES, tq), lambda b, i: (b, 0, 0, i))),
        scratch_shapes=[pltpu.VMEM((2 * LANES, gq), F32)],
        compiler_params=_cparams(("parallel", "parallel")),
        name="cmp_select",
    )(z, kcp, vctp, rowbt, cpos, sblk)


def _kvpack_kernel(z_ref, oh_ref, *refs, nqt):
    ks_ref, vs_ref, vs4_ref, kw_ref, vw_ref, ct_ref, st_ref, wt_ref = refs[-8:]
    i = pl.program_id(1) * TPS
    real = (i >= KPAD) & (i < KPAD + nqt)

    for u in range(TPS):
        rows = slice(u * KT, (u + 1) * KT)
        kv = z_ref[rows, :]
        kvt = [kv[:, j * NSA_KV:(j + 1) * NSA_KV].T for j in range(6)]
        for o_ref, j in ((ct_ref, 0), (st_ref, 2), (wt_ref, 4)):
            o_ref[0:NSA_KV, rows] = kvt[j]
            o_ref[NSA_KV:2 * NSA_KV, rows] = kvt[j + 1]

        @pl.when(real)
        def _(u=u, rows=rows, kv=kv, kvt=kvt):
            for k_ref, v_refs, j, lead in ((ks_ref, (vs_ref, vs4_ref.at[:, :, rows]), 2, oh_ref[rows, :]),
                                           (kw_ref, (vw_ref,), 4, jnp.zeros((KT, LANES), BF16))):
                vt = kvt[j + 1].astype(BF16)
                for g in range(NSA_KV_HEADS):
                    k_ref[g, u, :, 0:LANES] = lead
                    k_ref[g, u, :, LANES:LANES + NSA_DH] = kv[:, j * NSA_KV + g * NSA_DH:j * NSA_KV + (g + 1) * NSA_DH].astype(BF16)
                    vg = vt[g * NSA_DH:(g + 1) * NSA_DH, :]
                    v_refs[0][g, u] = vg
                    if len(v_refs) > 1:
                        v_refs[1][g] = vg

    @pl.when(jnp.logical_not(real))
    def _():
        for ref in (ks_ref, vs_ref, vs4_ref, kw_ref, vw_ref):
            ref[...] = jnp.zeros(ref.shape, BF16)


def _kvpack(z, onehot, *, n, t, layer, depth, bufs):
    nqt = t // KT
    ntile = nqt + 2 * KPAD
    assert ntile % CHUNK == 0 and KPAD % TPS == 0 and nqt % TPS == 0 and CHUNK % TPS == 0
    nstep, cps = ntile // TPS, CHUNK // TPS
    tile = lambda i: jnp.clip(i - KPAD // TPS, 0, nqt // TPS - 1)
    kshape = jax.ShapeDtypeStruct((n, NSA_KV_HEADS, ntile, KT, LANES + NSA_DH), BF16)
    vshape = jax.ShapeDtypeStruct((n, NSA_KV_HEADS, ntile, NSA_DH, KT), BF16)
    v4shape = jax.ShapeDtypeStruct((n, NSA_KV_HEADS, ntile // CHUNK, NSA_DH, CHUNK * KT), BF16)
    rshape = jax.ShapeDtypeStruct((depth, n, 2 * NSA_KV, t), F32)
    kspec = pl.BlockSpec((None, NSA_KV_HEADS, TPS, KT, LANES + NSA_DH), lambda b, i: (b, 0, i, 0, 0))
    vspec = pl.BlockSpec((None, NSA_KV_HEADS, TPS, NSA_DH, KT), lambda b, i: (b, 0, i, 0, 0))
    v4spec = pl.BlockSpec((None, NSA_KV_HEADS, None, NSA_DH, TPS * KT), lambda b, i: (b, 0, i // cps, 0, i % cps))
    rspec = pl.BlockSpec((None, None, 2 * NSA_KV, TPS * KT), lambda b, i: (layer, b, 0, tile(i)))
    extra, extra_specs, alias = (), [], {}
    if bufs is not None:
        extra, extra_specs, alias = tuple(bufs), [pl.BlockSpec(memory_space=pl.ANY)] * 3, {2: 5, 3: 6, 4: 7}
    return pl.pallas_call(
        functools.partial(_kvpack_kernel, nqt=nqt),
        out_shape=(kshape, vshape, v4shape, kshape, vshape, rshape, rshape, rshape),
        grid=(n, nstep),
        in_specs=[pl.BlockSpec((TPS * KT, 1024), lambda b, i: (b * (nqt // TPS) + tile(i), C_KV // 1024)),
                  pl.BlockSpec((TPS * KT, LANES), lambda b, i: (tile(i), 0))] + extra_specs,
        out_specs=(kspec, vspec, v4spec, kspec, vspec, rspec, rspec, rspec),
        input_output_aliases=alias,
        compiler_params=_cparams(("parallel", "arbitrary")),
        name="kv_pack",
    )(z, onehot, *extra)


def _qk_chunk(k_ref, t0, ntile, qpt_ref):
    k = k_ref[pl.ds(t0, ntile)].reshape(ntile * KT, LANES + NSA_DH)
    return _dot(k, qpt_ref[...])


def _online_softmax(st, pv, m, l, acc):
    m_new = jnp.maximum(m, jnp.max(st, axis=0, keepdims=True))
    alpha = jnp.exp2(m - m_new)
    p = jnp.exp2(st - m_new)
    l_new = alpha * l + jnp.sum(p, axis=0, keepdims=True)
    return m_new, l_new, alpha * acc + pv(p.astype(BF16))


def _pv_tiles(v_ref, t0, ntile):
    def pv(pb):
        out = _dot(v_ref[t0], pb[0:KT])
        for j in range(1, ntile):
            out = out + _dot(v_ref[t0 + j], pb[j * KT:(j + 1) * KT])
        return out
    return pv


def _selwin_kernel(q_ref, mn_ref, ks_ref, vs_ref, vs4_ref, kw_ref, vw_ref, asel_ref, awin_ref, os_ref, ow_ref,
                   qpt_scr, sta_scr, stb_scr, m_scr, l_scr, acc_scr):
    qt = pl.program_id(1)
    heads = range(NSA_KV_HEADS)
    gq = NSA_GROUP * KT
    for g in heads:
        mn = mn_ref[g]
        for k in range(NSA_GROUP // 2):
            c0 = (g * NSA_GROUP + 2 * k) * NSA_DH
            qt2 = (q_ref[:, c0:c0 + 2 * NSA_DH] * (NSA_DH ** -0.5 * LOG2E)).T.astype(BF16)
            for u in range(2):
                r = 2 * k + u
                qpt_scr[g, 0:LANES, r * KT:(r + 1) * KT] = mn
                qpt_scr[g, LANES:LANES + NSA_DH, r * KT:(r + 1) * KT] = qt2[u * NSA_DH:(u + 1) * NSA_DH]
        m_scr[g] = jnp.full((1, gq), M_INIT, F32)
        l_scr[g] = jnp.zeros((1, gq), F32)
        acc_scr[g] = jnp.zeros((NSA_DH, gq), F32)

    nplain = jnp.maximum(qt - 1, 0) // CHUNK

    def scores(st_scr, c):
        for g in heads:
            st_scr[g] = _qk_chunk(ks_ref.at[g], KPAD + CHUNK * c, CHUNK, qpt_scr.at[g])

    def update(st_scr, c):
        for g in heads:
            m, l, acc = _online_softmax(st_scr[g], lambda pb, g=g: _dot(vs4_ref[g, KPAD // CHUNK + c], pb),
                                        m_scr[g], l_scr[g], acc_scr[g])
            m_scr[g] = m
            l_scr[g] = l
            acc_scr[g] = acc

    @pl.when(nplain > 0)
    def _():
        scores(sta_scr, 0)

    def pair(i, carry):
        c0 = 2 * i
        scores(stb_scr, c0 + 1)
        update(sta_scr, c0)
        scores(sta_scr, jnp.minimum(c0 + 2, nplain - 1))
        update(stb_scr, c0 + 1)
        return carry

    lax.fori_loop(0, nplain // 2, pair, 0)

    @pl.when(nplain % 2 == 1)
    def _():
        update(sta_scr, nplain - 1)

    e0 = CHUNK * nplain
    off = pl.multiple_of((EDGE - (qt - e0)) * KT, KT)
    for g in heads:
        st = _qk_chunk(ks_ref.at[g], KPAD + e0, EDGE, qpt_scr.at[g]) + asel_ref[g, pl.ds(off, EDGE * KT), :]
        _, l, acc = _online_softmax(st, _pv_tiles(vs_ref.at[g], KPAD + e0, EDGE), m_scr[g], l_scr[g], acc_scr[g])
        _attn_store(os_ref, g, acc / l)

    for g in heads:
        st = _qk_chunk(kw_ref.at[g], qt, EDGE, qpt_scr.at[g]) + awin_ref[g]
        rows = []
        for j in range(EDGE):
            blk = st[j * KT:(j + 1) * KT]
            if j < EDGE - 1:
                blk = blk + jnp.where(qt >= EDGE - 1 - j, 0.0, NEG)
            rows.append(blk)
        st = jnp.concatenate(rows, axis=0)
        _, l, acc = _online_softmax(st, _pv_tiles(vw_ref.at[g], qt, EDGE), jnp.full((1, gq), M_INIT, F32),
                                    jnp.zeros((1, gq), F32), jnp.zeros((NSA_DH, gq), F32))
        _attn_store(ow_ref, g, acc / l)


def _selwin(z, mnt, ks, vs, vs4, kw, vw, asel, awin, *, n, t):
    nqt = t // KT
    ntile = nqt + 2 * KPAD
    gq = NSA_GROUP * KT
    hk = NSA_KV_HEADS
    once = pl.Buffered(1)
    kspec = pl.BlockSpec((None, hk, ntile, KT, LANES + NSA_DH), lambda b, i: (b, 0, 0, 0, 0), pipeline_mode=once)
    vspec = pl.BlockSpec((None, hk, ntile, NSA_DH, KT), lambda b, i: (b, 0, 0, 0, 0), pipeline_mode=once)
    v4spec = pl.BlockSpec((None, hk, ntile // CHUNK, NSA_DH, CHUNK * KT), lambda b, i: (b, 0, 0, 0, 0), pipeline_mode=once)
    ospec = pl.BlockSpec((KT, NSA_WIDTH), lambda b, i: (b * nqt + i, 0))
    return pl.pallas_call(
        _selwin_kernel,
        out_shape=(jax.ShapeDtypeStruct((n * t, NSA_WIDTH), F32), jax.ShapeDtypeStruct((n * t, NSA_WIDTH), F32)),
        grid=(n, nqt),
        in_specs=[pl.BlockSpec((KT, NSA_WIDTH), lambda b, i: (b * nqt + i, C_NQ // NSA_WIDTH)),
                  pl.BlockSpec((None, hk, LANES, KT), lambda b, i: (b, 0, 0, i)),
                  kspec, vspec, v4spec, kspec, vspec,
                  pl.BlockSpec((hk, 2 * EDGE * KT, gq), lambda b, i: (0, 0, 0), pipeline_mode=once),
                  pl.BlockSpec((hk, EDGE * KT, gq), lambda b, i: (0, 0, 0), pipeline_mode=once)],
        out_specs=(ospec, ospec),
        scratch_shapes=[pltpu.VMEM((hk, LANES + NSA_DH, gq), BF16),
                        pltpu.VMEM((hk, CHUNK * KT, gq), F32),
                        pltpu.VMEM((hk, CHUNK * KT, gq), F32),
                        pltpu.VMEM((hk, 1, gq), F32),
                        pltpu.VMEM((hk, 1, gq), F32),
                        pltpu.VMEM((hk, NSA_DH, gq), F32)],
        compiler_params=_cparams(("parallel", "arbitrary")),
        name="sel_win",
    )(z, mnt, ks, vs, vs4, kw, vw, asel, awin)


def _nsa_prompt(z, pe2, bd_ck, bd_cv, tab, *, n, t, layer, depth, row_bufs):
    assert t % KT == 0 and WINDOW == (EDGE - 1) * KT and KT % L_SEL == 0 and EDGE <= KPAD + 1 and EDGE <= CHUNK + 1
    nblk = t // L_CMP
    n_sel = -(-t // L_SEL)
    assert nblk <= 2 * LANES and n_sel <= LANES and nblk % 2 == 0
    kc, vc = _cmpkv(z, pe2, bd_ck, bd_cv, n=n, t=t)
    half = nblk // 2

    def perm(a):
        z0 = jnp.zeros((n, LANES - half, NSA_KV), F32)
        return jnp.concatenate([a[:, 0::2], z0, a[:, 1::2], z0], axis=1)

    slot = np.arange(LANES)
    big = 1 << 20
    cpos = np.concatenate([np.where(slot < half, (2 * slot) * L_CMP + L_CMP - 1, big),
                           np.where(slot < half, (2 * slot + 1) * L_CMP + L_CMP - 1, big)])
    cpos = np.broadcast_to(cpos[:, None], (2 * LANES, NSA_GROUP * KT))
    sblk = np.broadcast_to(np.where(slot < n_sel, slot, big)[:, None], (LANES, NSA_KV_HEADS * KT))
    rowb = _cmp_rowbias(tab, np.arange(KT), None)[:, :5 * NSA_HEADS].reshape(KT, 5, NSA_KV_HEADS, NSA_GROUP)
    rowbt = jnp.pad(rowb.transpose(2, 1, 3, 0).reshape(NSA_KV_HEADS, 5, NSA_GROUP * KT), ((0, 0), (0, 3), (0, 0)))
    o_cmp, mnt = _cmpsel(z, perm(kc), perm(vc).transpose(0, 2, 1), rowbt, jnp.asarray(cpos, I32), jnp.asarray(sblk, I32),
                         n=n, t=t, k_top=min(N_SEL, n_sel))

    onehot = jnp.asarray(np.arange(t)[:, None] // L_SEL == np.arange(LANES)[None, :], BF16)
    ks, vs, vs4, kw, vw, *row_bufs = _kvpack(z, onehot, n=n, t=t, layer=layer, depth=depth, bufs=row_bufs)
    a0 = _toeplitz_T(tab, 0, KT) * LOG2E
    a1 = _toeplitz_T(tab, KT, KT) * LOG2E
    aw = _toeplitz_T(tab, WINDOW, KT, window=WINDOW) * LOG2E
    zeros = jnp.zeros_like(a0)
    asel = jnp.concatenate([zeros] * (EDGE - 1) + [a1, a0] + [jnp.full_like(a0, NEG)] * (EDGE - 1), axis=1)
    awin = jnp.concatenate([aw] + [zeros] * (EDGE - 3) + [a1, a0], axis=1)
    o_sel, o_win = _selwin(z, mnt, ks, vs, vs4, kw, vw, asel, awin, n=n, t=t)
    return o_cmp, o_sel, o_win, row_bufs


def _nsa_sample_kernel(pt_ref, *refs, npg, t, past, wbuf, k_top, aliased, ns):
    del pt_ref
    cpages = refs[0:ns * npg]
    spages = refs[ns * npg:2 * ns * npg]
    rest = refs[2 * ns * npg:]
    (q_ref, kvn_ref, cw_ref, pe_ref, wk_ref, wv_ref, rb_ref, cpos_ref, sblk_ref, oht_ref, asel_ref, awin_ref) = rest[:12]
    (oc_ref, os_ref, ow_ref, wout_ref, kselt, vselt, kwint, vwint, cm_scr, kcv_scr, qw_scr, qp_scr) = rest[12 + aliased:]
    rows = NSA_HEADS * t
    half = NSA_GROUP * t
    bpp = PAGE_SIZE // L_CMP
    nblk = npg * bpp
    keep = wout_ref.shape[2]

    @pl.when(pl.program_id(0) == 0)
    def _():
        for s in range(ns):
            kselt[s, 0:LANES, :] = oht_ref[...]
        kcv_scr[...] = jnp.zeros(kcv_scr.shape, F32)
        qw_scr[...] = jnp.zeros(qw_scr.shape, F32)

    def pick(x):
        row = lax.broadcasted_iota(I32, (rows, NSA_DH), 0)
        return jnp.where(row < half, x[:, :NSA_DH], x[:, NSA_DH:])

    def store(o_ref, s, x):
        for h in range(NSA_HEADS):
            o_ref[s * t:(s + 1) * t, h * NSA_DH:(h + 1) * NSA_DH] = x[h * t:(h + 1) * t, :]

    qpos = lax.broadcasted_iota(I32, (rows, 1), 0) % t + past
    dist = qpos - cpos_ref[...]
    valid = dist >= 0
    rb = rb_ref[...]
    qws, imps = [], []
    for s in range(ns):
        kvn = kvn_ref[s * t:(s + 1) * t, :]
        kvpad = jnp.concatenate([kvn[:, 0:6 * NSA_KV], jnp.zeros((KT - t, 6 * NSA_KV), F32)], axis=0)
        new_t = [kvpad[:, j * NSA_KV:(j + 1) * NSA_KV].T for j in range(2, 6)]

        for p in range(npg):
            sp, cp = spages[s * npg + p], cpages[s * npg + p]
            kselt[s, LANES:2 * LANES, p * PAGE_SIZE:(p + 1) * PAGE_SIZE] = sp[0:NSA_KV, :].astype(BF16)
            vselt[s, :, p * PAGE_SIZE:(p + 1) * PAGE_SIZE] = sp[NSA_KV:2 * NSA_KV, :].astype(BF16)
            for c in range(2):
                blk = cp[c * NSA_KV:(c + 1) * NSA_KV, :].T
                cm_scr[s, p * bpp:(p + 1) * bpp, c * NSA_KV:(c + 1) * NSA_KV] = blk.reshape(bpp, L_CMP, NSA_KV).sum(axis=1) * (1.0 / L_CMP)
        kselt[s, LANES:2 * LANES, past:past + KT] = new_t[0].astype(BF16)
        vselt[s, :, past:past + KT] = new_t[1].astype(BF16)
        cw = cw_ref[s]
        kwint[s, :, 0:wbuf] = cw[0:NSA_KV, :].astype(BF16)
        vwint[s, :, 0:wbuf] = cw[NSA_KV:2 * NSA_KV, :].astype(BF16)
        kwint[s, :, wbuf:wbuf + KT] = new_t[2].astype(BF16)
        vwint[s, :, wbuf:wbuf + KT] = new_t[3].astype(BF16)
        for c in range(2):
            wout_ref[s, c * NSA_KV:(c + 1) * NSA_KV, :] = jnp.concatenate(
                [cw[c * NSA_KV:(c + 1) * NSA_KV, wbuf + t - keep:wbuf], new_t[2 + c][:, 0:t]], axis=1)

        for h in range(NSA_HEADS):
            g = h // NSA_GROUP
            qh = q_ref[s * t:(s + 1) * t, h * NSA_DH:(h + 1) * NSA_DH] * (NSA_DH ** -0.5)
            qw_scr[s, h * t:(h + 1) * t, g * NSA_DH:(g + 1) * NSA_DH] = qh
        qw = qw_scr[s].astype(BF16)
        qws.append(qw)

        cm = cm_scr[s] + jnp.mean(pe_ref[...], axis=0, keepdims=True)
        kcv_scr[s, 0:nblk, 0:NSA_KV] = _dot(cm[:, :NSA_KV].astype(BF16), wk_ref[...])
        kcv_scr[s, 0:nblk, NSA_KV:] = _dot(cm[:, NSA_KV:].astype(BF16), wv_ref[...])
        kc = kcv_scr[s, :, 0:NSA_KV].astype(BF16)
        vc = kcv_scr[s, :, NSA_KV:].astype(BF16)
        sc = _nt(qw, kc) + _cmp_bias(dist, lambda m: rb[:, m + 1:m + 2])
        sc = jnp.where(valid, sc, M_INIT)
        mx = jnp.max(sc, axis=-1, keepdims=True)
        e = jnp.where(valid, jnp.exp(sc - mx), 0.0)
        p = e / jnp.maximum(jnp.sum(e, axis=-1, keepdims=True), 1e-30)
        store(oc_ref, s, pick(_dot(p.astype(BF16), vc)))

        for g in range(NSA_KV_HEADS):
            imp = p[g * half:g * half + t]
            for r in range(1, NSA_GROUP):
                imp = imp + p[g * half + r * t:g * half + (r + 1) * t]
            imps.append(imp + pltpu.roll(imp, LANES - 1, 1))

    imps.append(jnp.zeros((LANES - ns * NSA_KV_HEADS * t, LANES), F32))
    imp_t = jnp.concatenate(imps, axis=0).T
    qpos_l = lax.broadcasted_iota(I32, (1, LANES), 1) % t + past
    mn = _select_mask(imp_t, sblk_ref[...], qpos_l, k_top, 0).T.astype(BF16)

    def attend(qmat, kt_all, vt_all, add):
        sc = _dot(qmat, kt_all) + add
        mm = jnp.max(sc, axis=-1, keepdims=True)
        ee = jnp.exp(sc - mm)
        ll = jnp.sum(ee, axis=-1, keepdims=True)
        return pick(_nt(ee.astype(BF16), vt_all)) / ll

    for s in range(ns):
        for g in range(NSA_KV_HEADS):
            row0 = (s * NSA_KV_HEADS + g) * t
            for r in range(NSA_GROUP):
                qp_scr[s, g * half + r * t:g * half + (r + 1) * t, 0:LANES] = mn[row0:row0 + t]
        qp_scr[s, :, LANES:2 * LANES] = qws[s]
        store(os_ref, s, attend(qp_scr[s], kselt[s], vselt[s], asel_ref[...]))
        store(ow_ref, s, attend(qws[s], kwint[s], vwint[s], awin_ref[...]))


def _nsa_sample(z, cache_cmp_t, cache_sel_t, cache_win_t, page_table, pe2, bd_ck, bd_cv, tab, *, n, t, past, layer, win_buf):
    npg = page_table.shape[1]
    depth = cache_win_t.shape[0]
    assert past == npg * PAGE_SIZE and past % L_SEL == 0 and past % L_CMP == 0 and PAGE_SIZE == KT
    wbuf = cache_win_t.shape[3]
    keep = min(WINDOW, wbuf + t)
    nblk = past // L_CMP
    assert t < L_CMP and nblk <= LANES and wbuf % KT == 0 and keep > t
    length = past + t
    n_sel = -(-length // L_SEL)
    ns = 2 if n % 2 == 0 else 1
    assert 2 * (n_sel - 1) < LANES and ns * NSA_KV_HEADS * t <= LANES
    rows = NSA_HEADS * t
    lsel = past + KT
    lwin = wbuf + KT
    big = 1 << 20
    slot = np.arange(LANES)
    cpos = np.where(slot < nblk, slot * L_CMP + L_CMP - 1, big)[None, :]
    sblk = np.broadcast_to(np.where((slot % 2 == 0) & (slot // 2 < n_sel), slot // 2, big)[:, None], (LANES, LANES))
    head_of_row = np.repeat(np.arange(NSA_HEADS), t)
    tok_of_row = np.tile(np.arange(t), NSA_HEADS)
    rowb = _cmp_rowbias(tab, past + tok_of_row, head_of_row)
    kpos_sel = np.arange(lsel)
    onehot_t = jnp.asarray((kpos_sel[None, :] < length) & (2 * (kpos_sel[None, :] // L_SEL) == slot[:, None]), BF16)

    def additive(kpos, exists, window):
        d = (past + np.arange(t))[:, None] - kpos[None, :]
        valid = (d >= 0) & exists[None, :]
        if window is not None:
            valid &= d < window
        near = valid & (d < 2 * KT)
        j0 = int(np.nonzero(near.any(axis=0))[0].min()) // KT * KT
        vals = tab[jnp.asarray(np.clip(d[:, j0:], 0, 2 * KT - 1))]
        vals = jnp.where(jnp.asarray(near[:, j0:])[:, :, None], vals, 0.0)
        vals = jnp.pad(vals, ((0, 0), (j0, 0), (0, 0)))
        vals = jnp.where(jnp.asarray(valid)[:, :, None], vals, NEG)
        return vals.transpose(2, 0, 1).reshape(NSA_HEADS * t, kpos.shape[0])

    asel = additive(kpos_sel, kpos_sel < length, None)
    kidx = np.arange(lwin)
    awin = additive(past - wbuf + kidx, kidx < wbuf + t, WINDOW)

    pages = [(s, p) for s in range(ns) for p in range(npg)]
    page_spec = lambda s, p: pl.BlockSpec((None, None, 2 * NSA_KV, PAGE_SIZE),
                                          lambda i, pt, s=s, p=p: (layer, pt[ns * i + s, p], 0, 0))
    const = lambda shape: pl.BlockSpec(shape, lambda i, pt: tuple(0 for _ in shape))
    ospec = pl.BlockSpec((ns * t, NSA_WIDTH), lambda i, pt: (i, 0))
    aliased = win_buf is not None
    extra, extra_specs, alias = (), [], {}
    if aliased:
        extra, extra_specs, alias = (win_buf,), [pl.BlockSpec(memory_space=pl.ANY)], {1 + 2 * ns * npg + 12: 3}
    kern = functools.partial(_nsa_sample_kernel, npg=npg, t=t, past=past, wbuf=wbuf, k_top=min(N_SEL, n_sel),
                             aliased=int(aliased), ns=ns)
    grid_spec = pltpu.PrefetchScalarGridSpec(
        num_scalar_prefetch=1,
        grid=(n // ns,),
        in_specs=[page_spec(s, p) for s, p in pages] + [page_spec(s, p) for s, p in pages] + [
            pl.BlockSpec((ns * t, NSA_WIDTH), lambda i, pt: (i, C_NQ // NSA_WIDTH)),
            pl.BlockSpec((ns * t, 1024), lambda i, pt: (i, C_KV // 1024)),
            pl.BlockSpec((None, ns, 2 * NSA_KV, wbuf), lambda i, pt: (layer, i, 0, 0)),
            const((L_CMP, 2 * NSA_KV)), const((NSA_KV, NSA_KV)), const((NSA_KV, NSA_KV)),
            const((rows, LANES)), const((1, LANES)), const((LANES, LANES)),
            const((LANES, lsel)), const((rows, lsel)), const((rows, lwin))] + extra_specs,
        out_specs=(ospec, ospec, ospec, pl.BlockSpec((None, ns, 2 * NSA_KV, keep), lambda i, pt: (layer, i, 0, 0))),
        scratch_shapes=[pltpu.VMEM((ns, 2 * LANES, lsel), BF16), pltpu.VMEM((ns, LANES, lsel), BF16),
                        pltpu.VMEM((ns, LANES, lwin), BF16), pltpu.VMEM((ns, LANES, lwin), BF16),
                        pltpu.VMEM((ns, nblk, 2 * NSA_KV), F32), pltpu.VMEM((ns, LANES, 2 * NSA_KV), F32),
                        pltpu.VMEM((ns, rows, NSA_KV), F32),
                        pltpu.VMEM((ns, rows, 2 * LANES), BF16)])
    o_cmp, o_sel, o_win, win_new = pl.pallas_call(
        kern,
        grid_spec=grid_spec,
        out_shape=(jax.ShapeDtypeStruct((n * t, NSA_WIDTH), F32),) * 3
        + (jax.ShapeDtypeStruct((depth, n, 2 * NSA_KV, keep), F32),),
        input_output_aliases=alias,
        compiler_params=_cparams(("arbitrary",)),
        name="nsa_sample",
    )(page_table, *([cache_cmp_t] * (ns * npg)), *([cache_sel_t] * (ns * npg)), z, z, cache_win_t, pe2, bd_ck, bd_cv, rowb,
      jnp.asarray(cpos, I32), jnp.asarray(sblk, I32), onehot_t, asel, awin, *extra)
    return o_cmp, o_sel, o_win, win_new


def _merge_kernel(x_ref, mg_ref, ga_ref, gb_ref, oc_ref, os_ref, ow_ref, nbg_ref, ng_ref,
                  wa_ref, wb_ref, wc_ref, wo_ref, fg_ref, y_ref, *, final):
    tm = x_ref.shape[0]
    gates = jax.nn.sigmoid(nbg_ref[...])
    lane = lax.broadcasted_iota(I32, (tm, LANES), 1)
    low = lane < NSA_DH
    parts = []
    for pi in range(NSA_HEADS // 2):
        cols = slice(pi * LANES, (pi + 1) * LANES)
        acc = jnp.zeros((tm, LANES), F32)
        for b, o_ref in enumerate((oc_ref, os_ref, ow_ref)):
            c0 = b * NSA_HEADS + 2 * pi
            gt = jnp.where(low, gates[:, c0:c0 + 1], gates[:, c0 + 1:c0 + 2])
            acc = acc + gt * o_ref[:, cols]
        parts.append((_silu(ng_ref[:, cols]) * acc).astype(BF16))
    gc = jnp.concatenate(parts, axis=1)
    b_a = _dot(ga_ref[...], wa_ref[...])
    b_b = _dot(gb_ref[...], wb_ref[...])
    b_c = _dot(gc, wc_ref[...])
    mg = mg_ref[...]
    merged = (jax.nn.sigmoid(mg[:, 0:D_MODEL]) * b_a + jax.nn.sigmoid(mg[:, D_MODEL:2 * D_MODEL]) * b_b
              + jax.nn.sigmoid(mg[:, 2 * D_MODEL:]) * b_c)
    y = x_ref[...] + _dot(merged.astype(BF16), wo_ref[...])
    if final:
        y = y * lax.rsqrt(jnp.mean(y * y, axis=-1, keepdims=True) + EPS) * fg_ref[...]
    y_ref[...] = y


def _merge(x2, z, ga, gb, o_cmp, o_sel, o_win, wa, wb, wc, wo, fgain, *, final):
    m = x2.shape[0]
    tm = min(512, m)
    row = lambda w, c=0: pl.BlockSpec((tm, w), lambda i, c=c: (i, c))
    const = lambda a: pl.BlockSpec(a.shape, lambda i: (0, 0), pipeline_mode=pl.Buffered(1))
    fg = fgain.reshape(1, D_MODEL)
    return pl.pallas_call(
        functools.partial(_merge_kernel, final=final),
        out_shape=jax.ShapeDtypeStruct((m, D_MODEL), F32),
        grid=(m // tm,),
        in_specs=[row(D_MODEL), row(3 * D_MODEL, C_MG // (3 * D_MODEL)), row(RET_V), row(POOL_WIDTH),
                  row(NSA_WIDTH), row(NSA_WIDTH), row(NSA_WIDTH),
                  row(LANES, C_NBG // LANES), row(NSA_WIDTH, C_NG // NSA_WIDTH),
                  const(wa), const(wb), const(wc), const(wo), const(fg)],
        out_specs=row(D_MODEL),
        compiler_params=_cparams(("parallel",)),
        name="merge",
    )(x2, z, ga, gb, o_cmp, o_sel, o_win, z, z, wa, wb, wc, wo, fg)


def _kv_rows(z, n, t):
    kv = z[:, C_KV:C_KV + 6 * NSA_KV].reshape(n, t, 3, 2, NSA_KV_HEADS, NSA_DH)
    return kv[:, :, 0], kv[:, :, 1], kv[:, :, 2]


def _slots_minor(cache):
    d, x, slots = cache.shape[:3]
    return cache.transpose(0, 1, 3, 4, 5, 2).reshape(d, x, 2 * NSA_KV, slots)


def _slots_major(rows_t):
    d, n, _, slots = rows_t.shape
    return rows_t.reshape(d, n, 2, NSA_KV_HEADS, NSA_DH, slots).transpose(0, 1, 5, 2, 3, 4)


def kernel(x_prompt, x_sample, state_ret, state_pool, cache_win, cache_cmp, cache_sel, page_table, norm_gain, w_in, w_pool, pool_scale, pe_cmp, w_ck, w_cv, w_br_a, w_br_b, w_br_c, w_out, rel_bias, final_gain):
    b, seq, _ = x_prompt.shape
    nb, dseq, _ = x_sample.shape
    depth = w_in.shape[0]
    past = page_table.shape[1] * PAGE_SIZE
    pt = page_table.astype(I32)

    w_in_p = _pack_w_in(w_in)
    w_pool_b = w_pool.astype(BF16)
    wa, wb, wc, wo = (w.astype(BF16) for w in (w_br_a, w_br_b, w_br_c, w_out))
    eye = jnp.eye(NSA_KV_HEADS, dtype=F32)
    bd_ck = jax.vmap(lambda w: jnp.kron(eye, w))(w_ck).astype(BF16)
    bd_cv = jax.vmap(lambda w: jnp.kron(eye, w))(w_cv).astype(BF16)
    pe2 = pe_cmp.reshape(depth, L_CMP, 2 * NSA_KV)
    tab = _bias_table(rel_bias)
    cmp_t, sel_t, win_t = _slots_minor(cache_cmp), _slots_minor(cache_sel), _slots_minor(cache_win)

    ret_chunk = 256 if seq % 256 == 0 else seq
    pool_tt = min(1024, seq)
    samp_nseq = 8 if nb % 8 == 0 else 1
    samp_nb = 16 if nb % 16 == 0 else 1

    hp = x_prompt.reshape(b * seq, D_MODEL)
    hs = x_sample.reshape(nb * dseq, D_MODEL)
    zero_state = jnp.zeros((b, RET_HEADS, RET_DK, RET_DV), F32)
    ret_p, pool_p, pool_s, cmp_s, sel_s = [], [], [], [], []
    row_bufs = ret_s = win_s = None
    for l in range(depth):
        final = l == depth - 1
        z = _inproj(hp, norm_gain[l], w_in_p[l])
        ga, s_new = _retention(z, zero_state, n=b, t=seq, pos0=0, chunk=ret_chunk, nseq=1)
        u = z[:, C_PU:C_PU + POOL_WIDTH].reshape(b, seq, POOL_WIDTH)
        nbt = seq // pool_tt
        halo = u.reshape(b, nbt, pool_tt, POOL_WIDTH)[:, :, pool_tt - 16:, :]
        halo = jnp.concatenate([jnp.zeros((b, 1, 16, POOL_WIDTH), F32), halo[:, :nbt - 1]], axis=1)
        gb = _pool(z, halo.reshape(b * nbt, 16, POOL_WIDTH), w_pool_b[l], pool_scale[l], n=b, t=seq, pos0=0, tt=pool_tt, nb=1)
        o_cmp, o_sel, o_win, row_bufs = _nsa_prompt(z, pe2[l], bd_ck[l], bd_cv[l], tab, n=b, t=seq, layer=l, depth=depth,
                                                    row_bufs=row_bufs)
        hp = _merge(hp, z, ga, gb, o_cmp, o_sel, o_win, wa[l], wb[l], wc[l], wo[l], final_gain, final=final)
        ret_p.append(s_new)
        pool_p.append(u[:, seq - POOL_BUF:])

        z = _inproj(hs, norm_gain[l], w_in_p[l])
        ga, ret_s = _retention(z, state_ret, n=nb, t=dseq, pos0=past, chunk=dseq, nseq=samp_nseq, layer=l, s_buf=ret_s)
        u = z[:, C_PU:C_PU + POOL_WIDTH].reshape(nb, dseq, POOL_WIDTH)
        ext = jnp.concatenate([state_pool[l], u], axis=1)
        halo = jnp.pad(state_pool[l], ((0, 0), (16 - POOL_BUF, 0), (0, 0)))
        gb = _pool(z, halo, w_pool_b[l], pool_scale[l], n=nb, t=dseq, pos0=past, tt=dseq, nb=samp_nb)
        o_cmp, o_sel, o_win, win_s = _nsa_sample(z, cmp_t, sel_t, win_t, pt, pe2[l], bd_ck[l], bd_cv[l], tab,
                                                 n=nb, t=dseq, past=past, layer=l, win_buf=win_s)
        cmp_new, sel_new, _ = _kv_rows(z, nb, dseq)
        hs = _merge(hs, z, ga, gb, o_cmp, o_sel, o_win, wa[l], wb[l], wc[l], wo[l], final_gain, final=final)
        pool_s.append(ext[:, ext.shape[1] - POOL_BUF:])
        cmp_s.append(cmp_new)
        sel_s.append(sel_new)

    y_prompt = hp.reshape(b, seq, D_MODEL)
    y_sample = hs.reshape(nb, dseq, D_MODEL)
    cmp_p, sel_p, win_p = row_bufs
    keep_p = min(WINDOW, seq)
    st = lambda xs: jnp.stack(xs)
    return (y_prompt, y_sample, st(ret_p), ret_s, st(pool_p), st(pool_s),
            _slots_major(win_p[:, :, :, seq - keep_p:]), _slots_major(win_s),
            _slots_major(cmp_p), st(cmp_s), _slots_major(sel_p), st(sel_s))
```
